```python
import math
import jax
import jax.numpy as jnp
from jax import lax
import numpy as np

D_MODEL = 1024
BATCH = 8
SEQ = 4096
DEPTH = 4

N_MIXERS = 2
N_MLA_LAYERS = (DEPTH + N_MIXERS - 1) // N_MIXERS
N_LRU_LAYERS = DEPTH // N_MIXERS

MLA_HEADS = 8
QK_NOPE_DIM = 128
QK_ROPE_DIM = 64
V_HEAD_DIM = 128
Q_LORA_RANK = 768
KV_LORA_RANK = 256
MLA_LATENT_DIM = Q_LORA_RANK + KV_LORA_RANK + QK_ROPE_DIM
ROPE_THETA = 10000.0
Q_BLOCK = 128

LRU_WIDTH = D_MODEL
LRU_BLOCKS = 4
LRU_BLOCK_DIM = LRU_WIDTH // LRU_BLOCKS
CONV_WIDTH = 4
LRU_C = 8.0

N_EXPERTS = 32
TOP_K = 4
D_FF = D_MODEL
SWIGLU_LIMIT = 7.0
SWIGLU_ALPHA = 1.702
MOE_BLOCK = 256

PLE_DIM = 256

DEEPNORM_ALPHA = (2.0 * DEPTH) ** 0.25
DEEPNORM_BETA = (8.0 * DEPTH) ** -0.25
LN_EPS = 1e-5
RMS_EPS = 1e-6

kernel_name = 'hybrid_mla_rglru_moe_deepnorm'


def _layer_norm(x, g, b):
    xf = x.astype(jnp.float32)
    mu = jnp.mean(xf, axis=-1, keepdims=True)
    xc = xf - mu
    var = jnp.mean(xc * xc, axis=-1, keepdims=True)
    return (xc * lax.rsqrt(var + LN_EPS) * g.astype(jnp.float32) + b.astype(jnp.float32)).astype(x.dtype)


def _rms_norm(x, g):
    xf = x.astype(jnp.float32)
    ms = jnp.mean(xf * xf, axis=-1, keepdims=True)
    return (xf * lax.rsqrt(ms + RMS_EPS) * g.astype(jnp.float32)).astype(x.dtype)


def _rope(x, positions):
    half = x.shape[-1] // 2
    inv_freq = jnp.exp(-math.log(ROPE_THETA) * jnp.arange(half, dtype=jnp.float32) / half)
    ang = positions.astype(jnp.float32)[..., None] * inv_freq
    cos = jnp.cos(ang)[:, :, None, :]
    sin = jnp.sin(ang)[:, :, None, :]
    xf = x.astype(jnp.float32)
    x1, x2 = xf[..., :half], xf[..., half:]
    return jnp.concatenate([x1 * cos - x2 * sin, x2 * cos + x1 * sin], axis=-1).astype(x.dtype)


def _mla(x, positions, w_in, q_norm, kv_norm, w_uq, w_ukv, w_o):
    bsz, seq, _ = x.shape
    lat = x @ w_in
    c_q = _rms_norm(lat[..., :Q_LORA_RANK], q_norm)
    c_kv = _rms_norm(lat[..., Q_LORA_RANK:Q_LORA_RANK + KV_LORA_RANK], kv_norm)
    k_rope = _rope(lat[..., Q_LORA_RANK + KV_LORA_RANK:][:, :, None, :], positions)[:, :, 0, :]
    q = (c_q @ w_uq).reshape(bsz, seq, MLA_HEADS, QK_NOPE_DIM + QK_ROPE_DIM)
    q_nope = q[..., :QK_NOPE_DIM]
    q_rope = _rope(q[..., QK_NOPE_DIM:], positions)
    kv = (c_kv @ w_ukv).reshape(bsz, seq, MLA_HEADS, QK_NOPE_DIM + V_HEAD_DIM)
    k_nope = kv[..., :QK_NOPE_DIM]
    v = kv[..., QK_NOPE_DIM:]

    n_blocks = seq // Q_BLOCK

    def to_blocks(t):
        return jnp.moveaxis(t.reshape(bsz, n_blocks, Q_BLOCK, *t.shape[2:]), 1, 0)

    scale = 1.0 / math.sqrt(QK_NOPE_DIM + QK_ROPE_DIM)
    key_pos = jnp.arange(seq, dtype=jnp.int32)

    def attend(args):
        qn, qr, blk = args
        s = (jnp.einsum('bqhd,bkhd->bhqk', qn, k_nope)
             + jnp.einsum('bqhr,bkr->bhqk', qr, k_rope)).astype(jnp.float32) * scale
        q_pos = blk * Q_BLOCK + jnp.arange(Q_BLOCK, dtype=jnp.int32)
        s = jnp.where(key_pos[None, :] <= q_pos[:, None], s, -jnp.inf)
        pr = jax.nn.softmax(s, axis=-1).astype(v.dtype)
        return jnp.einsum('bhqk,bkhd->bqhd', pr, v)

    o = lax.map(attend, (to_blocks(q_nope), to_blocks(q_rope), jnp.arange(n_blocks, dtype=jnp.int32)))
    o = jnp.moveaxis(o, 0, 1).reshape(bsz, seq, MLA_HEADS * V_HEAD_DIM)
    return o @ w_o


def _linear_combine(c1, c2):
    a1, b1 = c1
    a2, b2 = c2
    return a1 * a2, a2 * b1 + b2


def _rglru_block(x, w_in, conv_w, conv_b, w_a, b_a, w_x, b_x, lam, w_out):
    bsz, seq, _ = x.shape
    gu = x @ w_in
    gate, u = gu[..., :LRU_WIDTH], gu[..., LRU_WIDTH:]
    u = lax.conv_general_dilated(u, conv_w[:, None, :], window_strides=(1,),
                                 padding=[(CONV_WIDTH - 1, 0)],
                                 dimension_numbers=('NWC', 'WIO', 'NWC'),
                                 feature_group_count=LRU_WIDTH) + conv_b
    ub = u.reshape(bsz, seq, LRU_BLOCKS, LRU_BLOCK_DIM)
    r = jax.nn.sigmoid((jnp.einsum('bsnc,ncd->bsnd', ub, w_a).reshape(bsz, seq, LRU_WIDTH) + b_a).astype(jnp.float32))
    i = jax.nn.sigmoid((jnp.einsum('bsnc,ncd->bsnd', ub, w_x).reshape(bsz, seq, LRU_WIDTH) + b_x).astype(jnp.float32))
    log_a = -LRU_C * r * jax.nn.softplus(-lam.astype(jnp.float32))
    a = jnp.exp(log_a)
    mult = jnp.sqrt(-jnp.expm1(2.0 * log_a))
    b_in = mult * i * u.astype(jnp.float32)
    _, h = lax.associative_scan(_linear_combine, (a, b_in), axis=1)
    y = jax.nn.gelu(gate, approximate=True) * h.astype(x.dtype)
    return y @ w_out


def _moe(x, w_router, b_router, w_up, b_up, w_down, b_down):
    bsz, seq, d = x.shape
    t = bsz * seq
    xf = x.reshape(t, d)
    logits = (xf @ w_router + b_router).astype(jnp.float32)
    top_logits, top_idx = lax.top_k(logits, TOP_K)
    gates = jax.nn.softmax(top_logits, axis=-1)
    n = t * TOP_K
    flat_e = top_idx.reshape(n).astype(jnp.int32)
    flat_tok = jnp.arange(n, dtype=jnp.int32) // TOP_K
    flat_g = gates.reshape(n)
    order = jnp.argsort(flat_e)
    sorted_e = flat_e[order]
    counts = jnp.zeros((N_EXPERTS,), jnp.int32).at[flat_e].add(1)
    padded = (counts + MOE_BLOCK - 1) // MOE_BLOCK * MOE_BLOCK
    starts = jnp.cumsum(counts) - counts
    pad_ends = jnp.cumsum(padded)
    pad_starts = pad_ends - padded
    rank = jnp.arange(n, dtype=jnp.int32) - starts[sorted_e]
    dest = pad_starts[sorted_e] + rank
    n_pad = n + N_EXPERTS * MOE_BLOCK
    n_blocks = n_pad // MOE_BLOCK
    row_tok = jnp.zeros((n_pad,), jnp.int32).at[dest].set(flat_tok[order])
    row_gate = jnp.zeros((n_pad,), x.dtype).at[dest].set(flat_g[order].astype(x.dtype))
    block_e = jnp.searchsorted(pad_ends, jnp.arange(n_blocks, dtype=jnp.int32) * MOE_BLOCK, side='right')
    block_e = jnp.minimum(block_e, N_EXPERTS - 1).astype(jnp.int32)
    xs = xf[row_tok].reshape(n_blocks, MOE_BLOCK, d)

    def expert_block(args):
        xb, e = args
        hb = xb @ w_up[e] + b_up[e]
        g = jnp.minimum(hb[:, :D_FF], SWIGLU_LIMIT)
        up = jnp.clip(hb[:, D_FF:], -SWIGLU_LIMIT, SWIGLU_LIMIT)
        yb = (up + 1.0) * (g * jax.nn.sigmoid(SWIGLU_ALPHA * g))
        return yb @ w_down[e] + b_down[e]

    ys = lax.map(expert_block, (xs, block_e)).reshape(n_pad, d)
    out = jax.ops.segment_sum(ys * row_gate[:, None], row_tok, num_segments=t)
    return out.reshape(bsz, seq, d)


def setup_inputs(seed: int = 0) -> dict:
    key = jax.random.key(seed)
    ks = jax.random.split(key, 30)
    na, nl = N_MLA_LAYERS, N_LRU_LAYERS
    f32 = jnp.float32

    def nrm(k, shape, scale):
        return jax.random.normal(k, shape, f32) * scale

    x = nrm(ks[0], (BATCH, SEQ, D_MODEL), 1.0)
    p = nrm(ks[1], (DEPTH, BATCH, SEQ, PLE_DIM), 1.0)
    positions = (jax.random.randint(ks[2], (BATCH, 1), 0, 1024, dtype=jnp.int32)
                 + jnp.arange(SEQ, dtype=jnp.int32)[None, :])
    mla_w_in = nrm(ks[3], (na, D_MODEL, MLA_LATENT_DIM), D_MODEL ** -0.5)
    mla_q_norm = 1.0 + nrm(ks[4], (na, Q_LORA_RANK), 0.02)
    mla_kv_norm = 1.0 + nrm(ks[5], (na, KV_LORA_RANK), 0.02)
    mla_w_uq = nrm(ks[6], (na, Q_LORA_RANK, MLA_HEADS * (QK_NOPE_DIM + QK_ROPE_DIM)), Q_LORA_RANK ** -0.5)
    mla_w_ukv = nrm(ks[7], (na, KV_LORA_RANK, MLA_HEADS * (QK_NOPE_DIM + V_HEAD_DIM)), KV_LORA_RANK ** -0.5)
    mla_w_o = nrm(ks[8], (na, MLA_HEADS * V_HEAD_DIM, D_MODEL), DEEPNORM_BETA * (MLA_HEADS * V_HEAD_DIM) ** -0.5)
    lru_w_in = nrm(ks[9], (nl, D_MODEL, 2 * LRU_WIDTH), D_MODEL ** -0.5)
    lru_conv_w = nrm(ks[10], (nl, CONV_WIDTH, LRU_WIDTH), CONV_WIDTH ** -0.5)
    lru_conv_b = nrm(ks[11], (nl, LRU_WIDTH), 0.01)
    lru_w_a = nrm(ks[12], (nl, LRU_BLOCKS, LRU_BLOCK_DIM, LRU_BLOCK_DIM), LRU_BLOCK_DIM ** -0.5)
    lru_b_a = nrm(ks[13], (nl, LRU_WIDTH), 0.01)
    lru_w_x = nrm(ks[14], (nl, LRU_BLOCKS, LRU_BLOCK_DIM, LRU_BLOCK_DIM), LRU_BLOCK_DIM ** -0.5)
    lru_b_x = nrm(ks[15], (nl, LRU_WIDTH), 0.01)
    a_c = jax.random.uniform(ks[16], (nl, LRU_WIDTH), f32, 0.9, 0.999)
    sig = a_c ** (1.0 / LRU_C)
    lru_lambda = jnp.log(sig) - jnp.log1p(-sig)
    lru_w_out = nrm(ks[17], (nl, LRU_WIDTH, D_MODEL), DEEPNORM_BETA * LRU_WIDTH ** -0.5)
    ln1_g = 1.0 + nrm(ks[18], (DEPTH, D_MODEL), 0.02)
    ln1_b = nrm(ks[19], (DEPTH, D_MODEL), 0.02)
    ln2_g = 1.0 + nrm(ks[20], (DEPTH, D_MODEL), 0.02)
    ln2_b = nrm(ks[21], (DEPTH, D_MODEL), 0.02)
    moe_w_router = nrm(ks[22], (DEPTH, D_MODEL, N_EXPERTS), D_MODEL ** -0.5)
    moe_b_router = nrm(ks[23], (DEPTH, N_EXPERTS), 0.01)
    moe_w_up = nrm(ks[24], (DEPTH, N_EXPERTS, D_MODEL, 2 * D_FF), D_MODEL ** -0.5)
    moe_b_up = nrm(ks[25], (DEPTH, N_EXPERTS, 2 * D_FF), 0.01)
    moe_w_down = nrm(ks[26], (DEPTH, N_EXPERTS, D_FF, D_MODEL), DEEPNORM_BETA * D_FF ** -0.5)
    moe_b_down = nrm(ks[27], (DEPTH, N_EXPERTS, D_MODEL), 0.01)
    ple_w_gate = nrm(ks[28], (DEPTH, D_MODEL, D_MODEL), D_MODEL ** -0.5)
    ple_w_proj = nrm(ks[29], (DEPTH, PLE_DIM, D_MODEL), DEEPNORM_BETA * PLE_DIM ** -0.5)
    return {
        'x': x, 'p': p, 'positions': positions,
        'mla_w_in': mla_w_in, 'mla_q_norm': mla_q_norm, 'mla_kv_norm': mla_kv_norm,
        'mla_w_uq': mla_w_uq, 'mla_w_ukv': mla_w_ukv, 'mla_w_o': mla_w_o,
        'lru_w_in': lru_w_in, 'lru_conv_w': lru_conv_w, 'lru_conv_b': lru_conv_b,
        'lru_w_a': lru_w_a, 'lru_b_a': lru_b_a, 'lru_w_x': lru_w_x, 'lru_b_x': lru_b_x,
        'lru_lambda': lru_lambda, 'lru_w_out': lru_w_out,
        'ln1_g': ln1_g, 'ln1_b': ln1_b, 'ln2_g': ln2_g, 'ln2_b': ln2_b,
        'moe_w_router': moe_w_router, 'moe_b_router': moe_b_router,
        'moe_w_up': moe_w_up, 'moe_b_up': moe_b_up, 'moe_w_down': moe_w_down, 'moe_b_down': moe_b_down,
        'ple_w_gate': ple_w_gate, 'ple_w_proj': ple_w_proj,
    }


def reference(x, p, positions,
              mla_w_in, mla_q_norm, mla_kv_norm, mla_w_uq, mla_w_ukv, mla_w_o,
              lru_w_in, lru_conv_w, lru_conv_b, lru_w_a, lru_b_a, lru_w_x, lru_b_x,
              lru_lambda, lru_w_out,
              ln1_g, ln1_b, ln2_g, ln2_b,
              moe_w_router, moe_b_router, moe_w_up, moe_b_up, moe_w_down, moe_b_down,
              ple_w_gate, ple_w_proj):
    for layer in range(DEPTH):
        j = layer // N_MIXERS
        if layer % N_MIXERS == 0:
            mix = _mla(x, positions, mla_w_in[j], mla_q_norm[j], mla_kv_norm[j],
                       mla_w_uq[j], mla_w_ukv[j], mla_w_o[j])
        else:
            mix = _rglru_block(x, lru_w_in[j], lru_conv_w[j], lru_conv_b[j], lru_w_a[j], lru_b_a[j],
                               lru_w_x[j], lru_b_x[j], lru_lambda[j], lru_w_out[j])
        x = _layer_norm(DEEPNORM_ALPHA * x + mix, ln1_g[layer], ln1_b[layer])
        ffn = _moe(x, moe_w_router[layer], moe_b_router[layer], moe_w_up[layer], moe_b_up[layer],
                   moe_w_down[layer], moe_b_down[layer])
        x = _layer_norm(DEEPNORM_ALPHA * x + ffn, ln2_g[layer], ln2_b[layer])
        x = x + jax.nn.sigmoid(x @ ple_w_gate[layer]) * (p[layer] @ ple_w_proj[layer])
    return x
```

```python
import functools
import math

import jax
import jax.numpy as jnp
from jax import lax
from jax.experimental import pallas as pl
from jax.experimental.pallas import tpu as pltpu

F32 = jnp.float32
BF16 = jnp.bfloat16

MLA_HEADS = 8
QK_NOPE_DIM = 128
QK_ROPE_DIM = 64
V_HEAD_DIM = 128
ROPE_THETA = 10000.0
LRU_BLOCKS = 4
CONV_WIDTH = 4
LRU_C = 8.0
N_EXPERTS = 32
TOP_K = 4
SWIGLU_LIMIT = 7.0
SWIGLU_ALPHA = 1.702
LN_EPS = 1e-5
RMS_EPS = 1e-6

LANES = 128
SUBLANES = 8
QK_PAD = 2 * LANES
VMEM_LIMIT_BYTES = 56 * 1024 * 1024


def _params(*sem):
    return pltpu.CompilerParams(dimension_semantics=sem, vmem_limit_bytes=VMEM_LIMIT_BYTES)


def _dot(a, b):
    return jnp.dot(a, b, preferred_element_type=F32)


def _sigmoid(z):
    return 1.0 / (1.0 + jnp.exp(-z))


def _layer_norm(y, g, b):
    mu = jnp.mean(y, axis=-1, keepdims=True)
    yc = y - mu
    var = jnp.mean(yc * yc, axis=-1, keepdims=True)
    return yc * lax.rsqrt(var + LN_EPS) * g + b


def _rms_norm(y, g):
    ms = jnp.mean(y * y, axis=-1, keepdims=True)
    return y * lax.rsqrt(ms + RMS_EPS) * g


def _split_bf16(a):
    hi = a.astype(BF16)
    lo = (a - hi.astype(F32)).astype(BF16)
    return hi, lo


def _router_logits(x1, wr_hi, wr_lo, br):
    hi, lo = _split_bf16(x1)
    return _dot(hi, wr_hi) + (_dot(hi, wr_lo) + _dot(lo, wr_hi)) + br


def _full(shape):
    return pl.BlockSpec(shape, lambda *_: (0,) * len(shape))


def _rope(blk, cos, sin_a, sin_b):
    return (blk * cos + pltpu.roll(blk, LANES - QK_ROPE_DIM // 2, 1) * sin_a
            + pltpu.roll(blk, QK_ROPE_DIM // 2, 1) * sin_b)


def _mla_proj_kernel(x_ref, cos_ref, sina_ref, sinb_ref, win_ref, qn_ref, kvn_ref, wuq_ref, wukv_ref,
                     q_ref, k_ref, v_ref, *, q_lora, kv_lora):
    lat = _dot(x_ref[...].astype(BF16), win_ref[...])
    c_q = _rms_norm(lat[:, :q_lora], qn_ref[...])
    c_kv = _rms_norm(lat[:, q_lora:q_lora + kv_lora], kvn_ref[...])
    cos, sin_a, sin_b = cos_ref[...], sina_ref[...], sinb_ref[...]
    k_rope = _rope(lat[:, q_lora + kv_lora:], cos, sin_a, sin_b).astype(BF16)
    q = _dot(c_q.astype(BF16), wuq_ref[...])
    kv = _dot(c_kv.astype(BF16), wukv_ref[...])
    for h in range(MLA_HEADS):
        lo = h * QK_PAD
        mid = lo + LANES
        hi = lo + QK_PAD
        q_ref[:, lo:mid] = q[:, lo:mid].astype(BF16)
        q_ref[:, mid:hi] = _rope(q[:, mid:hi], cos, sin_a, sin_b).astype(BF16)
        k_ref[:, lo:mid] = kv[:, lo:mid].astype(BF16)
        k_ref[:, mid:hi] = k_rope
        v_ref[:, h * V_HEAD_DIM:(h + 1) * V_HEAD_DIM] = kv[:, mid:hi].astype(BF16)


def _mla_proj(x, cos, sin_a, sin_b, win_p, qn, kvn, wuq_p, wukv, tm):
    t, d = x.shape
    q_lora, kv_lora = qn.shape[1], kvn.shape[1]
    hq = MLA_HEADS * QK_PAD
    hv = MLA_HEADS * V_HEAD_DIM
    row = lambda c: pl.BlockSpec((tm, c), lambda i: (i, 0))
    return pl.pallas_call(
        functools.partial(_mla_proj_kernel, q_lora=q_lora, kv_lora=kv_lora),
        grid=(t // tm,),
        in_specs=[row(d), row(LANES), row(LANES), row(LANES), _full(win_p.shape), _full(qn.shape),
                  _full(kvn.shape), _full(wuq_p.shape), _full(wukv.shape)],
        out_specs=[row(hq), row(hq), row(hv)],
        out_shape=[jax.ShapeDtypeStruct((t, hq), BF16), jax.ShapeDtypeStruct((t, hq), BF16),
                   jax.ShapeDtypeStruct((t, hv), BF16)],
        compiler_params=_params("parallel"),
        name="mla_proj",
    )(x, cos, sin_a, sin_b, win_p, qn, kvn, wuq_p, wukv)


def _attn_kernel(q_ref, k_ref, v_ref, o_ref, m_sc, l_sc, acc_sc, *, blk):
    qi = pl.program_id(1)
    q = q_ref[...]
    m_sc[...] = jnp.full(m_sc.shape, -jnp.inf, F32)
    l_sc[...] = jnp.zeros(l_sc.shape, F32)
    acc_sc[...] = jnp.zeros(acc_sc.shape, F32)

    def step(ki, diagonal):
        start = pl.multiple_of(ki * blk, blk)
        k = k_ref[pl.ds(start, blk), :]
        v = v_ref[pl.ds(start, blk), :]
        s = lax.dot_general(q, k, (((1,), (1,)), ((), ())), preferred_element_type=F32)
        if diagonal:
            rows = lax.broadcasted_iota(jnp.int32, s.shape, 0)
            cols = lax.broadcasted_iota(jnp.int32, s.shape, 1)
            s = jnp.where(cols <= rows, s, -jnp.inf)
        m_prev = m_sc[...]
        m_new = jnp.maximum(m_prev, jnp.max(s, axis=1, keepdims=True))
        alpha = jnp.exp(m_prev - m_new)
        p = jnp.exp(s - m_new)
        l_sc[...] = alpha * l_sc[...] + jnp.sum(p, axis=1, keepdims=True)
        acc_sc[...] = alpha * acc_sc[...] + _dot(p.astype(BF16), v)
        m_sc[...] = m_new

    def body(ki, carry):
        step(ki, False)
        return carry

    lax.fori_loop(0, qi, body, 0)
    step(qi, True)
    o_ref[...] = (acc_sc[...] / l_sc[...]).astype(o_ref.dtype)


def _attention(q, k, v, seq, batch, blk):
    bh = batch * MLA_HEADS
    return pl.pallas_call(
        functools.partial(_attn_kernel, blk=blk),
        grid=(bh, seq // blk),
        in_specs=[pl.BlockSpec((blk, QK_PAD), lambda g, i: (i, g)),
                  pl.BlockSpec((seq, QK_PAD), lambda g, i: (0, g)),
                  pl.BlockSpec((seq, V_HEAD_DIM), lambda g, i: (0, g))],
        out_specs=pl.BlockSpec((blk, V_HEAD_DIM), lambda g, i: (i, g)),
        out_shape=jax.ShapeDtypeStruct((seq, bh * V_HEAD_DIM), BF16),
        scratch_shapes=[pltpu.VMEM((blk, 1), F32), pltpu.VMEM((blk, 1), F32),
                        pltpu.VMEM((blk, V_HEAD_DIM), F32)],
        compiler_params=_params("parallel", "arbitrary"),
        name="mla_attention",
    )(q, k, v)


def _mix_epilogue(x, mix, g_ref, b_ref, wrh_ref, wrl_ref, br_ref, x1_ref, lg_ref, alpha):
    x1 = _layer_norm(alpha * x + mix, g_ref[...], b_ref[...])
    x1_ref[...] = x1
    lg_ref[...] = _router_logits(x1, wrh_ref[...], wrl_ref[...], br_ref[...])


def _post_attn_kernel(x_ref, o_ref, wo_ref, g_ref, b_ref, wrh_ref, wrl_ref, br_ref, x1_ref, lg_ref, *, alpha):
    mix = _dot(o_ref[...], wo_ref[...])
    _mix_epilogue(x_ref[...], mix, g_ref, b_ref, wrh_ref, wrl_ref, br_ref, x1_ref, lg_ref, alpha)


def _post_attn(x, o, wo, g, b, wr_hi, wr_lo, br, alpha, tm):
    t, d = x.shape
    row = lambda c: pl.BlockSpec((tm, c), lambda i: (i, 0))
    return pl.pallas_call(
        functools.partial(_post_attn_kernel, alpha=alpha),
        grid=(t // tm,),
        in_specs=[row(d), row(o.shape[1]), _full(wo.shape), _full(g.shape), _full(b.shape),
                  _full(wr_hi.shape), _full(wr_lo.shape), _full(br.shape)],
        out_specs=[row(d), row(LANES)],
        out_shape=[jax.ShapeDtypeStruct((t, d), F32), jax.ShapeDtypeStruct((t, LANES), F32)],
        compiler_params=_params("parallel"),
        name="post_attention",
    )(x, o, wo, g, b, wr_hi, wr_lo, br)


def _lru_kernel(x_ref, win_ref, cw_ref, cb_ref, wa_ref, ba_ref, wx_ref, bx_ref, lam_ref, wout_ref,
                g_ref, b_ref, wrh_ref, wrl_ref, br_ref, x1_ref, lg_ref,
                ucarry_sc, hcarry_sc, a_sc, b_sc, h_sc, *, alpha, batch):
    tm, width = a_sc.shape
    halo = (CONV_WIDTH - 1) * batch
    blk_w = width // LRU_BLOCKS

    @pl.when(pl.program_id(0) == 0)
    def _():
        ucarry_sc[...] = jnp.zeros(ucarry_sc.shape, F32)
        hcarry_sc[...] = jnp.zeros(hcarry_sc.shape, F32)

    x = x_ref[...]
    gu = _dot(x.astype(BF16), win_ref[...])
    gate = gu[:, :width]
    u = gu[:, width:]
    u_ext = jnp.concatenate([ucarry_sc[...], u], axis=0)
    ucarry_sc[...] = u[tm - halo:, :]
    cw = cw_ref[...]
    uc = cb_ref[...] + cw[0:1, :] * u_ext[0:tm, :]
    for j in range(1, CONV_WIDTH):
        uc = uc + cw[j:j + 1, :] * u_ext[j * batch:j * batch + tm, :]
    ucb = uc.astype(BF16)
    ra = jnp.concatenate([_dot(ucb[:, n * blk_w:(n + 1) * blk_w], wa_ref[n]) for n in range(LRU_BLOCKS)], axis=1)
    rx = jnp.concatenate([_dot(ucb[:, n * blk_w:(n + 1) * blk_w], wx_ref[n]) for n in range(LRU_BLOCKS)], axis=1)
    r = _sigmoid(ra + ba_ref[...])
    gi = _sigmoid(rx + bx_ref[...])
    z = -lam_ref[...]
    softplus = jnp.maximum(z, 0.0) + jnp.log1p(jnp.exp(-jnp.abs(z)))
    log_a = (-LRU_C) * r * softplus
    a_sc[...] = jnp.exp(log_a)
    th = jnp.tanh(log_a)
    b_sc[...] = jnp.sqrt(-2.0 * th / (1.0 - th)) * gi * uc

    def body(t, h):
        rows = pl.ds(pl.multiple_of(t * batch, batch), batch)
        h = a_sc[rows, :] * h + b_sc[rows, :]
        h_sc[rows, :] = h
        return h

    hcarry_sc[...] = lax.fori_loop(0, tm // batch, body, hcarry_sc[...], unroll=8)
    c = math.sqrt(2.0 / math.pi)
    gelu = 0.5 * gate * (1.0 + jnp.tanh(c * (gate + 0.044715 * (gate * gate * gate))))
    y = gelu * h_sc[...]
    mix = _dot(y.astype(BF16), wout_ref[...])
    _mix_epilogue(x, mix, g_ref, b_ref, wrh_ref, wrl_ref, br_ref, x1_ref, lg_ref, alpha)


def _lru_block(x, win, cw, cb, wa, ba, wx, bx, lam, wout, g, b, wr_hi, wr_lo, br, alpha, batch, tm):
    t, d = x.shape
    width = wout.shape[0]
    row = lambda c: pl.BlockSpec((tm, c), lambda i: (i, 0))
    consts = (win, cw, cb, wa, ba, wx, bx, lam, wout, g, b, wr_hi, wr_lo, br)
    return pl.pallas_call(
        functools.partial(_lru_kernel, alpha=alpha, batch=batch),
        grid=(t // tm,),
        in_specs=[row(d)] + [_full(c.shape) for c in consts],
        out_specs=[row(d), row(LANES)],
        out_shape=[jax.ShapeDtypeStruct((t, d), F32), jax.ShapeDtypeStruct((t, LANES), F32)],
        scratch_shapes=[pltpu.VMEM(((CONV_WIDTH - 1) * batch, width), F32), pltpu.VMEM((batch, width), F32),
                        pltpu.VMEM((tm, width), F32), pltpu.VMEM((tm, width), F32), pltpu.VMEM((tm, width), F32)],
        compiler_params=_params("arbitrary"),
        name="rglru_block",
    )(x, *consts)


def _expert_kernel(be_ref, nb_ref, xs_ref, wup_ref, bup_ref, wdn_ref, bdn_ref, ys_ref):
    d_ff = wdn_ref.shape[1]

    @pl.when(pl.program_id(0) < nb_ref[0])
    def _():
        hb = _dot(xs_ref[...], wup_ref[0]) + bup_ref[0]
        gl = jnp.minimum(hb[:, :d_ff], SWIGLU_LIMIT)
        up = jnp.clip(hb[:, d_ff:], -SWIGLU_LIMIT, SWIGLU_LIMIT)
        yb = (up + 1.0) * (gl * _sigmoid(SWIGLU_ALPHA * gl))
        ys_ref[...] = _dot(yb.astype(BF16), wdn_ref[0]) + bdn_ref[0]


def _experts(block_e, n_blocks_used, xs, w_up, b_up, w_down, b_down, bm):
    n_pad, d = xs.shape
    e, _, f2 = w_up.shape
    blk = lambda j, be, nb: jnp.minimum(j, nb[0] - 1)
    exp = lambda j, be, nb: be[jnp.minimum(j, nb[0] - 1)]
    grid_spec = pltpu.PrefetchScalarGridSpec(
        num_scalar_prefetch=2,
        grid=(n_pad // bm,),
        in_specs=[pl.BlockSpec((bm, d), lambda j, be, nb: (blk(j, be, nb), 0)),
                  pl.BlockSpec((1, d, f2), lambda j, be, nb: (exp(j, be, nb), 0, 0)),
                  pl.BlockSpec((1, 1, f2), lambda j, be, nb: (exp(j, be, nb), 0, 0)),
                  pl.BlockSpec((1, f2 // 2, d), lambda j, be, nb: (exp(j, be, nb), 0, 0)),
                  pl.BlockSpec((1, 1, d), lambda j, be, nb: (exp(j, be, nb), 0, 0))],
        out_specs=pl.BlockSpec((bm, d), lambda j, be, nb: (blk(j, be, nb), 0)),
    )
    return pl.pallas_call(
        _expert_kernel,
        grid_spec=grid_spec,
        out_shape=jax.ShapeDtypeStruct((n_pad, d), F32),
        compiler_params=_params("arbitrary"),
        name="moe_experts",
    )(block_e, n_blocks_used, xs, w_up, b_up, w_down, b_down)


def _post_moe_kernel(x1_ref, ffn_ref, p_ref, g_ref, b_ref, wg_ref, wp_ref, out_ref, *, alpha):
    x2 = _layer_norm(alpha * x1_ref[...] + ffn_ref[...], g_ref[...], b_ref[...])
    gate = _sigmoid(_dot(x2.astype(BF16), wg_ref[...]))
    out_ref[...] = x2 + gate * _dot(p_ref[...].astype(BF16), wp_ref[...])


def _post_moe(x1, ffn, p, g, b, wg, wp, alpha, tm):
    t, d = x1.shape
    row = lambda c: pl.BlockSpec((tm, c), lambda i: (i, 0))
    return pl.pallas_call(
        functools.partial(_post_moe_kernel, alpha=alpha),
        grid=(t // tm,),
        in_specs=[row(d), row(d), row(p.shape[1]), _full(g.shape), _full(b.shape), _full(wg.shape),
                  _full(wp.shape)],
        out_specs=row(d),
        out_shape=jax.ShapeDtypeStruct((t, d), F32),
        compiler_params=_params("parallel"),
        name="post_moe_ple",
    )(x1, ffn, p, g, b, wg, wp)


def _moe(x1, logits, w_up, b_up, w_down, b_down, bm):
    t, d = x1.shape
    top_logits, top_idx = lax.top_k(logits[:, :N_EXPERTS], TOP_K)
    gates = jax.nn.softmax(top_logits, axis=-1)
    n = t * TOP_K
    flat_e = top_idx.reshape(n).astype(jnp.int32)
    order = jnp.argsort(flat_e)
    sorted_e = flat_e[order]
    counts = jnp.zeros((N_EXPERTS,), jnp.int32).at[flat_e].add(1)
    padded = (counts + bm - 1) // bm * bm
    starts = jnp.cumsum(counts) - counts
    pad_ends = jnp.cumsum(padded)
    pad_starts = pad_ends - padded
    rank = jnp.arange(n, dtype=jnp.int32) - starts[sorted_e]
    dest_sorted = pad_starts[sorted_e] + rank
    n_pad = n + N_EXPERTS * bm
    n_blocks = n_pad // bm
    row_tok = jnp.zeros((n_pad,), jnp.int32).at[dest_sorted].set((order // TOP_K).astype(jnp.int32))
    dest = jnp.zeros((n,), jnp.int32).at[order].set(dest_sorted)
    block_e = jnp.searchsorted(pad_ends, jnp.arange(n_blocks, dtype=jnp.int32) * bm, side='right')
    block_e = jnp.minimum(block_e, N_EXPERTS - 1).astype(jnp.int32)
    n_used = (pad_ends[-1] // bm).astype(jnp.int32).reshape(1)
    xs = x1.astype(BF16)[row_tok]
    ys = _experts(block_e, n_used, xs, w_up, b_up, w_down, b_down, bm)
    picked = ys[dest].reshape(t, TOP_K, d)
    return jnp.sum(picked * gates[:, :, None], axis=1)


def _pick_tile(n, target, quantum):
    tile = min(n, target)
    while n % tile or tile % quantum:
        tile -= quantum
    return tile


def kernel(x, p, positions, mla_w_in, mla_q_norm, mla_kv_norm, mla_w_uq, mla_w_ukv, mla_w_o, lru_w_in, lru_conv_w, lru_conv_b, lru_w_a, lru_b_a, lru_w_x, lru_b_x, lru_lambda, lru_w_out, ln1_g, ln1_b, ln2_g, ln2_b, moe_w_router, moe_b_router, moe_w_up, moe_b_up, moe_w_down, moe_b_down, ple_w_gate, ple_w_proj):
    batch, seq, d = x.shape
    depth = ln1_g.shape[0]
    t = batch * seq
    assert batch == SUBLANES, "the recurrence kernel maps the batch onto the sublanes of a vreg"
    alpha = (2.0 * depth) ** 0.25
    tm = _pick_tile(t, 512, SUBLANES * batch)
    attn_blk = _pick_tile(seq, 512, LANES)
    moe_bm = 512

    xt = jnp.transpose(x, (1, 0, 2)).reshape(t, d)
    pt = jnp.transpose(p, (0, 2, 1, 3)).reshape(depth, t, p.shape[-1])

    half = QK_ROPE_DIM // 2
    inv_freq = jnp.exp(-math.log(ROPE_THETA) * jnp.arange(half, dtype=F32) / half)
    ang = jnp.transpose(positions).reshape(t, 1).astype(F32) * inv_freq
    cos, sin = jnp.cos(ang), jnp.sin(ang)
    zeros = jnp.zeros_like(cos)
    rope_cos = jnp.concatenate([cos, cos, zeros, zeros], axis=1)
    rope_sin_a = jnp.concatenate([-sin, zeros, zeros, zeros], axis=1)
    rope_sin_b = jnp.concatenate([zeros, sin, zeros, zeros], axis=1)

    row2 = lambda a: a.reshape(1, -1)
    scale = 1.0 / math.sqrt(QK_NOPE_DIM + QK_ROPE_DIM)

    for layer in range(depth):
        j = layer // 2
        wr = jnp.pad(moe_w_router[layer], ((0, 0), (0, LANES - N_EXPERTS)))
        wr_hi, wr_lo = _split_bf16(wr)
        br = row2(jnp.pad(moe_b_router[layer], (0, LANES - N_EXPERTS)))
        g1, b1 = row2(ln1_g[layer]), row2(ln1_b[layer])
        if layer % 2 == 0:
            q_lora, kv_lora = mla_q_norm.shape[1], mla_kv_norm.shape[1]
            w_in = mla_w_in[j]
            win_p = jnp.pad(w_in, ((0, 0), (0, LANES - QK_ROPE_DIM))).astype(BF16)
            wuq = mla_w_uq[j].reshape(q_lora, MLA_HEADS, QK_NOPE_DIM + QK_ROPE_DIM)
            wuq_p = jnp.pad(wuq, ((0, 0), (0, 0), (0, QK_PAD - QK_NOPE_DIM - QK_ROPE_DIM)))
            wuq_p = wuq_p.reshape(q_lora, MLA_HEADS * QK_PAD).astype(BF16)
            q, k, v = _mla_proj(xt, rope_cos, rope_sin_a, rope_sin_b, win_p,
                                row2(mla_q_norm[j] * scale), row2(mla_kv_norm[j]),
                                wuq_p, mla_w_ukv[j].astype(BF16), tm)
            o = _attention(q.reshape(seq, -1), k.reshape(seq, -1), v.reshape(seq, -1), seq, batch, attn_blk)
            x1, logits = _post_attn(xt, o.reshape(t, -1), mla_w_o[j].astype(BF16), g1, b1,
                                    wr_hi, wr_lo, br, alpha, tm)
        else:
            x1, logits = _lru_block(
                xt, lru_w_in[j].astype(BF16), lru_conv_w[j], row2(lru_conv_b[j]),
                lru_w_a[j].astype(BF16), row2(lru_b_a[j]), lru_w_x[j].astype(BF16), row2(lru_b_x[j]),
                row2(lru_lambda[j]), lru_w_out[j].astype(BF16), g1, b1, wr_hi, wr_lo, br, alpha, batch, tm)
        ffn = _moe(x1, logits, moe_w_up[layer].astype(BF16), moe_b_up[layer][:, None, :],
                   moe_w_down[layer].astype(BF16), moe_b_down[layer][:, None, :], moe_bm)
        xt = _post_moe(x1, ffn, pt[layer], row2(ln2_g[layer]), row2(ln2_b[layer]),
                       ple_w_gate[layer].astype(BF16), ple_w_proj[layer].astype(BF16), alpha, tm)
    return jnp.transpose(xt.reshape(seq, batch, d), (1, 0, 2))
```

```python
import functools
import math

import jax
import jax.numpy as jnp
from jax import lax
from jax.experimental import pallas as pl
from jax.experimental.pallas import tpu as pltpu

F32 = jnp.float32
BF16 = jnp.bfloat16

MLA_HEADS = 8
QK_NOPE_DIM = 128
QK_ROPE_DIM = 64
V_HEAD_DIM = 128
ROPE_THETA = 10000.0
LRU_BLOCKS = 4
CONV_WIDTH = 4
LRU_C = 8.0
N_EXPERTS = 32
TOP_K = 4
SWIGLU_LIMIT = 7.0
SWIGLU_ALPHA = 1.702
LN_EPS = 1e-5
RMS_EPS = 1e-6

LANES = 128
SUBLANES = 8
QK_PAD = 2 * LANES
VMEM_LIMIT_BYTES = 56 * 1024 * 1024


def _params(*sem, flags=None):
    return pltpu.CompilerParams(dimension_semantics=sem, vmem_limit_bytes=VMEM_LIMIT_BYTES, flags=flags)


def _dot(a, b):
    return jnp.dot(a, b, preferred_element_type=F32)


def _sigmoid(z):
    return 1.0 / (1.0 + jnp.exp(-z))


def _layer_norm(y, g, b):
    mu = jnp.mean(y, axis=-1, keepdims=True)
    yc = y - mu
    var = jnp.mean(yc * yc, axis=-1, keepdims=True)
    return yc * lax.rsqrt(var + LN_EPS) * g + b


def _rms_norm(y, g):
    ms = jnp.mean(y * y, axis=-1, keepdims=True)
    return y * lax.rsqrt(ms + RMS_EPS) * g


def _split_bf16(a):
    hi = a.astype(BF16)
    lo = (a - hi.astype(F32)).astype(BF16)
    return hi, lo


def _router_logits(x1, wr_hi, wr_lo, br):
    hi, lo = _split_bf16(x1)
    return _dot(hi, wr_hi) + (_dot(hi, wr_lo) + _dot(lo, wr_hi)) + br


def _full(shape):
    return pl.BlockSpec(shape, lambda *_: (0,) * len(shape))


def _rope(blk, cos, sin_a, sin_b):
    return (blk * cos + pltpu.roll(blk, LANES - QK_ROPE_DIM // 2, 1) * sin_a
            + pltpu.roll(blk, QK_ROPE_DIM // 2, 1) * sin_b)


def _mla_proj_kernel(x_ref, cos_ref, sina_ref, sinb_ref, win_ref, qn_ref, kvn_ref, wuq_ref, wukv_ref,
                     q_ref, k_ref, vt_ref, *, q_lora, kv_lora):
    lat = _dot(x_ref[...].astype(BF16), win_ref[...])
    c_q = _rms_norm(lat[:, :q_lora], qn_ref[...])
    c_kv = _rms_norm(lat[:, q_lora:q_lora + kv_lora], kvn_ref[...])
    cos, sin_a, sin_b = cos_ref[...], sina_ref[...], sinb_ref[...]
    k_rope = _rope(lat[:, q_lora + kv_lora:], cos, sin_a, sin_b).astype(BF16)
    q = _dot(c_q.astype(BF16), wuq_ref[...])
    kv = _dot(c_kv.astype(BF16), wukv_ref[...])
    for h in range(MLA_HEADS):
        lo = h * QK_PAD
        mid = lo + LANES
        hi = lo + QK_PAD
        q_ref[:, lo:mid] = q[:, lo:mid].astype(BF16)
        q_ref[:, mid:hi] = _rope(q[:, mid:hi], cos, sin_a, sin_b).astype(BF16)
        k_ref[:, lo:mid] = kv[:, lo:mid].astype(BF16)
        k_ref[:, mid:hi] = k_rope
        vt_ref[h * V_HEAD_DIM:(h + 1) * V_HEAD_DIM, :] = jnp.transpose(kv[:, mid:hi]).astype(BF16)


def _mla_proj(x, cos, sin_a, sin_b, win_p, qn, kvn, wuq_p, wukv, seq, batch, ts):
    d = x.shape[1] // batch
    q_lora, kv_lora = qn.shape[1], kvn.shape[1]
    hq = MLA_HEADS * QK_PAD
    hv = MLA_HEADS * V_HEAD_DIM
    col = lambda c: pl.BlockSpec((ts, c), lambda i, b: (i, b))
    return pl.pallas_call(
        functools.partial(_mla_proj_kernel, q_lora=q_lora, kv_lora=kv_lora),
        grid=(seq // ts, batch),
        in_specs=[col(d), col(LANES), col(LANES), col(LANES), _full(win_p.shape), _full(qn.shape),
                  _full(kvn.shape), _full(wuq_p.shape), _full(wukv.shape)],
        out_specs=[col(hq), col(hq), pl.BlockSpec((hv, ts), lambda i, b: (b, i))],
        out_shape=[jax.ShapeDtypeStruct((seq, batch * hq), BF16), jax.ShapeDtypeStruct((seq, batch * hq), BF16),
                   jax.ShapeDtypeStruct((batch * hv, seq), BF16)],
        compiler_params=_params("parallel", "parallel"),
        name="mla_proj",
    )(x, cos, sin_a, sin_b, win_p, qn, kvn, wuq_p, wukv)


ONES_ROWS = 16


def _attn_kernel(q_ref, k_ref, vt_ref, o_ref, m_sc, acc_sc, *, blk, heads):
    qi = pl.program_id(1)
    m_sc[...] = jnp.full(m_sc.shape, -jnp.inf, F32)
    acc_sc[...] = jnp.zeros(acc_sc.shape, F32)
    ones = jnp.ones((ONES_ROWS, blk), BF16)

    def step(ki, diagonal):
        start = pl.multiple_of(ki * blk, blk)
        results = []
        for h in range(heads):
            q = q_ref[:, h * QK_PAD:(h + 1) * QK_PAD]
            k = k_ref[pl.ds(start, blk), h * QK_PAD:(h + 1) * QK_PAD]
            vt = vt_ref[h * V_HEAD_DIM:(h + 1) * V_HEAD_DIM, pl.ds(start, blk)]
            vt = jnp.concatenate([vt, ones], axis=0)
            st = lax.dot_general(k, q, (((1,), (1,)), ((), ())), preferred_element_type=F32)
            if diagonal:
                keys = lax.broadcasted_iota(jnp.int32, st.shape, 0)
                queries = lax.broadcasted_iota(jnp.int32, st.shape, 1)
                st = jnp.where(keys <= queries, st, -jnp.inf)
            m_prev = m_sc[h]
            m_new = jnp.maximum(m_prev, jnp.max(st, axis=0, keepdims=True))
            alpha = jnp.exp(m_prev - m_new)
            p = jnp.exp(st - m_new).astype(BF16)
            results.append((m_new, alpha * acc_sc[h] + _dot(vt, p)))
        for h, (m_new, acc) in enumerate(results):
            m_sc[h] = m_new
            acc_sc[h] = acc

    def body(ki, carry):
        step(ki, False)
        return carry

    lax.fori_loop(0, qi, body, 0)
    step(qi, True)
    for h in range(heads):
        acc = acc_sc[h]
        out_t = acc[:V_HEAD_DIM, :] / acc[V_HEAD_DIM:V_HEAD_DIM + 1, :]
        o_ref[:, h * V_HEAD_DIM:(h + 1) * V_HEAD_DIM] = jnp.transpose(out_t).astype(o_ref.dtype)


def _attention(q, k, vt, seq, batch, blk, heads):
    groups = batch * MLA_HEADS // heads
    return pl.pallas_call(
        functools.partial(_attn_kernel, blk=blk, heads=heads),
        grid=(groups, seq // blk),
        in_specs=[pl.BlockSpec((blk, heads * QK_PAD), lambda g, i: (i, g)),
                  pl.BlockSpec((seq, heads * QK_PAD), lambda g, i: (0, g)),
                  pl.BlockSpec((heads * V_HEAD_DIM, seq), lambda g, i: (g, 0))],
        out_specs=pl.BlockSpec((blk, heads * V_HEAD_DIM), lambda g, i: (i, g)),
        out_shape=jax.ShapeDtypeStruct((seq, batch * MLA_HEADS * V_HEAD_DIM), BF16),
        scratch_shapes=[pltpu.VMEM((heads, 1, blk), F32),
                        pltpu.VMEM((heads, V_HEAD_DIM + ONES_ROWS, blk), F32)],
        compiler_params=_params("parallel", "arbitrary"),
        name="mla_attention",
    )(q, k, vt)


def _mix_epilogue(x, mix, g_ref, b_ref, wrh_ref, wrl_ref, br_ref, x1_ref, lg_ref, alpha):
    x1 = _layer_norm(alpha * x + mix, g_ref[...], b_ref[...])
    x1_ref[...] = x1
    lg_ref[...] = _router_logits(x1, wrh_ref[...], wrl_ref[...], br_ref[...])


def _post_attn_kernel(x_ref, o_ref, wo_ref, g_ref, b_ref, wrh_ref, wrl_ref, br_ref, x1_ref, lg_ref, *, alpha):
    mix = _dot(o_ref[...], wo_ref[...])
    _mix_epilogue(x_ref[...], mix, g_ref, b_ref, wrh_ref, wrl_ref, br_ref, x1_ref, lg_ref, alpha)


def _post_attn(x, o, wo, g, b, wr_hi, wr_lo, br, alpha, tm):
    t, d = x.shape
    row = lambda c: pl.BlockSpec((tm, c), lambda i: (i, 0))
    return pl.pallas_call(
        functools.partial(_post_attn_kernel, alpha=alpha),
        grid=(t // tm,),
        in_specs=[row(d), row(o.shape[1]), _full(wo.shape), _full(g.shape), _full(b.shape),
                  _full(wr_hi.shape), _full(wr_lo.shape), _full(br.shape)],
        out_specs=[row(d), row(LANES)],
        out_shape=[jax.ShapeDtypeStruct((t, d), F32), jax.ShapeDtypeStruct((t, LANES), F32)],
        compiler_params=_params("parallel"),
        name="post_attention",
    )(x, o, wo, g, b, wr_hi, wr_lo, br)


def _lru_kernel(x_ref, win_ref, cw_ref, cb_ref, wa_ref, ba_ref, wx_ref, bx_ref, lam_ref, wout_ref,
                g_ref, b_ref, wrh_ref, wrl_ref, br_ref, x1_ref, lg_ref,
                ucarry_sc, hcarry_sc, a_sc, b_sc, h_sc, *, alpha, batch):
    tm, width = a_sc.shape
    halo = (CONV_WIDTH - 1) * batch
    blk_w = width // LRU_BLOCKS

    @pl.when(pl.program_id(0) == 0)
    def _():
        ucarry_sc[...] = jnp.zeros(ucarry_sc.shape, F32)
        hcarry_sc[...] = jnp.zeros(hcarry_sc.shape, F32)

    x = x_ref[...]
    gu = _dot(x.astype(BF16), win_ref[...])
    gate = gu[:, :width]
    u = gu[:, width:]
    u_ext = jnp.concatenate([ucarry_sc[...], u], axis=0)
    ucarry_sc[...] = u[tm - halo:, :]
    cw = cw_ref[...]
    uc = cb_ref[...] + cw[0:1, :] * u_ext[0:tm, :]
    for j in range(1, CONV_WIDTH):
        uc = uc + cw[j:j + 1, :] * u_ext[j * batch:j * batch + tm, :]
    ucb = uc.astype(BF16)
    ra = jnp.concatenate([_dot(ucb[:, n * blk_w:(n + 1) * blk_w], wa_ref[n]) for n in range(LRU_BLOCKS)], axis=1)
    rx = jnp.concatenate([_dot(ucb[:, n * blk_w:(n + 1) * blk_w], wx_ref[n]) for n in range(LRU_BLOCKS)], axis=1)
    r = _sigmoid(ra + ba_ref[...])
    gi = _sigmoid(rx + bx_ref[...])
    z = -lam_ref[...]
    softplus = jnp.maximum(z, 0.0) + jnp.log1p(jnp.exp(-jnp.abs(z)))
    log_a = (-LRU_C) * r * softplus
    a_sc[...] = jnp.exp(log_a)
    th = jnp.tanh(log_a)
    b_sc[...] = jnp.sqrt(-2.0 * th / (1.0 - th)) * gi * uc

    def body(t, h):
        rows = pl.ds(pl.multiple_of(t * batch, batch), batch)
        h = a_sc[rows, :] * h + b_sc[rows, :]
        h_sc[rows, :] = h
        return h

    hcarry_sc[...] = lax.fori_loop(0, tm // batch, body, hcarry_sc[...], unroll=8)
    c = math.sqrt(2.0 / math.pi)
    gelu = 0.5 * gate * (1.0 + jnp.tanh(c * (gate + 0.044715 * (gate * gate * gate))))
    y = gelu * h_sc[...]
    mix = _dot(y.astype(BF16), wout_ref[...])
    _mix_epilogue(x, mix, g_ref, b_ref, wrh_ref, wrl_ref, br_ref, x1_ref, lg_ref, alpha)


def _lru_block(x, win, cw, cb, wa, ba, wx, bx, lam, wout, g, b, wr_hi, wr_lo, br, alpha, batch, tm):
    t, d = x.shape
    width = wout.shape[0]
    row = lambda c: pl.BlockSpec((tm, c), lambda i: (i, 0))
    consts = (win, cw, cb, wa, ba, wx, bx, lam, wout, g, b, wr_hi, wr_lo, br)
    return pl.pallas_call(
        functools.partial(_lru_kernel, alpha=alpha, batch=batch),
        grid=(t // tm,),
        in_specs=[row(d)] + [_full(c.shape) for c in consts],
        out_specs=[row(d), row(LANES)],
        out_shape=[jax.ShapeDtypeStruct((t, d), F32), jax.ShapeDtypeStruct((t, LANES), F32)],
        scratch_shapes=[pltpu.VMEM(((CONV_WIDTH - 1) * batch, width), F32), pltpu.VMEM((batch, width), F32),
                        pltpu.VMEM((tm, width), F32), pltpu.VMEM((tm, width), F32), pltpu.VMEM((tm, width), F32)],
        compiler_params=_params("arbitrary"),
        name="rglru_block",
    )(x, *consts)


def _expert_kernel(be_ref, nb_ref, xs_ref, wup_ref, bup_ref, wdn_ref, bdn_ref, ys_ref):
    d_ff = wdn_ref.shape[1]

    @pl.when(pl.program_id(0) < nb_ref[0])
    def _():
        hb = _dot(xs_ref[...], wup_ref[0]) + bup_ref[0]
        gl = jnp.minimum(hb[:, :d_ff], SWIGLU_LIMIT)
        up = jnp.clip(hb[:, d_ff:], -SWIGLU_LIMIT, SWIGLU_LIMIT)
        yb = (up + 1.0) * (gl * _sigmoid(SWIGLU_ALPHA * gl))
        ys_ref[...] = _dot(yb.astype(BF16), wdn_ref[0]) + bdn_ref[0]


def _experts(block_e, n_blocks_used, xs, w_up, b_up, w_down, b_down, bm):
    n_pad, d = xs.shape
    e, _, f2 = w_up.shape
    blk = lambda j, be, nb: jnp.minimum(j, nb[0] - 1)
    exp = lambda j, be, nb: be[jnp.minimum(j, nb[0] - 1)]
    grid_spec = pltpu.PrefetchScalarGridSpec(
        num_scalar_prefetch=2,
        grid=(n_pad // bm,),
        in_specs=[pl.BlockSpec((bm, d), lambda j, be, nb: (blk(j, be, nb), 0)),
                  pl.BlockSpec((1, d, f2), lambda j, be, nb: (exp(j, be, nb), 0, 0)),
                  pl.BlockSpec((1, 1, f2), lambda j, be, nb: (exp(j, be, nb), 0, 0)),
                  pl.BlockSpec((1, f2 // 2, d), lambda j, be, nb: (exp(j, be, nb), 0, 0)),
                  pl.BlockSpec((1, 1, d), lambda j, be, nb: (exp(j, be, nb), 0, 0))],
        out_specs=pl.BlockSpec((bm, d), lambda j, be, nb: (blk(j, be, nb), 0)),
    )
    return pl.pallas_call(
        _expert_kernel,
        grid_spec=grid_spec,
        out_shape=jax.ShapeDtypeStruct((n_pad, d), F32),
        compiler_params=_params("arbitrary"),
        name="moe_experts",
    )(block_e, n_blocks_used, xs, w_up, b_up, w_down, b_down)


def _post_moe_kernel(x1_ref, ffn_ref, p_ref, g_ref, b_ref, wg_ref, wp_ref, out_ref, *, alpha):
    x2 = _layer_norm(alpha * x1_ref[...] + ffn_ref[...], g_ref[...], b_ref[...])
    gate = _sigmoid(_dot(x2.astype(BF16), wg_ref[...]))
    out_ref[...] = x2 + gate * _dot(p_ref[...].astype(BF16), wp_ref[...])


def _post_moe(x1, ffn, p, g, b, wg, wp, alpha, tm):
    t, d = x1.shape
    row = lambda c: pl.BlockSpec((tm, c), lambda i: (i, 0))
    return pl.pallas_call(
        functools.partial(_post_moe_kernel, alpha=alpha),
        grid=(t // tm,),
        in_specs=[row(d), row(d), row(p.shape[1]), _full(g.shape), _full(b.shape), _full(wg.shape),
                  _full(wp.shape)],
        out_specs=row(d),
        out_shape=jax.ShapeDtypeStruct((t, d), F32),
        compiler_params=_params("parallel"),
        name="post_moe_ple",
    )(x1, ffn, p, g, b, wg, wp)


def _moe(x1, logits, w_up, b_up, w_down, b_down, bm):
    t, d = x1.shape
    top_logits, top_idx = lax.top_k(logits[:, :N_EXPERTS], TOP_K)
    gates = jax.nn.softmax(top_logits, axis=-1)
    n = t * TOP_K
    flat_e = top_idx.reshape(n).astype(jnp.int32)
    order = jnp.argsort(flat_e)
    sorted_e = flat_e[order]
    counts = jnp.zeros((N_EXPERTS,), jnp.int32).at[flat_e].add(1)
    padded = (counts + bm - 1) // bm * bm
    starts = jnp.cumsum(counts) - counts
    pad_ends = jnp.cumsum(padded)
    pad_starts = pad_ends - padded
    rank = jnp.arange(n, dtype=jnp.int32) - starts[sorted_e]
    dest_sorted = pad_starts[sorted_e] + rank
    n_pad = n + N_EXPERTS * bm
    n_blocks = n_pad // bm
    row_tok = jnp.zeros((n_pad,), jnp.int32).at[dest_sorted].set((order // TOP_K).astype(jnp.int32))
    dest = jnp.zeros((n,), jnp.int32).at[order].set(dest_sorted)
    block_e = jnp.searchsorted(pad_ends, jnp.arange(n_blocks, dtype=jnp.int32) * bm, side='right')
    block_e = jnp.minimum(block_e, N_EXPERTS - 1).astype(jnp.int32)
    n_used = (pad_ends[-1] // bm).astype(jnp.int32).reshape(1)
    xs = x1.astype(BF16)[row_tok]
    ys = _experts(block_e, n_used, xs, w_up, b_up, w_down, b_down, bm)
    picked = ys[dest].reshape(t, TOP_K, d)
    return jnp.sum(picked * gates[:, :, None], axis=1)


def _pick_tile(n, target, quantum):
    tile = min(n, target)
    while n % tile or tile % quantum:
        tile -= quantum
    return tile


def kernel(x, p, positions, mla_w_in, mla_q_norm, mla_kv_norm, mla_w_uq, mla_w_ukv, mla_w_o, lru_w_in, lru_conv_w, lru_conv_b, lru_w_a, lru_b_a, lru_w_x, lru_b_x, lru_lambda, lru_w_out, ln1_g, ln1_b, ln2_g, ln2_b, moe_w_router, moe_b_router, moe_w_up, moe_b_up, moe_w_down, moe_b_down, ple_w_gate, ple_w_proj):
    batch, seq, d = x.shape
    depth = ln1_g.shape[0]
    t = batch * seq
    assert batch == SUBLANES, "the recurrence kernel maps the batch onto the sublanes of a vreg"
    alpha = (2.0 * depth) ** 0.25
    tm = _pick_tile(t, 512, SUBLANES * batch)
    attn_blk = _pick_tile(seq, 512, LANES)
    moe_bm = 512

    xt = jnp.transpose(x, (1, 0, 2)).reshape(t, d)
    pt = jnp.transpose(p, (0, 2, 1, 3)).reshape(depth, t, p.shape[-1])

    half = QK_ROPE_DIM // 2
    inv_freq = jnp.exp(-math.log(ROPE_THETA) * jnp.arange(half, dtype=F32) / half)
    ang = jnp.transpose(positions).reshape(t, 1).astype(F32) * inv_freq
    cos, sin = jnp.cos(ang), jnp.sin(ang)
    zeros = jnp.zeros_like(cos)
    rope_cos = jnp.concatenate([cos, cos, zeros, zeros], axis=1).reshape(seq, batch * LANES)
    rope_sin_a = jnp.concatenate([-sin, zeros, zeros, zeros], axis=1).reshape(seq, batch * LANES)
    rope_sin_b = jnp.concatenate([zeros, sin, zeros, zeros], axis=1).reshape(seq, batch * LANES)

    row2 = lambda a: a.reshape(1, -1)
    scale = 1.0 / math.sqrt(QK_NOPE_DIM + QK_ROPE_DIM)

    for layer in range(depth):
        j = layer // 2
        wr = jnp.pad(moe_w_router[layer], ((0, 0), (0, LANES - N_EXPERTS)))
        wr_hi, wr_lo = _split_bf16(wr)
        br = row2(jnp.pad(moe_b_router[layer], (0, LANES - N_EXPERTS)))
        g1, b1 = row2(ln1_g[layer]), row2(ln1_b[layer])
        if layer % 2 == 0:
            q_lora, kv_lora = mla_q_norm.shape[1], mla_kv_norm.shape[1]
            w_in = mla_w_in[j]
            win_p = jnp.pad(w_in, ((0, 0), (0, LANES - QK_ROPE_DIM))).astype(BF16)
            wuq = mla_w_uq[j].reshape(q_lora, MLA_HEADS, QK_NOPE_DIM + QK_ROPE_DIM)
            wuq_p = jnp.pad(wuq, ((0, 0), (0, 0), (0, QK_PAD - QK_NOPE_DIM - QK_ROPE_DIM)))
            wuq_p = wuq_p.reshape(q_lora, MLA_HEADS * QK_PAD).astype(BF16)
            q, k, vt = _mla_proj(xt.reshape(seq, batch * d), rope_cos, rope_sin_a, rope_sin_b, win_p,
                                 row2(mla_q_norm[j] * scale), row2(mla_kv_norm[j]),
                                 wuq_p, mla_w_ukv[j].astype(BF16), seq, batch, attn_blk)
            o = _attention(q, k, vt, seq, batch, attn_blk, 2)
            x1, logits = _post_attn(xt, o.reshape(t, -1), mla_w_o[j].astype(BF16), g1, b1,
                                    wr_hi, wr_lo, br, alpha, tm)
        else:
            x1, logits = _lru_block(
                xt, lru_w_in[j].astype(BF16), lru_conv_w[j], row2(lru_conv_b[j]),
                lru_w_a[j].astype(BF16), row2(lru_b_a[j]), lru_w_x[j].astype(BF16), row2(lru_b_x[j]),
                row2(lru_lambda[j]), lru_w_out[j].astype(BF16), g1, b1, wr_hi, wr_lo, br, alpha, batch, tm)
        ffn = _moe(x1, logits, moe_w_up[layer].astype(BF16), moe_b_up[layer][:, None, :],
                   moe_w_down[layer].astype(BF16), moe_b_down[layer][:, None, :], moe_bm)
        xt = _post_moe(x1, ffn, pt[layer], row2(ln2_g[layer]), row2(ln2_b[layer]),
                       ple_w_gate[layer].astype(BF16), ple_w_proj[layer].astype(BF16), alpha, tm)
    return jnp.transpose(xt.reshape(seq, batch, d), (1, 0, 2))
```

```python
import functools
import math

import jax
import jax.numpy as jnp
from jax import lax
from jax.experimental import pallas as pl
from jax.experimental.pallas import tpu as pltpu

F32 = jnp.float32
BF16 = jnp.bfloat16

MLA_HEADS = 8
QK_NOPE_DIM = 128
QK_ROPE_DIM = 64
V_HEAD_DIM = 128
ROPE_THETA = 10000.0
LRU_BLOCKS = 4
CONV_WIDTH = 4
LRU_C = 8.0
N_EXPERTS = 32
TOP_K = 4
SWIGLU_LIMIT = 7.0
SWIGLU_ALPHA = 1.702
LN_EPS = 1e-5
RMS_EPS = 1e-6

LANES = 128
SUBLANES = 8
QK_PAD = 2 * LANES
VMEM_LIMIT_BYTES = 56 * 1024 * 1024


def _params(*sem, flags=None):
    return pltpu.CompilerParams(dimension_semantics=sem, vmem_limit_bytes=VMEM_LIMIT_BYTES, flags=flags)


def _dot(a, b):
    return jnp.dot(a, b, preferred_element_type=F32)


def _sigmoid(z):
    return 1.0 / (1.0 + jnp.exp(-z))


def _layer_norm(y, g, b):
    mu = jnp.mean(y, axis=-1, keepdims=True)
    yc = y - mu
    var = jnp.mean(yc * yc, axis=-1, keepdims=True)
    return yc * lax.rsqrt(var + LN_EPS) * g + b


def _rms_norm(y, g):
    ms = jnp.mean(y * y, axis=-1, keepdims=True)
    return y * lax.rsqrt(ms + RMS_EPS) * g


def _split_bf16(a):
    hi = a.astype(BF16)
    lo = (a - hi.astype(F32)).astype(BF16)
    return hi, lo


def _router_logits(x1, wr_hi, wr_lo, br):
    hi, lo = _split_bf16(x1)
    return _dot(hi, wr_hi) + (_dot(hi, wr_lo) + _dot(lo, wr_hi)) + br


def _full(shape):
    return pl.BlockSpec(shape, lambda *_: (0,) * len(shape))


def _rope(blk, cos, sin_a, sin_b):
    return (blk * cos + pltpu.roll(blk, LANES - QK_ROPE_DIM // 2, 1) * sin_a
            + pltpu.roll(blk, QK_ROPE_DIM // 2, 1) * sin_b)


def _mla_proj_kernel(x_ref, cos_ref, sina_ref, sinb_ref, win_ref, qn_ref, kvn_ref, wuq_ref, wukv_ref,
                     q_ref, k_ref, vt_ref, *, q_lora, kv_lora):
    lat = _dot(x_ref[...].astype(BF16), win_ref[...])
    c_q = _rms_norm(lat[:, :q_lora], qn_ref[...])
    c_kv = _rms_norm(lat[:, q_lora:q_lora + kv_lora], kvn_ref[...])
    cos, sin_a, sin_b = cos_ref[...], sina_ref[...], sinb_ref[...]
    k_rope = _rope(lat[:, q_lora + kv_lora:], cos, sin_a, sin_b).astype(BF16)
    q = _dot(c_q.astype(BF16), wuq_ref[...])
    kv = _dot(c_kv.astype(BF16), wukv_ref[...])
    for h in range(MLA_HEADS):
        lo = h * QK_PAD
        mid = lo + LANES
        hi = lo + QK_PAD
        q_ref[:, lo:mid] = q[:, lo:mid].astype(BF16)
        q_ref[:, mid:hi] = _rope(q[:, mid:hi], cos, sin_a, sin_b).astype(BF16)
        k_ref[:, lo:mid] = kv[:, lo:mid].astype(BF16)
        k_ref[:, mid:hi] = k_rope
        vt_ref[h * V_HEAD_DIM:(h + 1) * V_HEAD_DIM, :] = jnp.transpose(kv[:, mid:hi]).astype(BF16)


def _mla_proj(x, cos, sin_a, sin_b, win_p, qn, kvn, wuq_p, wukv, seq, batch, ts):
    d = x.shape[1] // batch
    q_lora, kv_lora = qn.shape[1], kvn.shape[1]
    hq = MLA_HEADS * QK_PAD
    hv = MLA_HEADS * V_HEAD_DIM
    col = lambda c: pl.BlockSpec((ts, c), lambda i, b: (i, b))
    return pl.pallas_call(
        functools.partial(_mla_proj_kernel, q_lora=q_lora, kv_lora=kv_lora),
        grid=(seq // ts, batch),
        in_specs=[col(d), col(LANES), col(LANES), col(LANES), _full(win_p.shape), _full(qn.shape),
                  _full(kvn.shape), _full(wuq_p.shape), _full(wukv.shape)],
        out_specs=[col(hq), col(hq), pl.BlockSpec((hv, ts), lambda i, b: (b, i))],
        out_shape=[jax.ShapeDtypeStruct((seq, batch * hq), BF16), jax.ShapeDtypeStruct((seq, batch * hq), BF16),
                   jax.ShapeDtypeStruct((batch * hv, seq), BF16)],
        compiler_params=_params("parallel", "parallel"),
        name="mla_proj",
    )(x, cos, sin_a, sin_b, win_p, qn, kvn, wuq_p, wukv)


ONES_ROWS = 16


def _attn_kernel(q_ref, k_ref, vt_ref, o_ref, m_sc, acc_sc, *, blk, heads):
    qi = pl.program_id(1)
    m_sc[...] = jnp.full(m_sc.shape, -jnp.inf, F32)
    acc_sc[...] = jnp.zeros(acc_sc.shape, F32)
    ones = jnp.ones((ONES_ROWS, blk), BF16)

    def step(ki, diagonal):
        start = pl.multiple_of(ki * blk, blk)
        results = []
        for h in range(heads):
            q = q_ref[:, h * QK_PAD:(h + 1) * QK_PAD]
            k = k_ref[pl.ds(start, blk), h * QK_PAD:(h + 1) * QK_PAD]
            vt = vt_ref[h * V_HEAD_DIM:(h + 1) * V_HEAD_DIM, pl.ds(start, blk)]
            vt = jnp.concatenate([vt, ones], axis=0)
            st = lax.dot_general(k, q, (((1,), (1,)), ((), ())), preferred_element_type=F32)
            if diagonal:
                keys = lax.broadcasted_iota(jnp.int32, st.shape, 0)
                queries = lax.broadcasted_iota(jnp.int32, st.shape, 1)
                st = jnp.where(keys <= queries, st, -jnp.inf)
            m_prev = m_sc[h]
            m_new = jnp.maximum(m_prev, jnp.max(st, axis=0, keepdims=True))
            alpha = jnp.exp(m_prev - m_new)
            p = jnp.exp(st - m_new).astype(BF16)
            results.append((m_new, alpha * acc_sc[h] + _dot(vt, p)))
        for h, (m_new, acc) in enumerate(results):
            m_sc[h] = m_new
            acc_sc[h] = acc

    def body(ki, carry):
        step(ki, False)
        return carry

    lax.fori_loop(0, qi, body, 0)
    step(qi, True)
    for h in range(heads):
        acc = acc_sc[h]
        out_t = acc[:V_HEAD_DIM, :] / acc[V_HEAD_DIM:V_HEAD_DIM + 1, :]
        o_ref[:, h * V_HEAD_DIM:(h + 1) * V_HEAD_DIM] = jnp.transpose(out_t).astype(o_ref.dtype)


def _attention(q, k, vt, seq, batch, blk, heads):
    groups = batch * MLA_HEADS // heads
    return pl.pallas_call(
        functools.partial(_attn_kernel, blk=blk, heads=heads),
        grid=(groups, seq // blk),
        in_specs=[pl.BlockSpec((blk, heads * QK_PAD), lambda g, i: (i, g)),
                  pl.BlockSpec((seq, heads * QK_PAD), lambda g, i: (0, g)),
                  pl.BlockSpec((heads * V_HEAD_DIM, seq), lambda g, i: (g, 0))],
        out_specs=pl.BlockSpec((blk, heads * V_HEAD_DIM), lambda g, i: (i, g)),
        out_shape=jax.ShapeDtypeStruct((seq, batch * MLA_HEADS * V_HEAD_DIM), BF16),
        scratch_shapes=[pltpu.VMEM((heads, 1, blk), F32),
                        pltpu.VMEM((heads, V_HEAD_DIM + ONES_ROWS, blk), F32)],
        compiler_params=_params("parallel", "arbitrary"),
        name="mla_attention",
    )(q, k, vt)


def _mix_epilogue(x, mix, g_ref, b_ref, wrh_ref, wrl_ref, br_ref, x1_ref, lg_ref, alpha):
    x1 = _layer_norm(alpha * x + mix, g_ref[...], b_ref[...])
    x1_ref[...] = x1
    lg_ref[...] = _router_logits(x1, wrh_ref[...], wrl_ref[...], br_ref[...])


def _post_attn_kernel(x_ref, o_ref, wo_ref, g_ref, b_ref, wrh_ref, wrl_ref, br_ref, x1_ref, lg_ref, *, alpha):
    mix = _dot(o_ref[...], wo_ref[...])
    _mix_epilogue(x_ref[...], mix, g_ref, b_ref, wrh_ref, wrl_ref, br_ref, x1_ref, lg_ref, alpha)


def _post_attn(x, o, wo, g, b, wr_hi, wr_lo, br, alpha, tm):
    t, d = x.shape
    row = lambda c: pl.BlockSpec((tm, c), lambda i: (i, 0))
    return pl.pallas_call(
        functools.partial(_post_attn_kernel, alpha=alpha),
        grid=(t // tm,),
        in_specs=[row(d), row(o.shape[1]), _full(wo.shape), _full(g.shape), _full(b.shape),
                  _full(wr_hi.shape), _full(wr_lo.shape), _full(br.shape)],
        out_specs=[row(d), row(LANES)],
        out_shape=[jax.ShapeDtypeStruct((t, d), F32), jax.ShapeDtypeStruct((t, LANES), F32)],
        compiler_params=_params("parallel"),
        name="post_attention",
    )(x, o, wo, g, b, wr_hi, wr_lo, br)


def _lru_kernel(x_ref, win_ref, cw_ref, cb_ref, wa_ref, ba_ref, wx_ref, bx_ref, lam_ref, wout_ref,
                g_ref, b_ref, wrh_ref, wrl_ref, br_ref, x1_ref, lg_ref,
                ucarry_sc, hcarry_sc, a_sc, b_sc, h_sc, *, alpha, batch):
    tm, width = a_sc.shape
    halo = (CONV_WIDTH - 1) * batch
    blk_w = width // LRU_BLOCKS

    @pl.when(pl.program_id(0) == 0)
    def _():
        ucarry_sc[...] = jnp.zeros(ucarry_sc.shape, F32)
        hcarry_sc[...] = jnp.zeros(hcarry_sc.shape, F32)

    x = x_ref[...]
    gu = _dot(x.astype(BF16), win_ref[...])
    gate = gu[:, :width]
    u = gu[:, width:]
    u_ext = jnp.concatenate([ucarry_sc[...], u], axis=0)
    ucarry_sc[...] = u[tm - halo:, :]
    cw = cw_ref[...]
    uc = cb_ref[...] + cw[0:1, :] * u_ext[0:tm, :]
    for j in range(1, CONV_WIDTH):
        uc = uc + cw[j:j + 1, :] * u_ext[j * batch:j * batch + tm, :]
    ucb = uc.astype(BF16)
    ra = jnp.concatenate([_dot(ucb[:, n * blk_w:(n + 1) * blk_w], wa_ref[n]) for n in range(LRU_BLOCKS)], axis=1)
    rx = jnp.concatenate([_dot(ucb[:, n * blk_w:(n + 1) * blk_w], wx_ref[n]) for n in range(LRU_BLOCKS)], axis=1)
    r = _sigmoid(ra + ba_ref[...])
    gi = _sigmoid(rx + bx_ref[...])
    z = -lam_ref[...]
    softplus = jnp.maximum(z, 0.0) + jnp.log1p(jnp.exp(-jnp.abs(z)))
    log_a = (-LRU_C) * r * softplus
    a_sc[...] = jnp.exp(log_a)
    th = jnp.tanh(log_a)
    b_sc[...] = jnp.sqrt(-2.0 * th / (1.0 - th)) * gi * uc

    def body(t, h):
        rows = pl.ds(pl.multiple_of(t * batch, batch), batch)
        h = a_sc[rows, :] * h + b_sc[rows, :]
        h_sc[rows, :] = h
        return h

    hcarry_sc[...] = lax.fori_loop(0, tm // batch, body, hcarry_sc[...], unroll=8)
    c = math.sqrt(2.0 / math.pi)
    gelu = 0.5 * gate * (1.0 + jnp.tanh(c * (gate + 0.044715 * (gate * gate * gate))))
    y = gelu * h_sc[...]
    mix = _dot(y.astype(BF16), wout_ref[...])
    _mix_epilogue(x, mix, g_ref, b_ref, wrh_ref, wrl_ref, br_ref, x1_ref, lg_ref, alpha)


def _lru_block(x, win, cw, cb, wa, ba, wx, bx, lam, wout, g, b, wr_hi, wr_lo, br, alpha, batch, tm):
    t, d = x.shape
    width = wout.shape[0]
    row = lambda c: pl.BlockSpec((tm, c), lambda i: (i, 0))
    consts = (win, cw, cb, wa, ba, wx, bx, lam, wout, g, b, wr_hi, wr_lo, br)
    return pl.pallas_call(
        functools.partial(_lru_kernel, alpha=alpha, batch=batch),
        grid=(t // tm,),
        in_specs=[row(d)] + [_full(c.shape) for c in consts],
        out_specs=[row(d), row(LANES)],
        out_shape=[jax.ShapeDtypeStruct((t, d), F32), jax.ShapeDtypeStruct((t, LANES), F32)],
        scratch_shapes=[pltpu.VMEM(((CONV_WIDTH - 1) * batch, width), F32), pltpu.VMEM((batch, width), F32),
                        pltpu.VMEM((tm, width), F32), pltpu.VMEM((tm, width), F32), pltpu.VMEM((tm, width), F32)],
        compiler_params=_params("arbitrary"),
        name="rglru_block",
    )(x, *consts)


def _route_kernel(lg_ref, eidx_ref, rank_ref, gate_ref, cnt_ref, run_sc):
    tm = lg_ref.shape[0]

    @pl.when(pl.program_id(0) == 0)
    def _():
        run_sc[...] = jnp.zeros(run_sc.shape, F32)

    lane = lax.broadcasted_iota(jnp.int32, (tm, LANES), 1)
    lane_f = lane.astype(F32)
    work = jnp.where(lane < N_EXPERTS, lg_ref[...], -jnp.inf)
    tops, onehots = [], []
    eidx = jnp.zeros((tm, LANES), jnp.int32)
    for k in range(TOP_K):
        top = jnp.max(work, axis=1, keepdims=True)
        idx = jnp.min(jnp.where(work == top, lane_f, float(LANES)), axis=1, keepdims=True)
        hot = lane_f == idx
        work = jnp.where(hot, -jnp.inf, work)
        eidx = jnp.where(lane == k, idx.astype(jnp.int32), eidx)
        tops.append(top)
        onehots.append(hot)
    exps = [jnp.exp(top - tops[0]) for top in tops]
    denom = exps[0]
    for e in exps[1:]:
        denom = denom + e
    gate = jnp.zeros((tm, LANES), F32)
    for k in range(TOP_K):
        gate = jnp.where(lane == k, exps[k] / denom, gate)
    hits = onehots[0].astype(F32)
    for hot in onehots[1:]:
        hits = hits + hot.astype(F32)
    earlier = (lax.broadcasted_iota(jnp.int32, (tm, tm), 1) < lax.broadcasted_iota(jnp.int32, (tm, tm), 0))
    before = _dot(earlier.astype(BF16), hits.astype(BF16)) + run_sc[...]
    rank = jnp.zeros((tm, LANES), jnp.int32)
    for k in range(TOP_K):
        r_k = jnp.sum(jnp.where(onehots[k], before, 0.0), axis=1, keepdims=True)
        rank = jnp.where(lane == k, r_k.astype(jnp.int32), rank)
    run_sc[...] = run_sc[...] + jnp.sum(hits, axis=0, keepdims=True)
    eidx_ref[...] = eidx
    rank_ref[...] = rank
    gate_ref[...] = gate
    cnt_ref[...] = run_sc[...].astype(jnp.int32)


def _route(logits, tm):
    t = logits.shape[0]
    row = pl.BlockSpec((tm, LANES), lambda i: (i, 0))
    return pl.pallas_call(
        _route_kernel,
        grid=(t // tm,),
        in_specs=[row],
        out_specs=[row, row, row, pl.BlockSpec((1, LANES), lambda i: (0, 0))],
        out_shape=[jax.ShapeDtypeStruct((t, LANES), jnp.int32), jax.ShapeDtypeStruct((t, LANES), jnp.int32),
                   jax.ShapeDtypeStruct((t, LANES), F32), jax.ShapeDtypeStruct((1, LANES), jnp.int32)],
        scratch_shapes=[pltpu.VMEM((1, LANES), F32)],
        compiler_params=_params("arbitrary"),
        name="moe_route",
    )(logits)


def _row_copy(src_hbm, src_row, dst, dst_row, sem):
    return pltpu.make_async_copy(src_hbm.at[pl.ds(src_row, 1)], dst.at[pl.ds(dst_row, 1)], sem)


def _dispatch_kernel(dest_ref, x_hbm, xs_hbm, sem, *, tm):
    base = pl.program_id(0) * tm

    def body(r, carry):
        for k in range(TOP_K):
            _row_copy(x_hbm, base + r, xs_hbm, dest_ref[r * TOP_K + k], sem).start()
        return carry

    lax.fori_loop(0, tm, body, 0)
    rows = pl.ds(0, tm * TOP_K)
    pltpu.make_async_copy(x_hbm.at[rows], xs_hbm.at[rows], sem).wait()


def _dispatch(dest, x1, tm):
    t, d = x1.shape
    return pl.pallas_call(
        functools.partial(_dispatch_kernel, tm=tm),
        grid=(t // tm,),
        in_specs=[pl.BlockSpec((tm * TOP_K,), lambda i: (i,), memory_space=pltpu.SMEM),
                  pl.BlockSpec(memory_space=pl.ANY)],
        out_specs=pl.BlockSpec(memory_space=pl.ANY),
        out_shape=jax.ShapeDtypeStruct((t * TOP_K, d), x1.dtype),
        scratch_shapes=[pltpu.SemaphoreType.DMA(())],
        compiler_params=_params("arbitrary"),
        name="moe_dispatch",
    )(dest, x1)


def _expert_kernel(ie_ref, ib_ref, lo_ref, hi_ref, nw_ref, xs_ref, wup_ref, bup_ref, wdn_ref, bdn_ref, ys_ref,
                   wup_sc, wdn_sc):
    w = pl.program_id(0)
    bm = xs_ref.shape[0]
    d_ff = wdn_ref.shape[1]
    valid = w < nw_ref[0]
    new_expert = jnp.logical_or(w == 0, ie_ref[w] != ie_ref[jnp.maximum(w - 1, 0)])

    @pl.when(jnp.logical_and(valid, new_expert))
    def _():
        wup_sc[...] = wup_ref[0].astype(BF16)
        wdn_sc[...] = wdn_ref[0].astype(BF16)

    @pl.when(valid)
    def _():
        hb = _dot(xs_ref[...].astype(BF16), wup_sc[...]) + bup_ref[0]
        gl = jnp.minimum(hb[:, :d_ff], SWIGLU_LIMIT)
        up = jnp.clip(hb[:, d_ff:], -SWIGLU_LIMIT, SWIGLU_LIMIT)
        yb = (up + 1.0) * (gl * _sigmoid(SWIGLU_ALPHA * gl))
        y = _dot(yb.astype(BF16), wdn_sc[...]) + bdn_ref[0]
        rows = lax.broadcasted_iota(jnp.int32, (bm, 1), 0)
        mine = jnp.logical_and(rows >= lo_ref[w], rows < hi_ref[w])
        first_visit = lo_ref[w] == 0

        @pl.when(first_visit)
        def _():
            ys_ref[...] = jnp.where(mine, y, 0.0)

        @pl.when(jnp.logical_not(first_visit))
        def _():
            ys_ref[...] = jnp.where(mine, y, ys_ref[...])


def _experts(item_e, item_blk, item_lo, item_hi, n_items, xs, w_up, b_up, w_down, b_down, bm):
    n, d = xs.shape
    f2 = w_up.shape[2]
    n_max = item_e.shape[0]
    cur = lambda w, nw: jnp.minimum(w, nw[0] - 1)
    blk = lambda w, ie, ib, lo, hi, nw: (ib[cur(w, nw)], 0)
    exp3 = lambda w, ie, ib, lo, hi, nw: (ie[cur(w, nw)], 0, 0)
    grid_spec = pltpu.PrefetchScalarGridSpec(
        num_scalar_prefetch=5,
        grid=(n_max,),
        in_specs=[pl.BlockSpec((bm, d), blk),
                  pl.BlockSpec((1, d, f2), exp3), pl.BlockSpec((1, 1, f2), exp3),
                  pl.BlockSpec((1, f2 // 2, d), exp3), pl.BlockSpec((1, 1, d), exp3)],
        out_specs=pl.BlockSpec((bm, d), blk),
        scratch_shapes=[pltpu.VMEM((d, f2), BF16), pltpu.VMEM((f2 // 2, d), BF16)],
    )
    return pl.pallas_call(
        _expert_kernel,
        grid_spec=grid_spec,
        out_shape=jax.ShapeDtypeStruct((n, d), F32),
        compiler_params=_params("arbitrary"),
        name="moe_experts",
    )(item_e, item_blk, item_lo, item_hi, n_items, xs, w_up, b_up, w_down, b_down)


def _post_moe_kernel(dest_ref, x1_ref, gate_ref, p_ref, g_ref, b_ref, wg_ref, wp_ref, ys_hbm, out_ref,
                     buf, sem, *, alpha):
    tm = x1_ref.shape[0]

    def body(r, carry):
        for k in range(TOP_K):
            _row_copy(ys_hbm, dest_ref[r * TOP_K + k], buf.at[k], r, sem).start()
        return carry

    lax.fori_loop(0, tm, body, 0)
    for k in range(TOP_K):
        pltpu.make_async_copy(ys_hbm.at[pl.ds(0, tm)], buf.at[k], sem).wait()
    gates = gate_ref[...]
    ffn = gates[:, 0:1] * buf[0]
    for k in range(1, TOP_K):
        ffn = ffn + gates[:, k:k + 1] * buf[k]
    x2 = _layer_norm(alpha * x1_ref[...] + ffn, g_ref[...], b_ref[...])
    gate = _sigmoid(_dot(x2.astype(BF16), wg_ref[...]))
    out_ref[...] = x2 + gate * _dot(p_ref[...].astype(BF16), wp_ref[...])


def _post_moe(dest, x1, gates, ys, p, g, b, wg, wp, alpha, tm):
    t, d = x1.shape
    row = lambda c: pl.BlockSpec((tm, c), lambda i: (i, 0))
    return pl.pallas_call(
        functools.partial(_post_moe_kernel, alpha=alpha),
        grid=(t // tm,),
        in_specs=[pl.BlockSpec((tm * TOP_K,), lambda i: (i,), memory_space=pltpu.SMEM),
                  row(d), row(LANES), row(p.shape[1]), _full(g.shape), _full(b.shape), _full(wg.shape),
                  _full(wp.shape), pl.BlockSpec(memory_space=pl.ANY)],
        out_specs=row(d),
        out_shape=jax.ShapeDtypeStruct((t, d), F32),
        scratch_shapes=[pltpu.VMEM((TOP_K, tm, d), F32), pltpu.SemaphoreType.DMA(())],
        compiler_params=_params("arbitrary"),
        name="post_moe_ple",
    )(dest, x1, gates, p, g, b, wg, wp, ys)


def _moe_plan(counts, n, bm):
    n_blocks = n // bm
    n_max = n_blocks + N_EXPERTS - 1
    ends = jnp.cumsum(counts)
    starts = ends - counts
    first_blk = starts // bm
    n_items_e = jnp.where(counts > 0, (ends - 1) // bm - first_blk + 1, 0)
    item_end = jnp.cumsum(n_items_e)
    item_start = item_end - n_items_e
    n_items = item_end[-1]
    w = jnp.minimum(jnp.arange(n_max, dtype=jnp.int32), n_items - 1)
    item_e = jnp.sum((w[:, None] >= item_end[None, :]).astype(jnp.int32), axis=1)
    item_blk = (first_blk[item_e] + w - item_start[item_e]).astype(jnp.int32)
    item_lo = jnp.maximum(starts[item_e] - item_blk * bm, 0).astype(jnp.int32)
    item_hi = jnp.minimum(ends[item_e] - item_blk * bm, bm).astype(jnp.int32)
    return starts, item_e.astype(jnp.int32), item_blk, item_lo, item_hi, n_items.astype(jnp.int32).reshape(1)


def _moe_dispatch(x1, logits, w_up, b_up, w_down, b_down, tm, bm):
    t, d = x1.shape
    eidx, rank, gates, counts = _route(logits, tm)
    starts, item_e, item_blk, item_lo, item_hi, n_items = _moe_plan(counts[0, :N_EXPERTS], t * TOP_K, bm)
    dest = (starts[eidx[:, :TOP_K]] + rank[:, :TOP_K]).reshape(t * TOP_K).astype(jnp.int32)
    xs = _dispatch(dest, x1, tm)
    ys = _experts(item_e, item_blk, item_lo, item_hi, n_items, xs, w_up, b_up, w_down, b_down, bm)
    return dest, gates, ys


def _pick_tile(n, target, quantum):
    tile = min(n, target)
    while n % tile or tile % quantum:
        tile -= quantum
    return tile


def kernel(x, p, positions, mla_w_in, mla_q_norm, mla_kv_norm, mla_w_uq, mla_w_ukv, mla_w_o, lru_w_in, lru_conv_w, lru_conv_b, lru_w_a, lru_b_a, lru_w_x, lru_b_x, lru_lambda, lru_w_out, ln1_g, ln1_b, ln2_g, ln2_b, moe_w_router, moe_b_router, moe_w_up, moe_b_up, moe_w_down, moe_b_down, ple_w_gate, ple_w_proj):
    batch, seq, d = x.shape
    depth = ln1_g.shape[0]
    t = batch * seq
    assert batch == SUBLANES, "the recurrence kernel maps the batch onto the sublanes of a vreg"
    alpha = (2.0 * depth) ** 0.25
    tm = _pick_tile(t, 512, SUBLANES * batch)
    attn_blk = _pick_tile(seq, 512, LANES)
    moe_bm = _pick_tile(t * TOP_K, 512, SUBLANES)
    combine_tm = _pick_tile(t, 256, SUBLANES)

    xt = jnp.transpose(x, (1, 0, 2)).reshape(t, d)
    pt = jnp.transpose(p, (0, 2, 1, 3)).reshape(depth, t, p.shape[-1])

    half = QK_ROPE_DIM // 2
    inv_freq = jnp.exp(-math.log(ROPE_THETA) * jnp.arange(half, dtype=F32) / half)
    ang = jnp.transpose(positions).reshape(t, 1).astype(F32) * inv_freq
    cos, sin = jnp.cos(ang), jnp.sin(ang)
    zeros = jnp.zeros_like(cos)
    rope_cos = jnp.concatenate([cos, cos, zeros, zeros], axis=1).reshape(seq, batch * LANES)
    rope_sin_a = jnp.concatenate([-sin, zeros, zeros, zeros], axis=1).reshape(seq, batch * LANES)
    rope_sin_b = jnp.concatenate([zeros, sin, zeros, zeros], axis=1).reshape(seq, batch * LANES)

    row2 = lambda a: a.reshape(1, -1)
    scale = 1.0 / math.sqrt(QK_NOPE_DIM + QK_ROPE_DIM)

    for layer in range(depth):
        j = layer // 2
        wr = jnp.pad(moe_w_router[layer], ((0, 0), (0, LANES - N_EXPERTS)))
        wr_hi, wr_lo = _split_bf16(wr)
        br = row2(jnp.pad(moe_b_router[layer], (0, LANES - N_EXPERTS)))
        g1, b1 = row2(ln1_g[layer]), row2(ln1_b[layer])
        if layer % 2 == 0:
            q_lora, kv_lora = mla_q_norm.shape[1], mla_kv_norm.shape[1]
            w_in = mla_w_in[j]
            win_p = jnp.pad(w_in, ((0, 0), (0, LANES - QK_ROPE_DIM))).astype(BF16)
            wuq = mla_w_uq[j].reshape(q_lora, MLA_HEADS, QK_NOPE_DIM + QK_ROPE_DIM)
            wuq_p = jnp.pad(wuq, ((0, 0), (0, 0), (0, QK_PAD - QK_NOPE_DIM - QK_ROPE_DIM)))
            wuq_p = wuq_p.reshape(q_lora, MLA_HEADS * QK_PAD).astype(BF16)
            q, k, vt = _mla_proj(xt.reshape(seq, batch * d), rope_cos, rope_sin_a, rope_sin_b, win_p,
                                 row2(mla_q_norm[j] * scale), row2(mla_kv_norm[j]),
                                 wuq_p, mla_w_ukv[j].astype(BF16), seq, batch, attn_blk)
            o = _attention(q, k, vt, seq, batch, attn_blk, 2)
            x1, logits = _post_attn(xt, o.reshape(t, -1), mla_w_o[j].astype(BF16), g1, b1,
                                    wr_hi, wr_lo, br, alpha, tm)
        else:
            x1, logits = _lru_block(
                xt, lru_w_in[j].astype(BF16), lru_conv_w[j], row2(lru_conv_b[j]),
                lru_w_a[j].astype(BF16), row2(lru_b_a[j]), lru_w_x[j].astype(BF16), row2(lru_b_x[j]),
                row2(lru_lambda[j]), lru_w_out[j].astype(BF16), g1, b1, wr_hi, wr_lo, br, alpha, batch, tm)
        dest, gates, ys = _moe_dispatch(x1, logits, moe_w_up[layer], moe_b_up[layer][:, None, :],
                                        moe_w_down[layer], moe_b_down[layer][:, None, :], tm, moe_bm)
        xt = _post_moe(dest, x1, gates, ys, pt[layer], row2(ln2_g[layer]), row2(ln2_b[layer]),
                       ple_w_gate[layer].astype(BF16), ple_w_proj[layer].astype(BF16), alpha, combine_tm)
    return jnp.transpose(xt.reshape(seq, batch, d), (1, 0, 2))
```

```python
import functools
import math

import jax
import jax.numpy as jnp
from jax import lax
from jax.experimental import pallas as pl
from jax.experimental.pallas import tpu as pltpu

F32 = jnp.float32
BF16 = jnp.bfloat16

MLA_HEADS = 8
QK_NOPE_DIM = 128
QK_ROPE_DIM = 64
V_HEAD_DIM = 128
ROPE_THETA = 10000.0
LRU_BLOCKS = 4
CONV_WIDTH = 4
LRU_C = 8.0
N_EXPERTS = 32
TOP_K = 4
SWIGLU_LIMIT = 7.0
SWIGLU_ALPHA = 1.702
LN_EPS = 1e-5
RMS_EPS = 1e-6

LANES = 128
SUBLANES = 8
QK_PAD = 2 * LANES
VMEM_LIMIT_BYTES = 56 * 1024 * 1024


def _params(*sem, flags=None):
    return pltpu.CompilerParams(dimension_semantics=sem, vmem_limit_bytes=VMEM_LIMIT_BYTES, flags=flags)


def _dot(a, b):
    return jnp.dot(a, b, preferred_element_type=F32)


def _sigmoid(z):
    return 1.0 / (1.0 + jnp.exp(-z))


def _layer_norm(y, g, b):
    mu = jnp.mean(y, axis=-1, keepdims=True)
    yc = y - mu
    var = jnp.mean(yc * yc, axis=-1, keepdims=True)
    return yc * lax.rsqrt(var + LN_EPS) * g + b


def _rms_norm(y, g):
    ms = jnp.mean(y * y, axis=-1, keepdims=True)
    return y * lax.rsqrt(ms + RMS_EPS) * g


def _split_bf16(a):
    hi = a.astype(BF16)
    lo = (a - hi.astype(F32)).astype(BF16)
    return hi, lo


def _router_logits(x1, wr_hi, wr_lo, br):
    hi, lo = _split_bf16(x1)
    return _dot(hi, wr_hi) + (_dot(hi, wr_lo) + _dot(lo, wr_hi)) + br


def _full(shape):
    return pl.BlockSpec(shape, lambda *_: (0,) * len(shape))


def _rope(blk, cos, sin_a, sin_b):
    return (blk * cos + pltpu.roll(blk, LANES - QK_ROPE_DIM // 2, 1) * sin_a
            + pltpu.roll(blk, QK_ROPE_DIM // 2, 1) * sin_b)


def _mla_proj_kernel(x_ref, cos_ref, sina_ref, sinb_ref, win_ref, qn_ref, kvn_ref, wuq_ref, wukv_ref,
                     q_ref, k_ref, vt_ref, *, q_lora, kv_lora):
    lat = _dot(x_ref[...].astype(BF16), win_ref[...])
    c_q = _rms_norm(lat[:, :q_lora], qn_ref[...])
    c_kv = _rms_norm(lat[:, q_lora:q_lora + kv_lora], kvn_ref[...])
    cos, sin_a, sin_b = cos_ref[...], sina_ref[...], sinb_ref[...]
    k_rope = _rope(lat[:, q_lora + kv_lora:], cos, sin_a, sin_b).astype(BF16)
    q = _dot(c_q.astype(BF16), wuq_ref[...])
    kv = _dot(c_kv.astype(BF16), wukv_ref[...])
    for h in range(MLA_HEADS):
        lo = h * QK_PAD
        mid = lo + LANES
        hi = lo + QK_PAD
        q_ref[:, lo:mid] = q[:, lo:mid].astype(BF16)
        q_ref[:, mid:hi] = _rope(q[:, mid:hi], cos, sin_a, sin_b).astype(BF16)
        k_ref[:, lo:mid] = kv[:, lo:mid].astype(BF16)
        k_ref[:, mid:hi] = k_rope
        vt_ref[h * V_HEAD_DIM:(h + 1) * V_HEAD_DIM, :] = jnp.transpose(kv[:, mid:hi]).astype(BF16)


def _mla_proj(x, cos, sin_a, sin_b, win_p, qn, kvn, wuq_p, wukv, seq, batch, ts):
    d = x.shape[1] // batch
    q_lora, kv_lora = qn.shape[1], kvn.shape[1]
    hq = MLA_HEADS * QK_PAD
    hv = MLA_HEADS * V_HEAD_DIM
    col = lambda c: pl.BlockSpec((ts, c), lambda i, b: (i, b))
    return pl.pallas_call(
        functools.partial(_mla_proj_kernel, q_lora=q_lora, kv_lora=kv_lora),
        grid=(seq // ts, batch),
        in_specs=[col(d), col(LANES), col(LANES), col(LANES), _full(win_p.shape), _full(qn.shape),
                  _full(kvn.shape), _full(wuq_p.shape), _full(wukv.shape)],
        out_specs=[col(hq), col(hq), pl.BlockSpec((hv, ts), lambda i, b: (b, i))],
        out_shape=[jax.ShapeDtypeStruct((seq, batch * hq), BF16), jax.ShapeDtypeStruct((seq, batch * hq), BF16),
                   jax.ShapeDtypeStruct((batch * hv, seq), BF16)],
        compiler_params=_params("parallel", "parallel"),
        name="mla_proj",
    )(x, cos, sin_a, sin_b, win_p, qn, kvn, wuq_p, wukv)


ONES_ROWS = 16


def _attn_kernel(q_ref, k_ref, vt_ref, o_ref, m_sc, acc_sc, *, blk, heads):
    qi = pl.program_id(1)
    m_sc[...] = jnp.full(m_sc.shape, -jnp.inf, F32)
    acc_sc[...] = jnp.zeros(acc_sc.shape, F32)
    ones = jnp.ones((ONES_ROWS, blk), BF16)

    def step(ki, diagonal):
        start = pl.multiple_of(ki * blk, blk)
        results = []
        for h in range(heads):
            q = q_ref[:, h * QK_PAD:(h + 1) * QK_PAD]
            k = k_ref[pl.ds(start, blk), h * QK_PAD:(h + 1) * QK_PAD]
            vt = vt_ref[h * V_HEAD_DIM:(h + 1) * V_HEAD_DIM, pl.ds(start, blk)]
            vt = jnp.concatenate([vt, ones], axis=0)
            st = lax.dot_general(k, q, (((1,), (1,)), ((), ())), preferred_element_type=F32)
            if diagonal:
                keys = lax.broadcasted_iota(jnp.int32, st.shape, 0)
                queries = lax.broadcasted_iota(jnp.int32, st.shape, 1)
                st = jnp.where(keys <= queries, st, -jnp.inf)
            m_prev = m_sc[h]
            m_new = jnp.maximum(m_prev, jnp.max(st, axis=0, keepdims=True))
            alpha = jnp.exp(m_prev - m_new)
            p = jnp.exp(st - m_new).astype(BF16)
            results.append((m_new, alpha * acc_sc[h] + _dot(vt, p)))
        for h, (m_new, acc) in enumerate(results):
            m_sc[h] = m_new
            acc_sc[h] = acc

    def body(ki, carry):
        step(ki, False)
        return carry

    lax.fori_loop(0, qi, body, 0)
    step(qi, True)
    for h in range(heads):
        acc = acc_sc[h]
        out_t = acc[:V_HEAD_DIM, :] / acc[V_HEAD_DIM:V_HEAD_DIM + 1, :]
        o_ref[:, h * V_HEAD_DIM:(h + 1) * V_HEAD_DIM] = jnp.transpose(out_t).astype(o_ref.dtype)


def _attention(q, k, vt, seq, batch, blk, heads):
    groups = batch * MLA_HEADS // heads
    return pl.pallas_call(
        functools.partial(_attn_kernel, blk=blk, heads=heads),
        grid=(groups, seq // blk),
        in_specs=[pl.BlockSpec((blk, heads * QK_PAD), lambda g, i: (i, g)),
                  pl.BlockSpec((seq, heads * QK_PAD), lambda g, i: (0, g)),
                  pl.BlockSpec((heads * V_HEAD_DIM, seq), lambda g, i: (g, 0))],
        out_specs=pl.BlockSpec((blk, heads * V_HEAD_DIM), lambda g, i: (i, g)),
        out_shape=jax.ShapeDtypeStruct((seq, batch * MLA_HEADS * V_HEAD_DIM), BF16),
        scratch_shapes=[pltpu.VMEM((heads, 1, blk), F32),
                        pltpu.VMEM((heads, V_HEAD_DIM + ONES_ROWS, blk), F32)],
        compiler_params=_params("parallel", "arbitrary"),
        name="mla_attention",
    )(q, k, vt)


def _mix_epilogue(x, mix, g_ref, b_ref, wrh_ref, wrl_ref, br_ref, x1_ref, lg_ref, alpha):
    x1 = _layer_norm(alpha * x + mix, g_ref[...], b_ref[...])
    x1_ref[...] = x1
    lg_ref[...] = _router_logits(x1, wrh_ref[...], wrl_ref[...], br_ref[...])


def _post_attn_kernel(x_ref, o_ref, wo_ref, g_ref, b_ref, wrh_ref, wrl_ref, br_ref, x1_ref, lg_ref, *, alpha):
    mix = _dot(o_ref[...], wo_ref[...])
    _mix_epilogue(x_ref[...], mix, g_ref, b_ref, wrh_ref, wrl_ref, br_ref, x1_ref, lg_ref, alpha)


def _post_attn(x, o, wo, g, b, wr_hi, wr_lo, br, alpha, tm):
    t, d = x.shape
    row = lambda c: pl.BlockSpec((tm, c), lambda i: (i, 0))
    return pl.pallas_call(
        functools.partial(_post_attn_kernel, alpha=alpha),
        grid=(t // tm,),
        in_specs=[row(d), row(o.shape[1]), _full(wo.shape), _full(g.shape), _full(b.shape),
                  _full(wr_hi.shape), _full(wr_lo.shape), _full(br.shape)],
        out_specs=[row(d), row(LANES)],
        out_shape=[jax.ShapeDtypeStruct((t, d), F32), jax.ShapeDtypeStruct((t, LANES), F32)],
        compiler_params=_params("parallel"),
        name="post_attention",
    )(x, o, wo, g, b, wr_hi, wr_lo, br)


def _lru_kernel(x_ref, win_ref, cw_ref, cb_ref, wa_ref, ba_ref, wx_ref, bx_ref, lam_ref, wout_ref,
                g_ref, b_ref, wrh_ref, wrl_ref, br_ref, x1_ref, lg_ref,
                ucarry_sc, hcarry_sc, a_sc, b_sc, h_sc, *, alpha, batch):
    tm, width = a_sc.shape
    halo = (CONV_WIDTH - 1) * batch
    blk_w = width // LRU_BLOCKS

    @pl.when(pl.program_id(0) == 0)
    def _():
        ucarry_sc[...] = jnp.zeros(ucarry_sc.shape, F32)
        hcarry_sc[...] = jnp.zeros(hcarry_sc.shape, F32)

    x = x_ref[...]
    gu = _dot(x.astype(BF16), win_ref[...])
    gate = gu[:, :width]
    u = gu[:, width:]
    u_ext = jnp.concatenate([ucarry_sc[...], u], axis=0)
    ucarry_sc[...] = u[tm - halo:, :]
    cw = cw_ref[...]
    uc = cb_ref[...] + cw[0:1, :] * u_ext[0:tm, :]
    for j in range(1, CONV_WIDTH):
        uc = uc + cw[j:j + 1, :] * u_ext[j * batch:j * batch + tm, :]
    ucb = uc.astype(BF16)
    ra = jnp.concatenate([_dot(ucb[:, n * blk_w:(n + 1) * blk_w], wa_ref[n]) for n in range(LRU_BLOCKS)], axis=1)
    rx = jnp.concatenate([_dot(ucb[:, n * blk_w:(n + 1) * blk_w], wx_ref[n]) for n in range(LRU_BLOCKS)], axis=1)
    r = _sigmoid(ra + ba_ref[...])
    gi = _sigmoid(rx + bx_ref[...])
    z = -lam_ref[...]
    softplus = jnp.maximum(z, 0.0) + jnp.log1p(jnp.exp(-jnp.abs(z)))
    log_a = (-LRU_C) * r * softplus
    a_sc[...] = jnp.exp(log_a)
    th = jnp.tanh(log_a)
    b_sc[...] = jnp.sqrt(-2.0 * th / (1.0 - th)) * gi * uc

    def body(t, h):
        rows = pl.ds(pl.multiple_of(t * batch, batch), batch)
        h = a_sc[rows, :] * h + b_sc[rows, :]
        h_sc[rows, :] = h
        return h

    hcarry_sc[...] = lax.fori_loop(0, tm // batch, body, hcarry_sc[...], unroll=8)
    c = math.sqrt(2.0 / math.pi)
    gelu = 0.5 * gate * (1.0 + jnp.tanh(c * (gate + 0.044715 * (gate * gate * gate))))
    y = gelu * h_sc[...]
    mix = _dot(y.astype(BF16), wout_ref[...])
    _mix_epilogue(x, mix, g_ref, b_ref, wrh_ref, wrl_ref, br_ref, x1_ref, lg_ref, alpha)


def _lru_block(x, win, cw, cb, wa, ba, wx, bx, lam, wout, g, b, wr_hi, wr_lo, br, alpha, batch, tm):
    t, d = x.shape
    width = wout.shape[0]
    row = lambda c: pl.BlockSpec((tm, c), lambda i: (i, 0))
    consts = (win, cw, cb, wa, ba, wx, bx, lam, wout, g, b, wr_hi, wr_lo, br)
    return pl.pallas_call(
        functools.partial(_lru_kernel, alpha=alpha, batch=batch),
        grid=(t // tm,),
        in_specs=[row(d)] + [_full(c.shape) for c in consts],
        out_specs=[row(d), row(LANES)],
        out_shape=[jax.ShapeDtypeStruct((t, d), F32), jax.ShapeDtypeStruct((t, LANES), F32)],
        scratch_shapes=[pltpu.VMEM(((CONV_WIDTH - 1) * batch, width), F32), pltpu.VMEM((batch, width), F32),
                        pltpu.VMEM((tm, width), F32), pltpu.VMEM((tm, width), F32), pltpu.VMEM((tm, width), F32)],
        compiler_params=_params("arbitrary"),
        name="rglru_block",
    )(x, *consts)


def _route_kernel(lg_ref, eidx_ref, rank_ref, gate_ref, cnt_ref, run_sc):
    tm = lg_ref.shape[0]

    @pl.when(pl.program_id(0) == 0)
    def _():
        run_sc[...] = jnp.zeros(run_sc.shape, F32)

    lane = lax.broadcasted_iota(jnp.int32, (tm, LANES), 1)
    lane_f = lane.astype(F32)
    work = jnp.where(lane < N_EXPERTS, lg_ref[...], -jnp.inf)
    tops, onehots = [], []
    eidx = jnp.zeros((tm, LANES), jnp.int32)
    for k in range(TOP_K):
        top = jnp.max(work, axis=1, keepdims=True)
        idx = jnp.min(jnp.where(work == top, lane_f, float(LANES)), axis=1, keepdims=True)
        hot = lane_f == idx
        work = jnp.where(hot, -jnp.inf, work)
        eidx = jnp.where(lane == k, idx.astype(jnp.int32), eidx)
        tops.append(top)
        onehots.append(hot)
    exps = [jnp.exp(top - tops[0]) for top in tops]
    denom = exps[0]
    for e in exps[1:]:
        denom = denom + e
    gate = jnp.zeros((tm, LANES), F32)
    for k in range(TOP_K):
        gate = jnp.where(lane == k, exps[k] / denom, gate)
    hits = onehots[0].astype(F32)
    for hot in onehots[1:]:
        hits = hits + hot.astype(F32)
    earlier = (lax.broadcasted_iota(jnp.int32, (tm, tm), 1) < lax.broadcasted_iota(jnp.int32, (tm, tm), 0))
    before = _dot(earlier.astype(BF16), hits.astype(BF16)) + run_sc[...]
    rank = jnp.zeros((tm, LANES), jnp.int32)
    for k in range(TOP_K):
        r_k = jnp.sum(jnp.where(onehots[k], before, 0.0), axis=1, keepdims=True)
        rank = jnp.where(lane == k, r_k.astype(jnp.int32), rank)
    run_sc[...] = run_sc[...] + jnp.sum(hits, axis=0, keepdims=True)
    eidx_ref[...] = eidx
    rank_ref[...] = rank
    gate_ref[...] = gate
    cnt_ref[...] = run_sc[...].astype(jnp.int32)


def _route(logits, tm):
    t = logits.shape[0]
    row = pl.BlockSpec((tm, LANES), lambda i: (i, 0))
    return pl.pallas_call(
        _route_kernel,
        grid=(t // tm,),
        in_specs=[row],
        out_specs=[row, row, row, pl.BlockSpec((1, LANES), lambda i: (0, 0))],
        out_shape=[jax.ShapeDtypeStruct((t, LANES), jnp.int32), jax.ShapeDtypeStruct((t, LANES), jnp.int32),
                   jax.ShapeDtypeStruct((t, LANES), F32), jax.ShapeDtypeStruct((1, LANES), jnp.int32)],
        scratch_shapes=[pltpu.VMEM((1, LANES), F32)],
        compiler_params=_params("arbitrary"),
        name="moe_route",
    )(logits)


def _row_copy(src_hbm, src_row, dst, dst_row, sem):
    return pltpu.make_async_copy(src_hbm.at[pl.ds(src_row, 1)], dst.at[pl.ds(dst_row, 1)], sem)


def _dispatch_kernel(dest_ref, x_ref, xs_hbm, sem, *, tm):
    def body(r, carry):
        for k in range(TOP_K):
            _row_copy(x_ref, r, xs_hbm, dest_ref[r * TOP_K + k], sem).start()
        return carry

    lax.fori_loop(0, tm, body, 0)
    for k in range(TOP_K):
        pltpu.make_async_copy(x_ref, xs_hbm.at[pl.ds(0, tm)], sem).wait()


def _dispatch(dest, x1, tm):
    t, d = x1.shape
    return pl.pallas_call(
        functools.partial(_dispatch_kernel, tm=tm),
        grid=(t // tm,),
        in_specs=[pl.BlockSpec((tm * TOP_K,), lambda i: (i,), memory_space=pltpu.SMEM),
                  pl.BlockSpec((tm, d), lambda i: (i, 0))],
        out_specs=pl.BlockSpec(memory_space=pl.ANY),
        out_shape=jax.ShapeDtypeStruct((t * TOP_K, d), x1.dtype),
        scratch_shapes=[pltpu.SemaphoreType.DMA(())],
        compiler_params=_params("arbitrary"),
        name="moe_dispatch",
    )(dest, x1)


def _expert_kernel(ie_ref, ib_ref, lo_ref, hi_ref, nw_ref, xs_ref, wup_ref, bup_ref, wdn_ref, bdn_ref, ys_ref,
                   wup_sc, wdn_sc):
    w = pl.program_id(0)
    bm = xs_ref.shape[0]
    d_ff = wdn_ref.shape[1]
    valid = w < nw_ref[0]
    new_expert = jnp.logical_or(w == 0, ie_ref[w] != ie_ref[jnp.maximum(w - 1, 0)])

    @pl.when(jnp.logical_and(valid, new_expert))
    def _():
        wup_sc[...] = wup_ref[0].astype(BF16)
        wdn_sc[...] = wdn_ref[0].astype(BF16)

    @pl.when(valid)
    def _():
        hb = _dot(xs_ref[...].astype(BF16), wup_sc[...]) + bup_ref[0]
        gl = jnp.minimum(hb[:, :d_ff], SWIGLU_LIMIT)
        up = jnp.clip(hb[:, d_ff:], -SWIGLU_LIMIT, SWIGLU_LIMIT)
        yb = (up + 1.0) * (gl * _sigmoid(SWIGLU_ALPHA * gl))
        y = _dot(yb.astype(BF16), wdn_sc[...]) + bdn_ref[0]
        rows = lax.broadcasted_iota(jnp.int32, (bm, 1), 0)
        mine = jnp.logical_and(rows >= lo_ref[w], rows < hi_ref[w])
        first_visit = lo_ref[w] == 0

        @pl.when(first_visit)
        def _():
            ys_ref[...] = jnp.where(mine, y, 0.0)

        @pl.when(jnp.logical_not(first_visit))
        def _():
            ys_ref[...] = jnp.where(mine, y, ys_ref[...])


def _experts(item_e, item_blk, item_lo, item_hi, n_items, xs, w_up, b_up, w_down, b_down, layer, bm):
    n, d = xs.shape
    f2 = w_up.shape[3]
    n_max = item_e.shape[0]
    cur = lambda w, nw: jnp.minimum(w, nw[0] - 1)
    blk = lambda w, ie, ib, lo, hi, nw: (ib[cur(w, nw)], 0)
    exp3 = lambda w, ie, ib, lo, hi, nw: (layer, ie[cur(w, nw)], 0, 0)
    grid_spec = pltpu.PrefetchScalarGridSpec(
        num_scalar_prefetch=5,
        grid=(n_max,),
        in_specs=[pl.BlockSpec((bm, d), blk),
                  pl.BlockSpec((None, 1, d, f2), exp3), pl.BlockSpec((None, 1, 1, f2), exp3),
                  pl.BlockSpec((None, 1, f2 // 2, d), exp3), pl.BlockSpec((None, 1, 1, d), exp3)],
        out_specs=pl.BlockSpec((bm, d), blk),
        scratch_shapes=[pltpu.VMEM((d, f2), BF16), pltpu.VMEM((f2 // 2, d), BF16)],
    )
    return pl.pallas_call(
        _expert_kernel,
        grid_spec=grid_spec,
        out_shape=jax.ShapeDtypeStruct((n, d), F32),
        compiler_params=_params("arbitrary"),
        name="moe_experts",
    )(item_e, item_blk, item_lo, item_hi, n_items, xs, w_up, b_up, w_down, b_down)


def _post_moe_kernel(dest_ref, x1_ref, gate_ref, p_ref, g_ref, b_ref, wg_ref, wp_ref, ys_hbm, out_ref,
                     buf, sem, *, alpha):
    tm = x1_ref.shape[0]

    def body(r, carry):
        for k in range(TOP_K):
            _row_copy(ys_hbm, dest_ref[r * TOP_K + k], buf.at[k], r, sem).start()
        return carry

    lax.fori_loop(0, tm, body, 0)
    for k in range(TOP_K):
        pltpu.make_async_copy(ys_hbm.at[pl.ds(0, tm)], buf.at[k], sem).wait()
    gates = gate_ref[...]
    ffn = gates[:, 0:1] * buf[0]
    for k in range(1, TOP_K):
        ffn = ffn + gates[:, k:k + 1] * buf[k]
    x2 = _layer_norm(alpha * x1_ref[...] + ffn, g_ref[...], b_ref[...])
    gate = _sigmoid(_dot(x2.astype(BF16), wg_ref[...]))
    out_ref[...] = x2 + gate * _dot(p_ref[...].astype(BF16), wp_ref[...])


def _post_moe(dest, x1, gates, ys, p, g, b, wg, wp, alpha, tm):
    t, d = x1.shape
    row = lambda c: pl.BlockSpec((tm, c), lambda i: (i, 0))
    return pl.pallas_call(
        functools.partial(_post_moe_kernel, alpha=alpha),
        grid=(t // tm,),
        in_specs=[pl.BlockSpec((tm * TOP_K,), lambda i: (i,), memory_space=pltpu.SMEM),
                  row(d), row(LANES), row(p.shape[1]), _full(g.shape), _full(b.shape), _full(wg.shape),
                  _full(wp.shape), pl.BlockSpec(memory_space=pl.ANY)],
        out_specs=row(d),
        out_shape=jax.ShapeDtypeStruct((t, d), F32),
        scratch_shapes=[pltpu.VMEM((TOP_K, tm, d), F32), pltpu.SemaphoreType.DMA(())],
        compiler_params=_params("arbitrary"),
        name="post_moe_ple",
    )(dest, x1, gates, p, g, b, wg, wp, ys)


def _moe_plan(counts, n, bm):
    n_blocks = n // bm
    n_max = n_blocks + N_EXPERTS - 1
    ends = jnp.cumsum(counts)
    starts = ends - counts
    first_blk = starts // bm
    n_items_e = jnp.where(counts > 0, (ends - 1) // bm - first_blk + 1, 0)
    item_end = jnp.cumsum(n_items_e)
    item_start = item_end - n_items_e
    n_items = item_end[-1]
    w = jnp.minimum(jnp.arange(n_max, dtype=jnp.int32), n_items - 1)
    item_e = jnp.sum((w[:, None] >= item_end[None, :]).astype(jnp.int32), axis=1)
    item_blk = (first_blk[item_e] + w - item_start[item_e]).astype(jnp.int32)
    item_lo = jnp.maximum(starts[item_e] - item_blk * bm, 0).astype(jnp.int32)
    item_hi = jnp.minimum(ends[item_e] - item_blk * bm, bm).astype(jnp.int32)
    return starts, item_e.astype(jnp.int32), item_blk, item_lo, item_hi, n_items.astype(jnp.int32).reshape(1)


def _moe_dispatch(x1, logits, w_up, b_up, w_down, b_down, layer, tm, bm):
    t, d = x1.shape
    eidx, rank, gates, counts = _route(logits, tm)
    starts, item_e, item_blk, item_lo, item_hi, n_items = _moe_plan(counts[0, :N_EXPERTS], t * TOP_K, bm)
    dest = (starts[eidx[:, :TOP_K]] + rank[:, :TOP_K]).reshape(t * TOP_K).astype(jnp.int32)
    xs = _dispatch(dest, x1, tm)
    ys = _experts(item_e, item_blk, item_lo, item_hi, n_items, xs, w_up, b_up, w_down, b_down, layer, bm)
    return dest, gates, ys


def _pick_tile(n, target, quantum):
    tile = min(n, target)
    while n % tile or tile % quantum:
        tile -= quantum
    return tile


def kernel(x, p, positions, mla_w_in, mla_q_norm, mla_kv_norm, mla_w_uq, mla_w_ukv, mla_w_o, lru_w_in, lru_conv_w, lru_conv_b, lru_w_a, lru_b_a, lru_w_x, lru_b_x, lru_lambda, lru_w_out, ln1_g, ln1_b, ln2_g, ln2_b, moe_w_router, moe_b_router, moe_w_up, moe_b_up, moe_w_down, moe_b_down, ple_w_gate, ple_w_proj):
    batch, seq, d = x.shape
    depth = ln1_g.shape[0]
    t = batch * seq
    assert batch == SUBLANES, "the recurrence kernel maps the batch onto the sublanes of a vreg"
    alpha = (2.0 * depth) ** 0.25
    tm = _pick_tile(t, 512, SUBLANES * batch)
    attn_blk = _pick_tile(seq, 512, LANES)
    moe_bm = _pick_tile(t * TOP_K, 512, SUBLANES)
    combine_tm = _pick_tile(t, 256, SUBLANES)

    xt = jnp.transpose(x, (1, 0, 2)).reshape(t, d)
    pt = jnp.transpose(p, (0, 2, 1, 3)).reshape(depth, t, p.shape[-1])

    half = QK_ROPE_DIM // 2
    inv_freq = jnp.exp(-math.log(ROPE_THETA) * jnp.arange(half, dtype=F32) / half)
    ang = jnp.transpose(positions).reshape(t, 1).astype(F32) * inv_freq
    cos, sin = jnp.cos(ang), jnp.sin(ang)
    zeros = jnp.zeros_like(cos)
    rope_cos = jnp.concatenate([cos, cos, zeros, zeros], axis=1).reshape(seq, batch * LANES)
    rope_sin_a = jnp.concatenate([-sin, zeros, zeros, zeros], axis=1).reshape(seq, batch * LANES)
    rope_sin_b = jnp.concatenate([zeros, sin, zeros, zeros], axis=1).reshape(seq, batch * LANES)

    row2 = lambda a: a.reshape(1, -1)
    scale = 1.0 / math.sqrt(QK_NOPE_DIM + QK_ROPE_DIM)

    for layer in range(depth):
        j = layer // 2
        wr = jnp.pad(moe_w_router[layer], ((0, 0), (0, LANES - N_EXPERTS)))
        wr_hi, wr_lo = _split_bf16(wr)
        br = row2(jnp.pad(moe_b_router[layer], (0, LANES - N_EXPERTS)))
        g1, b1 = row2(ln1_g[layer]), row2(ln1_b[layer])
        if layer % 2 == 0:
            q_lora, kv_lora = mla_q_norm.shape[1], mla_kv_norm.shape[1]
            w_in = mla_w_in[j]
            win_p = jnp.pad(w_in, ((0, 0), (0, LANES - QK_ROPE_DIM))).astype(BF16)
            wuq = mla_w_uq[j].reshape(q_lora, MLA_HEADS, QK_NOPE_DIM + QK_ROPE_DIM)
            wuq_p = jnp.pad(wuq, ((0, 0), (0, 0), (0, QK_PAD - QK_NOPE_DIM - QK_ROPE_DIM)))
            wuq_p = wuq_p.reshape(q_lora, MLA_HEADS * QK_PAD).astype(BF16)
            q, k, vt = _mla_proj(xt.reshape(seq, batch * d), rope_cos, rope_sin_a, rope_sin_b, win_p,
                                 row2(mla_q_norm[j] * scale), row2(mla_kv_norm[j]),
                                 wuq_p, mla_w_ukv[j].astype(BF16), seq, batch, attn_blk)
            o = _attention(q, k, vt, seq, batch, attn_blk, 2)
            x1, logits = _post_attn(xt, o.reshape(t, -1), mla_w_o[j].astype(BF16), g1, b1,
                                    wr_hi, wr_lo, br, alpha, tm)
        else:
            x1, logits = _lru_block(
                xt, lru_w_in[j].astype(BF16), lru_conv_w[j], row2(lru_conv_b[j]),
                lru_w_a[j].astype(BF16), row2(lru_b_a[j]), lru_w_x[j].astype(BF16), row2(lru_b_x[j]),
                row2(lru_lambda[j]), lru_w_out[j].astype(BF16), g1, b1, wr_hi, wr_lo, br, alpha, batch, tm)
        dest, gates, ys = _moe_dispatch(x1, logits, moe_w_up, moe_b_up[:, :, None, :],
                                        moe_w_down, moe_b_down[:, :, None, :], layer, tm, moe_bm)
        xt = _post_moe(dest, x1, gates, ys, pt[layer], row2(ln2_g[layer]), row2(ln2_b[layer]),
                       ple_w_gate[layer].astype(BF16), ple_w_proj[layer].astype(BF16), alpha, combine_tm)
    return jnp.transpose(xt.reshape(seq, batch, d), (1, 0, 2))
```

```python
import functools
import math

import jax
import jax.numpy as jnp
from jax import lax
from jax.experimental import pallas as pl
from jax.experimental.pallas import tpu as pltpu

F32 = jnp.float32
BF16 = jnp.bfloat16

MLA_HEADS = 8
QK_NOPE_DIM = 128
QK_ROPE_DIM = 64
V_HEAD_DIM = 128
ROPE_THETA = 10000.0
LRU_BLOCKS = 4
CONV_WIDTH = 4
LRU_C = 8.0
N_EXPERTS = 32
TOP_K = 4
SWIGLU_LIMIT = 7.0
SWIGLU_ALPHA = 1.702
LN_EPS = 1e-5
RMS_EPS = 1e-6

LANES = 128
SUBLANES = 8
QK_PAD = 2 * LANES
VMEM_LIMIT_BYTES = 56 * 1024 * 1024


def _params(*sem, flags=None):
    return pltpu.CompilerParams(dimension_semantics=sem, vmem_limit_bytes=VMEM_LIMIT_BYTES, flags=flags)


def _dot(a, b):
    return jnp.dot(a, b, preferred_element_type=F32)


def _sigmoid(z):
    return 1.0 / (1.0 + jnp.exp(-z))


def _layer_norm(y, g, b):
    mu = jnp.mean(y, axis=-1, keepdims=True)
    yc = y - mu
    var = jnp.mean(yc * yc, axis=-1, keepdims=True)
    return yc * lax.rsqrt(var + LN_EPS) * g + b


def _rms_norm(y, g):
    ms = jnp.mean(y * y, axis=-1, keepdims=True)
    return y * lax.rsqrt(ms + RMS_EPS) * g


def _split_bf16(a):
    hi = a.astype(BF16)
    lo = (a - hi.astype(F32)).astype(BF16)
    return hi, lo


def _router_logits(x1, wr_hi, wr_lo, br):
    hi, lo = _split_bf16(x1)
    return _dot(hi, wr_hi) + (_dot(hi, wr_lo) + _dot(lo, wr_hi)) + br


def _full(shape):
    return pl.BlockSpec(shape, lambda *_: (0,) * len(shape))


def _rope(blk, cos, sin_a, sin_b):
    return (blk * cos + pltpu.roll(blk, LANES - QK_ROPE_DIM // 2, 1) * sin_a
            + pltpu.roll(blk, QK_ROPE_DIM // 2, 1) * sin_b)


def _mla_proj_kernel(x_ref, pos_ref, freq_ref, win_ref, qn_ref, kvn_ref, wuq_ref, wukv_ref,
                     q_ref, k_ref, vt_ref, *, q_lora, kv_lora):
    lat = _dot(x_ref[...].astype(BF16), win_ref[...])
    c_q = _rms_norm(lat[:, :q_lora], qn_ref[...])
    c_kv = _rms_norm(lat[:, q_lora:q_lora + kv_lora], kvn_ref[...])
    ang = pos_ref[...] * freq_ref[...]
    lane = lax.broadcasted_iota(jnp.int32, ang.shape, 1)
    cos, sin = jnp.cos(ang), jnp.sin(ang)
    half = QK_ROPE_DIM // 2
    sin_a = jnp.where(lane < half, -sin, 0.0)
    sin_b = jnp.where(jnp.logical_and(lane >= half, lane < QK_ROPE_DIM), sin, 0.0)
    k_rope = _rope(lat[:, q_lora + kv_lora:], cos, sin_a, sin_b).astype(BF16)
    q = _dot(c_q.astype(BF16), wuq_ref[...])
    kv = _dot(c_kv.astype(BF16), wukv_ref[...])
    for h in range(MLA_HEADS):
        lo = h * QK_PAD
        mid = lo + LANES
        hi = lo + QK_PAD
        q_ref[:, lo:mid] = q[:, lo:mid].astype(BF16)
        q_ref[:, mid:hi] = _rope(q[:, mid:hi], cos, sin_a, sin_b).astype(BF16)
        k_ref[:, lo:mid] = kv[:, lo:mid].astype(BF16)
        k_ref[:, mid:hi] = k_rope
        vt_ref[h * V_HEAD_DIM:(h + 1) * V_HEAD_DIM, :] = jnp.transpose(kv[:, mid:hi]).astype(BF16)


def _col_spec(ts, c):
    return pl.BlockSpec((ts, c), lambda i, b: (i, b))


def _x_spec(x, ts, d):
    if x.ndim == 3:
        return pl.BlockSpec((None, ts, d), lambda i, b: (b, i, 0))
    return _col_spec(ts, d)


def _mla_proj(x, pos, freq, win_p, qn, kvn, wuq_p, wukv, seq, batch, ts):
    d = win_p.shape[0]
    q_lora, kv_lora = qn.shape[1], kvn.shape[1]
    hq = MLA_HEADS * QK_PAD
    hv = MLA_HEADS * V_HEAD_DIM
    col = functools.partial(_col_spec, ts)
    return pl.pallas_call(
        functools.partial(_mla_proj_kernel, q_lora=q_lora, kv_lora=kv_lora),
        grid=(seq // ts, batch),
        in_specs=[_x_spec(x, ts, d), col(LANES), _full(freq.shape), _full(win_p.shape), _full(qn.shape),
                  _full(kvn.shape), _full(wuq_p.shape), _full(wukv.shape)],
        out_specs=[col(hq), col(hq), pl.BlockSpec((hv, ts), lambda i, b: (b, i))],
        out_shape=[jax.ShapeDtypeStruct((seq, batch * hq), BF16), jax.ShapeDtypeStruct((seq, batch * hq), BF16),
                   jax.ShapeDtypeStruct((batch * hv, seq), BF16)],
        compiler_params=_params("parallel", "parallel"),
        name="mla_proj",
    )(x, pos, freq, win_p, qn, kvn, wuq_p, wukv)


ONES_ROWS = 16


def _attn_kernel(q_ref, k_ref, vt_ref, o_ref, m_sc, acc_sc, *, blk, heads):
    qi = pl.program_id(1)
    m_sc[...] = jnp.full(m_sc.shape, -jnp.inf, F32)
    acc_sc[...] = jnp.zeros(acc_sc.shape, F32)
    ones = jnp.ones((ONES_ROWS, blk), BF16)

    def step(ki, diagonal):
        start = pl.multiple_of(ki * blk, blk)
        results = []
        for h in range(heads):
            q = q_ref[:, h * QK_PAD:(h + 1) * QK_PAD]
            k = k_ref[pl.ds(start, blk), h * QK_PAD:(h + 1) * QK_PAD]
            vt = vt_ref[h * V_HEAD_DIM:(h + 1) * V_HEAD_DIM, pl.ds(start, blk)]
            vt = jnp.concatenate([vt, ones], axis=0)
            st = lax.dot_general(k, q, (((1,), (1,)), ((), ())), preferred_element_type=F32)
            if diagonal:
                keys = lax.broadcasted_iota(jnp.int32, st.shape, 0)
                queries = lax.broadcasted_iota(jnp.int32, st.shape, 1)
                st = jnp.where(keys <= queries, st, -jnp.inf)
            m_prev = m_sc[h]
            m_new = jnp.maximum(m_prev, jnp.max(st, axis=0, keepdims=True))
            alpha = jnp.exp(m_prev - m_new)
            p = jnp.exp(st - m_new).astype(BF16)
            results.append((m_new, alpha * acc_sc[h] + _dot(vt, p)))
        for h, (m_new, acc) in enumerate(results):
            m_sc[h] = m_new
            acc_sc[h] = acc

    def body(ki, carry):
        step(ki, False)
        return carry

    lax.fori_loop(0, qi, body, 0)
    step(qi, True)
    for h in range(heads):
        acc = acc_sc[h]
        out_t = acc[:V_HEAD_DIM, :] / acc[V_HEAD_DIM:V_HEAD_DIM + 1, :]
        o_ref[:, h * V_HEAD_DIM:(h + 1) * V_HEAD_DIM] = jnp.transpose(out_t).astype(o_ref.dtype)


def _attention(q, k, vt, seq, batch, blk, heads):
    groups = batch * MLA_HEADS // heads
    return pl.pallas_call(
        functools.partial(_attn_kernel, blk=blk, heads=heads),
        grid=(groups, seq // blk),
        in_specs=[pl.BlockSpec((blk, heads * QK_PAD), lambda g, i: (i, g)),
                  pl.BlockSpec((seq, heads * QK_PAD), lambda g, i: (0, g)),
                  pl.BlockSpec((heads * V_HEAD_DIM, seq), lambda g, i: (g, 0))],
        out_specs=pl.BlockSpec((blk, heads * V_HEAD_DIM), lambda g, i: (i, g)),
        out_shape=jax.ShapeDtypeStruct((seq, batch * MLA_HEADS * V_HEAD_DIM), BF16),
        scratch_shapes=[pltpu.VMEM((heads, 1, blk), F32),
                        pltpu.VMEM((heads, V_HEAD_DIM + ONES_ROWS, blk), F32)],
        compiler_params=_params("parallel", "arbitrary"),
        name="mla_attention",
    )(q, k, vt)


def _mix_epilogue(x, mix, g_ref, b_ref, wrh_ref, wrl_ref, br_ref, x1_ref, lg_ref, alpha):
    x1 = _layer_norm(alpha * x + mix, g_ref[...], b_ref[...])
    x1_ref[...] = x1
    lg_ref[...] = _router_logits(x1, wrh_ref[...], wrl_ref[...], br_ref[...])


def _post_attn_kernel(x_ref, o_ref, wo_ref, g_ref, b_ref, wrh_ref, wrl_ref, br_ref, x1_ref, lg_ref, *, alpha):
    mix = _dot(o_ref[...], wo_ref[...])
    _mix_epilogue(x_ref[...], mix, g_ref, b_ref, wrh_ref, wrl_ref, br_ref, x1_ref, lg_ref, alpha)


def _post_attn(x, o, wo, g, b, wr_hi, wr_lo, br, alpha, seq, batch, ts):
    hv, d = wo.shape
    col = functools.partial(_col_spec, ts)
    return pl.pallas_call(
        functools.partial(_post_attn_kernel, alpha=alpha),
        grid=(seq // ts, batch),
        in_specs=[_x_spec(x, ts, d), col(hv), _full(wo.shape), _full(g.shape), _full(b.shape),
                  _full(wr_hi.shape), _full(wr_lo.shape), _full(br.shape)],
        out_specs=[col(d), col(LANES)],
        out_shape=[jax.ShapeDtypeStruct((seq, batch * d), F32), jax.ShapeDtypeStruct((seq, batch * LANES), F32)],
        compiler_params=_params("parallel", "parallel"),
        name="post_attention",
    )(x, o, wo, g, b, wr_hi, wr_lo, br)


def _lru_kernel(x_ref, win_ref, cw_ref, cb_ref, wa_ref, ba_ref, wx_ref, bx_ref, lam_ref, wout_ref,
                g_ref, b_ref, wrh_ref, wrl_ref, br_ref, x1_ref, lg_ref,
                ucarry_sc, hcarry_sc, a_sc, b_sc, h_sc, *, alpha, batch):
    tm, width = a_sc.shape
    halo = (CONV_WIDTH - 1) * batch
    blk_w = width // LRU_BLOCKS

    @pl.when(pl.program_id(0) == 0)
    def _():
        ucarry_sc[...] = jnp.zeros(ucarry_sc.shape, F32)
        hcarry_sc[...] = jnp.zeros(hcarry_sc.shape, F32)

    x = x_ref[...]
    gu = _dot(x.astype(BF16), win_ref[...])
    gate = gu[:, :width]
    u = gu[:, width:]
    u_ext = jnp.concatenate([ucarry_sc[...], u], axis=0)
    ucarry_sc[...] = u[tm - halo:, :]
    cw = cw_ref[...]
    uc = cb_ref[...] + cw[0:1, :] * u_ext[0:tm, :]
    for j in range(1, CONV_WIDTH):
        uc = uc + cw[j:j + 1, :] * u_ext[j * batch:j * batch + tm, :]
    ucb = uc.astype(BF16)
    ra = jnp.concatenate([_dot(ucb[:, n * blk_w:(n + 1) * blk_w], wa_ref[n]) for n in range(LRU_BLOCKS)], axis=1)
    rx = jnp.concatenate([_dot(ucb[:, n * blk_w:(n + 1) * blk_w], wx_ref[n]) for n in range(LRU_BLOCKS)], axis=1)
    r = _sigmoid(ra + ba_ref[...])
    gi = _sigmoid(rx + bx_ref[...])
    z = -lam_ref[...]
    softplus = jnp.maximum(z, 0.0) + jnp.log1p(jnp.exp(-jnp.abs(z)))
    log_a = (-LRU_C) * r * softplus
    a_sc[...] = jnp.exp(log_a)
    th = jnp.tanh(log_a)
    b_sc[...] = jnp.sqrt(-2.0 * th / (1.0 - th)) * gi * uc

    def body(t, h):
        rows = pl.ds(pl.multiple_of(t * batch, batch), batch)
        h = a_sc[rows, :] * h + b_sc[rows, :]
        h_sc[rows, :] = h
        return h

    hcarry_sc[...] = lax.fori_loop(0, tm // batch, body, hcarry_sc[...], unroll=8)
    c = math.sqrt(2.0 / math.pi)
    gelu = 0.5 * gate * (1.0 + jnp.tanh(c * (gate + 0.044715 * (gate * gate * gate))))
    y = gelu * h_sc[...]
    mix = _dot(y.astype(BF16), wout_ref[...])
    _mix_epilogue(x, mix, g_ref, b_ref, wrh_ref, wrl_ref, br_ref, x1_ref, lg_ref, alpha)


def _lru_block(x, win, cw, cb, wa, ba, wx, bx, lam, wout, g, b, wr_hi, wr_lo, br, alpha, batch, tm):
    t, d = x.shape
    width = wout.shape[0]
    row = lambda c: pl.BlockSpec((tm, c), lambda i: (i, 0))
    consts = (win, cw, cb, wa, ba, wx, bx, lam, wout, g, b, wr_hi, wr_lo, br)
    return pl.pallas_call(
        functools.partial(_lru_kernel, alpha=alpha, batch=batch),
        grid=(t // tm,),
        in_specs=[row(d)] + [_full(c.shape) for c in consts],
        out_specs=[row(d), row(LANES)],
        out_shape=[jax.ShapeDtypeStruct((t, d), F32), jax.ShapeDtypeStruct((t, LANES), F32)],
        scratch_shapes=[pltpu.VMEM(((CONV_WIDTH - 1) * batch, width), F32), pltpu.VMEM((batch, width), F32),
                        pltpu.VMEM((tm, width), F32), pltpu.VMEM((tm, width), F32), pltpu.VMEM((tm, width), F32)],
        compiler_params=_params("arbitrary"),
        name="rglru_block",
    )(x, *consts)


def _route_kernel(lg_ref, eidx_ref, rank_ref, gate_ref, cnt_ref, run_sc):
    tm = lg_ref.shape[0]

    @pl.when(pl.program_id(0) == 0)
    def _():
        run_sc[...] = jnp.zeros(run_sc.shape, F32)

    lane = lax.broadcasted_iota(jnp.int32, (tm, LANES), 1)
    lane_f = lane.astype(F32)
    work = jnp.where(lane < N_EXPERTS, lg_ref[...], -jnp.inf)
    tops, onehots = [], []
    eidx = jnp.zeros((tm, LANES), jnp.int32)
    for k in range(TOP_K):
        top = jnp.max(work, axis=1, keepdims=True)
        idx = jnp.min(jnp.where(work == top, lane_f, float(LANES)), axis=1, keepdims=True)
        hot = lane_f == idx
        work = jnp.where(hot, -jnp.inf, work)
        eidx = jnp.where(lane == k, idx.astype(jnp.int32), eidx)
        tops.append(top)
        onehots.append(hot)
    exps = [jnp.exp(top - tops[0]) for top in tops]
    denom = exps[0]
    for e in exps[1:]:
        denom = denom + e
    gate = jnp.zeros((tm, LANES), F32)
    for k in range(TOP_K):
        gate = jnp.where(lane == k, exps[k] / denom, gate)
    hits = onehots[0].astype(F32)
    for hot in onehots[1:]:
        hits = hits + hot.astype(F32)
    earlier = (lax.broadcasted_iota(jnp.int32, (tm, tm), 1) < lax.broadcasted_iota(jnp.int32, (tm, tm), 0))
    before = _dot(earlier.astype(BF16), hits.astype(BF16)) + run_sc[...]
    rank = jnp.zeros((tm, LANES), jnp.int32)
    for k in range(TOP_K):
        r_k = jnp.sum(jnp.where(onehots[k], before, 0.0), axis=1, keepdims=True)
        rank = jnp.where(lane == k, r_k.astype(jnp.int32), rank)
    run_sc[...] = run_sc[...] + jnp.sum(hits, axis=0, keepdims=True)
    eidx_ref[...] = eidx
    rank_ref[...] = rank
    gate_ref[...] = gate
    cnt_ref[...] = run_sc[...].astype(jnp.int32)


def _route(logits, tm):
    t = logits.shape[0]
    row = pl.BlockSpec((tm, LANES), lambda i: (i, 0))
    return pl.pallas_call(
        _route_kernel,
        grid=(t // tm,),
        in_specs=[row],
        out_specs=[row, row, row, pl.BlockSpec((1, LANES), lambda i: (0, 0))],
        out_shape=[jax.ShapeDtypeStruct((t, LANES), jnp.int32), jax.ShapeDtypeStruct((t, LANES), jnp.int32),
                   jax.ShapeDtypeStruct((t, LANES), F32), jax.ShapeDtypeStruct((1, LANES), jnp.int32)],
        scratch_shapes=[pltpu.VMEM((1, LANES), F32)],
        compiler_params=_params("arbitrary"),
        name="moe_route",
    )(logits)


def _row_copy(src_hbm, src_row, dst, dst_row, sem):
    return pltpu.make_async_copy(src_hbm.at[pl.ds(src_row, 1)], dst.at[pl.ds(dst_row, 1)], sem)


def _dispatch_kernel(dest_ref, x_ref, xs_hbm, sem, *, tm):
    def body(r, carry):
        for k in range(TOP_K):
            _row_copy(x_ref, r, xs_hbm, dest_ref[r * TOP_K + k], sem).start(priority=k % 2)
        return carry

    lax.fori_loop(0, tm, body, 0)
    for k in range(TOP_K):
        pltpu.make_async_copy(x_ref, xs_hbm.at[pl.ds(0, tm)], sem).wait()


def _dispatch(dest, x1, tm):
    t, d = x1.shape
    return pl.pallas_call(
        functools.partial(_dispatch_kernel, tm=tm),
        grid=(t // tm,),
        in_specs=[pl.BlockSpec((tm * TOP_K,), lambda i: (i,), memory_space=pltpu.SMEM),
                  pl.BlockSpec((tm, d), lambda i: (i, 0))],
        out_specs=pl.BlockSpec(memory_space=pl.ANY),
        out_shape=jax.ShapeDtypeStruct((t * TOP_K, d), x1.dtype),
        scratch_shapes=[pltpu.SemaphoreType.DMA(())],
        compiler_params=_params("arbitrary"),
        name="moe_dispatch",
    )(dest, x1)


def _expert_kernel(ie_ref, ib_ref, lo_ref, hi_ref, nw_ref, xs_ref, wup_ref, bup_ref, wdn_ref, bdn_ref, ys_ref,
                   wup_sc, wdn_sc):
    w = pl.program_id(0)
    bm = xs_ref.shape[0]
    d_ff = wdn_ref.shape[1]
    valid = w < nw_ref[0]
    new_expert = jnp.logical_or(w == 0, ie_ref[w] != ie_ref[jnp.maximum(w - 1, 0)])

    @pl.when(jnp.logical_and(valid, new_expert))
    def _():
        wup_sc[...] = wup_ref[0].astype(BF16)
        wdn_sc[...] = wdn_ref[0].astype(BF16)

    @pl.when(valid)
    def _():
        hb = _dot(xs_ref[...].astype(BF16), wup_sc[...]) + bup_ref[0]
        gl = jnp.minimum(hb[:, :d_ff], SWIGLU_LIMIT)
        up = jnp.clip(hb[:, d_ff:], -SWIGLU_LIMIT, SWIGLU_LIMIT)
        yb = (up + 1.0) * (gl * _sigmoid(SWIGLU_ALPHA * gl))
        y = _dot(yb.astype(BF16), wdn_sc[...]) + bdn_ref[0]
        rows = lax.broadcasted_iota(jnp.int32, (bm, 1), 0)
        mine = jnp.logical_and(rows >= lo_ref[w], rows < hi_ref[w])
        first_visit = lo_ref[w] == 0

        @pl.when(first_visit)
        def _():
            ys_ref[...] = jnp.where(mine, y, 0.0)

        @pl.when(jnp.logical_not(first_visit))
        def _():
            ys_ref[...] = jnp.where(mine, y, ys_ref[...])


def _experts(item_e, item_blk, item_lo, item_hi, n_items, xs, w_up, b_up, w_down, b_down, layer, bm):
    n, d = xs.shape
    f2 = w_up.shape[3]
    n_max = item_e.shape[0]
    cur = lambda w, nw: jnp.minimum(w, nw[0] - 1)
    blk = lambda w, ie, ib, lo, hi, nw: (ib[cur(w, nw)], 0)
    exp3 = lambda w, ie, ib, lo, hi, nw: (layer, ie[cur(w, nw)], 0, 0)
    grid_spec = pltpu.PrefetchScalarGridSpec(
        num_scalar_prefetch=5,
        grid=(n_max,),
        in_specs=[pl.BlockSpec((bm, d), blk),
                  pl.BlockSpec((None, 1, d, f2), exp3), pl.BlockSpec((None, 1, 1, f2), exp3),
                  pl.BlockSpec((None, 1, f2 // 2, d), exp3), pl.BlockSpec((None, 1, 1, d), exp3)],
        out_specs=pl.BlockSpec((bm, d), blk),
        scratch_shapes=[pltpu.VMEM((d, f2), BF16), pltpu.VMEM((f2 // 2, d), BF16)],
    )
    return pl.pallas_call(
        _expert_kernel,
        grid_spec=grid_spec,
        out_shape=jax.ShapeDtypeStruct((n, d), F32),
        compiler_params=_params("arbitrary"),
        name="moe_experts",
    )(item_e, item_blk, item_lo, item_hi, n_items, xs, w_up, b_up, w_down, b_down)


def _post_moe_kernel(dest_ref, x1_ref, gate_ref, p_ref, g_ref, b_ref, wg_ref, wp_ref, ys_hbm, out_ref,
                     buf, sem, *, alpha, batch):
    ts = x1_ref.shape[0]
    b = pl.program_id(1)

    def body(s, carry):
        base = (s * batch + b) * TOP_K
        for k in range(TOP_K):
            _row_copy(ys_hbm, dest_ref[base + k], buf.at[k], s, sem).start(priority=k % 2)
        return carry

    lax.fori_loop(0, ts, body, 0)
    for k in range(TOP_K):
        pltpu.make_async_copy(ys_hbm.at[pl.ds(0, ts)], buf.at[k], sem).wait()
    gates = gate_ref[...]
    ffn = gates[:, 0:1] * buf[0]
    for k in range(1, TOP_K):
        ffn = ffn + gates[:, k:k + 1] * buf[k]
    x2 = _layer_norm(alpha * x1_ref[...] + ffn, g_ref[...], b_ref[...])
    gate = _sigmoid(_dot(x2.astype(BF16), wg_ref[...]))
    out_ref[...] = x2 + gate * _dot(p_ref[...].astype(BF16), wp_ref[...])


def _post_moe(dest, x1, gates, ys, p, layer, g, b, wg, wp, alpha, seq, batch, ts, batch_major_out):
    d = wg.shape[0]
    col = functools.partial(_col_spec, ts)
    if batch_major_out:
        out_spec = pl.BlockSpec((None, ts, d), lambda i, bb: (bb, i, 0))
        out_shape = jax.ShapeDtypeStruct((batch, seq, d), F32)
    else:
        out_spec = col(d)
        out_shape = jax.ShapeDtypeStruct((seq, batch * d), F32)
    return pl.pallas_call(
        functools.partial(_post_moe_kernel, alpha=alpha, batch=batch),
        grid=(seq // ts, batch),
        in_specs=[pl.BlockSpec((ts * batch * TOP_K,), lambda i, bb: (i,), memory_space=pltpu.SMEM),
                  col(d), col(LANES),
                  pl.BlockSpec((None, None, ts, p.shape[-1]), lambda i, bb: (layer, bb, i, 0)),
                  _full(g.shape), _full(b.shape), _full(wg.shape), _full(wp.shape),
                  pl.BlockSpec(memory_space=pl.ANY)],
        out_specs=out_spec,
        out_shape=out_shape,
        scratch_shapes=[pltpu.VMEM((TOP_K, ts, d), F32), pltpu.SemaphoreType.DMA(())],
        compiler_params=_params("arbitrary", "arbitrary"),
        name="post_moe_ple",
    )(dest, x1, gates, p, g, b, wg, wp, ys)


def _moe_plan(counts, n, bm):
    n_blocks = n // bm
    n_max = n_blocks + N_EXPERTS - 1
    ends = jnp.cumsum(counts)
    starts = ends - counts
    first_blk = starts // bm
    n_items_e = jnp.where(counts > 0, (ends - 1) // bm - first_blk + 1, 0)
    item_end = jnp.cumsum(n_items_e)
    item_start = item_end - n_items_e
    n_items = item_end[-1]
    w = jnp.minimum(jnp.arange(n_max, dtype=jnp.int32), n_items - 1)
    item_e = jnp.sum((w[:, None] >= item_end[None, :]).astype(jnp.int32), axis=1)
    item_blk = (first_blk[item_e] + w - item_start[item_e]).astype(jnp.int32)
    item_lo = jnp.maximum(starts[item_e] - item_blk * bm, 0).astype(jnp.int32)
    item_hi = jnp.minimum(ends[item_e] - item_blk * bm, bm).astype(jnp.int32)
    return starts, item_e.astype(jnp.int32), item_blk, item_lo, item_hi, n_items.astype(jnp.int32).reshape(1)


def _moe_dispatch(x1, logits, w_up, b_up, w_down, b_down, layer, tm, bm):
    t, d = x1.shape
    eidx, rank, gates, counts = _route(logits, tm)
    starts, item_e, item_blk, item_lo, item_hi, n_items = _moe_plan(counts[0, :N_EXPERTS], t * TOP_K, bm)
    dest = (starts[eidx[:, :TOP_K]] + rank[:, :TOP_K]).reshape(t * TOP_K).astype(jnp.int32)
    xs = _dispatch(dest, x1, tm)
    ys = _experts(item_e, item_blk, item_lo, item_hi, n_items, xs, w_up, b_up, w_down, b_down, layer, bm)
    return dest, gates, ys


def _pick_tile(n, target, quantum):
    tile = min(n, target)
    while n % tile or tile % quantum:
        tile -= quantum
    return tile


def kernel(x, p, positions, mla_w_in, mla_q_norm, mla_kv_norm, mla_w_uq, mla_w_ukv, mla_w_o, lru_w_in, lru_conv_w, lru_conv_b, lru_w_a, lru_b_a, lru_w_x, lru_b_x, lru_lambda, lru_w_out, ln1_g, ln1_b, ln2_g, ln2_b, moe_w_router, moe_b_router, moe_w_up, moe_b_up, moe_w_down, moe_b_down, ple_w_gate, ple_w_proj):
    batch, seq, d = x.shape
    depth = ln1_g.shape[0]
    t = batch * seq
    assert batch == SUBLANES, "the recurrence kernel maps the batch onto the sublanes of a vreg"
    alpha = (2.0 * depth) ** 0.25
    tm = _pick_tile(t, 512, SUBLANES * batch)
    attn_blk = _pick_tile(seq, 512, LANES)
    moe_bm = _pick_tile(t * TOP_K, 512, SUBLANES)
    combine_ts = _pick_tile(seq, 256, SUBLANES)

    xt = x
    half = QK_ROPE_DIM // 2
    inv_freq = jnp.exp(-math.log(ROPE_THETA) * jnp.arange(half, dtype=F32) / half)
    rope_freq = jnp.concatenate([inv_freq, inv_freq, jnp.zeros((LANES - QK_ROPE_DIM,), F32)]).reshape(1, LANES)
    pos_lanes = jnp.broadcast_to(jnp.transpose(positions).astype(F32)[:, :, None],
                                 (seq, batch, LANES)).reshape(seq, batch * LANES)

    row2 = lambda a: a.reshape(1, -1)
    scale = 1.0 / math.sqrt(QK_NOPE_DIM + QK_ROPE_DIM)

    for layer in range(depth):
        j = layer // 2
        wr = jnp.pad(moe_w_router[layer], ((0, 0), (0, LANES - N_EXPERTS)))
        wr_hi, wr_lo = _split_bf16(wr)
        br = row2(jnp.pad(moe_b_router[layer], (0, LANES - N_EXPERTS)))
        g1, b1 = row2(ln1_g[layer]), row2(ln1_b[layer])
        if layer % 2 == 0:
            q_lora, kv_lora = mla_q_norm.shape[1], mla_kv_norm.shape[1]
            w_in = mla_w_in[j]
            win_p = jnp.pad(w_in, ((0, 0), (0, LANES - QK_ROPE_DIM))).astype(BF16)
            wuq = mla_w_uq[j].reshape(q_lora, MLA_HEADS, QK_NOPE_DIM + QK_ROPE_DIM)
            wuq_p = jnp.pad(wuq, ((0, 0), (0, 0), (0, QK_PAD - QK_NOPE_DIM - QK_ROPE_DIM)))
            wuq_p = wuq_p.reshape(q_lora, MLA_HEADS * QK_PAD).astype(BF16)
            q, k, vt = _mla_proj(xt, pos_lanes, rope_freq, win_p,
                                 row2(mla_q_norm[j] * scale), row2(mla_kv_norm[j]),
                                 wuq_p, mla_w_ukv[j].astype(BF16), seq, batch, attn_blk)
            o = _attention(q, k, vt, seq, batch, attn_blk, 2)
            x1, logits = _post_attn(xt, o, mla_w_o[j].astype(BF16), g1, b1,
                                    wr_hi, wr_lo, br, alpha, seq, batch, attn_blk)
        else:
            x1, logits = _lru_block(
                xt.reshape(t, d), lru_w_in[j].astype(BF16), lru_conv_w[j], row2(lru_conv_b[j]),
                lru_w_a[j].astype(BF16), row2(lru_b_a[j]), lru_w_x[j].astype(BF16), row2(lru_b_x[j]),
                row2(lru_lambda[j]), lru_w_out[j].astype(BF16), g1, b1, wr_hi, wr_lo, br, alpha, batch, tm)
        x1 = x1.reshape(t, d)
        dest, gates, ys = _moe_dispatch(x1, logits.reshape(t, LANES), moe_w_up, moe_b_up[:, :, None, :],
                                        moe_w_down, moe_b_down[:, :, None, :], layer, tm, moe_bm)
        xt = _post_moe(dest, x1.reshape(seq, batch * d), gates.reshape(seq, batch * LANES), ys, p, layer,
                       row2(ln2_g[layer]), row2(ln2_b[layer]), ple_w_gate[layer].astype(BF16),
                       ple_w_proj[layer].astype(BF16), alpha, seq, batch, combine_ts, layer == depth - 1)
    return xt
```

```python
import functools
import math

import jax
import jax.numpy as jnp
from jax import lax
from jax.experimental import pallas as pl
from jax.experimental.pallas import tpu as pltpu

F32 = jnp.float32
BF16 = jnp.bfloat16

MLA_HEADS = 8
QK_NOPE_DIM = 128
QK_ROPE_DIM = 64
V_HEAD_DIM = 128
ROPE_THETA = 10000.0
LRU_BLOCKS = 4
CONV_WIDTH = 4
LRU_C = 8.0
N_EXPERTS = 32
TOP_K = 4
SWIGLU_LIMIT = 7.0
SWIGLU_ALPHA = 1.702
LN_EPS = 1e-5
RMS_EPS = 1e-6

LANES = 128
SUBLANES = 8
QK_PAD = 2 * LANES
VMEM_LIMIT_BYTES = 56 * 1024 * 1024


def _params(*sem, flags=None):
    return pltpu.CompilerParams(dimension_semantics=sem, vmem_limit_bytes=VMEM_LIMIT_BYTES, flags=flags)


def _dot(a, b):
    return jnp.dot(a, b, preferred_element_type=F32)


def _sigmoid(z):
    return 1.0 / (1.0 + jnp.exp(-z))


def _layer_norm(y, g, b):
    mu = jnp.mean(y, axis=-1, keepdims=True)
    yc = y - mu
    var = jnp.mean(yc * yc, axis=-1, keepdims=True)
    return yc * lax.rsqrt(var + LN_EPS) * g + b


def _rms_norm(y, g):
    ms = jnp.mean(y * y, axis=-1, keepdims=True)
    return y * lax.rsqrt(ms + RMS_EPS) * g


def _split_bf16(a):
    hi = a.astype(BF16)
    lo = (a - hi.astype(F32)).astype(BF16)
    return hi, lo


def _router_logits(x1, wr_hi, wr_lo, br):
    hi, lo = _split_bf16(x1)
    return _dot(hi, wr_hi) + (_dot(hi, wr_lo) + _dot(lo, wr_hi)) + br


def _full(shape):
    return pl.BlockSpec(shape, lambda *_: (0,) * len(shape))


def _rope(blk, cos, sin_a, sin_b):
    return (blk * cos + pltpu.roll(blk, LANES - QK_ROPE_DIM // 2, 1) * sin_a
            + pltpu.roll(blk, QK_ROPE_DIM // 2, 1) * sin_b)


def _mla_proj_kernel(x_ref, pos_ref, freq_ref, win_ref, qn_ref, kvn_ref, wuq_ref, wukv_ref,
                     q_ref, k_ref, vt_ref, *, q_lora, kv_lora):
    lat = _dot(x_ref[...].astype(BF16), win_ref[...])
    c_q = _rms_norm(lat[:, :q_lora], qn_ref[...])
    c_kv = _rms_norm(lat[:, q_lora:q_lora + kv_lora], kvn_ref[...])
    ang = pos_ref[...] * freq_ref[...]
    lane = lax.broadcasted_iota(jnp.int32, ang.shape, 1)
    cos, sin = jnp.cos(ang), jnp.sin(ang)
    half = QK_ROPE_DIM // 2
    sin_a = jnp.where(lane < half, -sin, 0.0)
    sin_b = jnp.where(jnp.logical_and(lane >= half, lane < QK_ROPE_DIM), sin, 0.0)
    k_rope = _rope(lat[:, q_lora + kv_lora:], cos, sin_a, sin_b).astype(BF16)
    q = _dot(c_q.astype(BF16), wuq_ref[...])
    kv = _dot(c_kv.astype(BF16), wukv_ref[...])
    for h in range(MLA_HEADS):
        lo = h * QK_PAD
        mid = lo + LANES
        hi = lo + QK_PAD
        q_ref[:, lo:mid] = q[:, lo:mid].astype(BF16)
        q_ref[:, mid:hi] = _rope(q[:, mid:hi], cos, sin_a, sin_b).astype(BF16)
        k_ref[:, lo:mid] = kv[:, lo:mid].astype(BF16)
        k_ref[:, mid:hi] = k_rope
        vt_ref[h * V_HEAD_DIM:(h + 1) * V_HEAD_DIM, :] = jnp.transpose(kv[:, mid:hi]).astype(BF16)


def _col_spec(ts, c):
    return pl.BlockSpec((ts, c), lambda i, b: (i, b))


def _x_spec(x, ts, d):
    if x.ndim == 3:
        return pl.BlockSpec((None, ts, d), lambda i, b: (b, i, 0))
    return _col_spec(ts, d)


def _mla_proj(x, pos, freq, win_p, qn, kvn, wuq_p, wukv, seq, batch, ts):
    d = win_p.shape[0]
    q_lora, kv_lora = qn.shape[1], kvn.shape[1]
    hq = MLA_HEADS * QK_PAD
    hv = MLA_HEADS * V_HEAD_DIM
    col = functools.partial(_col_spec, ts)
    return pl.pallas_call(
        functools.partial(_mla_proj_kernel, q_lora=q_lora, kv_lora=kv_lora),
        grid=(seq // ts, batch),
        in_specs=[_x_spec(x, ts, d), col(LANES), _full(freq.shape), _full(win_p.shape), _full(qn.shape),
                  _full(kvn.shape), _full(wuq_p.shape), _full(wukv.shape)],
        out_specs=[col(hq), col(hq), pl.BlockSpec((hv, ts), lambda i, b: (b, i))],
        out_shape=[jax.ShapeDtypeStruct((seq, batch * hq), BF16), jax.ShapeDtypeStruct((seq, batch * hq), BF16),
                   jax.ShapeDtypeStruct((batch * hv, seq), BF16)],
        compiler_params=_params("parallel", "parallel"),
        name="mla_proj",
    )(x, pos, freq, win_p, qn, kvn, wuq_p, wukv)


ONES_ROWS = 16


def _attn_kernel(q_ref, k_ref, vt_ref, o_ref, m_sc, acc_sc, st_a, st_b, *, blk, heads):
    qi = pl.program_id(1)
    m_sc[...] = jnp.full(m_sc.shape, -jnp.inf, F32)
    acc_sc[...] = jnp.zeros(acc_sc.shape, F32)
    ones = jnp.ones((ONES_ROWS, blk), BF16)

    def scores(ki, st_ref):
        start = pl.multiple_of(ki * blk, blk)
        for h in range(heads):
            k = k_ref[pl.ds(start, blk), h * QK_PAD:(h + 1) * QK_PAD]
            st_ref[h] = lax.dot_general(k, q_ref[:, h * QK_PAD:(h + 1) * QK_PAD], (((1,), (1,)), ((), ())),
                                        preferred_element_type=F32)

    def consume(ki, st_ref, diagonal):
        start = pl.multiple_of(ki * blk, blk)
        for h in range(heads):
            vt = vt_ref[h * V_HEAD_DIM:(h + 1) * V_HEAD_DIM, pl.ds(start, blk)]
            vt = jnp.concatenate([vt, ones], axis=0)
            st = st_ref[h]
            if diagonal:
                keys = lax.broadcasted_iota(jnp.int32, st.shape, 0)
                queries = lax.broadcasted_iota(jnp.int32, st.shape, 1)
                st = jnp.where(keys <= queries, st, -jnp.inf)
            m_prev = m_sc[h]
            m_new = jnp.maximum(m_prev, jnp.max(st, axis=0, keepdims=True))
            alpha = jnp.exp(m_prev - m_new)
            p = jnp.exp(st - m_new).astype(BF16)
            acc_sc[h] = alpha * acc_sc[h] + _dot(vt, p)
            m_sc[h] = m_new

    pairs = qi // 2
    scores(0, st_a)

    def body(j, carry):
        scores(2 * j + 1, st_b)
        consume(2 * j, st_a, False)
        scores(2 * j + 2, st_a)
        consume(2 * j + 1, st_b, False)
        return carry

    lax.fori_loop(0, pairs, body, 0)

    @pl.when(qi == 2 * pairs)
    def _():
        consume(qi, st_a, True)

    @pl.when(qi != 2 * pairs)
    def _():
        scores(qi, st_b)
        consume(qi - 1, st_a, False)
        consume(qi, st_b, True)

    for h in range(heads):
        acc = acc_sc[h]
        out_t = acc[:V_HEAD_DIM, :] / acc[V_HEAD_DIM:V_HEAD_DIM + 1, :]
        o_ref[:, h * V_HEAD_DIM:(h + 1) * V_HEAD_DIM] = jnp.transpose(out_t).astype(o_ref.dtype)


def _attention(q, k, vt, seq, batch, blk, heads):
    groups = batch * MLA_HEADS // heads
    return pl.pallas_call(
        functools.partial(_attn_kernel, blk=blk, heads=heads),
        grid=(groups, seq // blk),
        in_specs=[pl.BlockSpec((blk, heads * QK_PAD), lambda g, i: (i, g)),
                  pl.BlockSpec((seq, heads * QK_PAD), lambda g, i: (0, g)),
                  pl.BlockSpec((heads * V_HEAD_DIM, seq), lambda g, i: (g, 0))],
        out_specs=pl.BlockSpec((blk, heads * V_HEAD_DIM), lambda g, i: (i, g)),
        out_shape=jax.ShapeDtypeStruct((seq, batch * MLA_HEADS * V_HEAD_DIM), BF16),
        scratch_shapes=[pltpu.VMEM((heads, 1, blk), F32),
                        pltpu.VMEM((heads, V_HEAD_DIM + ONES_ROWS, blk), F32),
                        pltpu.VMEM((heads, blk, blk), F32), pltpu.VMEM((heads, blk, blk), F32)],
        compiler_params=_params("parallel", "arbitrary"),
        name="mla_attention",
    )(q, k, vt)


def _mix_epilogue(x, mix, g_ref, b_ref, wrh_ref, wrl_ref, br_ref, x1_ref, lg_ref, alpha):
    x1 = _layer_norm(alpha * x + mix, g_ref[...], b_ref[...])
    x1_ref[...] = x1
    lg_ref[...] = _router_logits(x1, wrh_ref[...], wrl_ref[...], br_ref[...])


def _post_attn_kernel(x_ref, o_ref, wo_ref, g_ref, b_ref, wrh_ref, wrl_ref, br_ref, x1_ref, lg_ref, *, alpha):
    mix = _dot(o_ref[...], wo_ref[...])
    _mix_epilogue(x_ref[...], mix, g_ref, b_ref, wrh_ref, wrl_ref, br_ref, x1_ref, lg_ref, alpha)


def _post_attn(x, o, wo, g, b, wr_hi, wr_lo, br, alpha, seq, batch, ts):
    hv, d = wo.shape
    col = functools.partial(_col_spec, ts)
    return pl.pallas_call(
        functools.partial(_post_attn_kernel, alpha=alpha),
        grid=(seq // ts, batch),
        in_specs=[_x_spec(x, ts, d), col(hv), _full(wo.shape), _full(g.shape), _full(b.shape),
                  _full(wr_hi.shape), _full(wr_lo.shape), _full(br.shape)],
        out_specs=[col(d), col(LANES)],
        out_shape=[jax.ShapeDtypeStruct((seq, batch * d), F32), jax.ShapeDtypeStruct((seq, batch * LANES), F32)],
        compiler_params=_params("parallel", "parallel"),
        name="post_attention",
    )(x, o, wo, g, b, wr_hi, wr_lo, br)


def _lru_kernel(x_ref, win_ref, cw_ref, cb_ref, wa_ref, ba_ref, wx_ref, bx_ref, lam_ref, wout_ref,
                g_ref, b_ref, wrh_ref, wrl_ref, br_ref, x1_ref, lg_ref,
                ucarry_sc, hcarry_sc, a_sc, b_sc, h_sc, *, alpha, batch):
    tm, width = a_sc.shape
    halo = (CONV_WIDTH - 1) * batch
    blk_w = width // LRU_BLOCKS

    @pl.when(pl.program_id(0) == 0)
    def _():
        ucarry_sc[...] = jnp.zeros(ucarry_sc.shape, F32)
        hcarry_sc[...] = jnp.zeros(hcarry_sc.shape, F32)

    x = x_ref[...]
    gu = _dot(x.astype(BF16), win_ref[...])
    gate = gu[:, :width]
    u = gu[:, width:]
    u_ext = jnp.concatenate([ucarry_sc[...], u], axis=0)
    ucarry_sc[...] = u[tm - halo:, :]
    cw = cw_ref[...]
    uc = cb_ref[...] + cw[0:1, :] * u_ext[0:tm, :]
    for j in range(1, CONV_WIDTH):
        uc = uc + cw[j:j + 1, :] * u_ext[j * batch:j * batch + tm, :]
    ucb = uc.astype(BF16)
    ra = jnp.concatenate([_dot(ucb[:, n * blk_w:(n + 1) * blk_w], wa_ref[n]) for n in range(LRU_BLOCKS)], axis=1)
    rx = jnp.concatenate([_dot(ucb[:, n * blk_w:(n + 1) * blk_w], wx_ref[n]) for n in range(LRU_BLOCKS)], axis=1)
    r = _sigmoid(ra + ba_ref[...])
    gi = _sigmoid(rx + bx_ref[...])
    z = -lam_ref[...]
    softplus = jnp.maximum(z, 0.0) + jnp.log1p(jnp.exp(-jnp.abs(z)))
    log_a = (-LRU_C) * r * softplus
    a_sc[...] = jnp.exp(log_a)
    th = jnp.tanh(log_a)
    b_sc[...] = jnp.sqrt(-2.0 * th / (1.0 - th)) * gi * uc

    def body(t, h):
        rows = pl.ds(pl.multiple_of(t * batch, batch), batch)
        h = a_sc[rows, :] * h + b_sc[rows, :]
        h_sc[rows, :] = h
        return h

    hcarry_sc[...] = lax.fori_loop(0, tm // batch, body, hcarry_sc[...], unroll=8)
    c = math.sqrt(2.0 / math.pi)
    gelu = 0.5 * gate * (1.0 + jnp.tanh(c * (gate + 0.044715 * (gate * gate * gate))))
    y = gelu * h_sc[...]
    mix = _dot(y.astype(BF16), wout_ref[...])
    _mix_epilogue(x, mix, g_ref, b_ref, wrh_ref, wrl_ref, br_ref, x1_ref, lg_ref, alpha)


def _lru_block(x, win, cw, cb, wa, ba, wx, bx, lam, wout, g, b, wr_hi, wr_lo, br, alpha, batch, tm):
    t, d = x.shape
    width = wout.shape[0]
    row = lambda c: pl.BlockSpec((tm, c), lambda i: (i, 0))
    consts = (win, cw, cb, wa, ba, wx, bx, lam, wout, g, b, wr_hi, wr_lo, br)
    return pl.pallas_call(
        functools.partial(_lru_kernel, alpha=alpha, batch=batch),
        grid=(t // tm,),
        in_specs=[row(d)] + [_full(c.shape) for c in consts],
        out_specs=[row(d), row(LANES)],
        out_shape=[jax.ShapeDtypeStruct((t, d), F32), jax.ShapeDtypeStruct((t, LANES), F32)],
        scratch_shapes=[pltpu.VMEM(((CONV_WIDTH - 1) * batch, width), F32), pltpu.VMEM((batch, width), F32),
                        pltpu.VMEM((tm, width), F32), pltpu.VMEM((tm, width), F32), pltpu.VMEM((tm, width), F32)],
        compiler_params=_params("arbitrary"),
        name="rglru_block",
    )(x, *consts)


def _route_kernel(lg_ref, eidx_ref, rank_ref, gate_ref, cnt_ref, run_sc):
    tm = lg_ref.shape[0]

    @pl.when(pl.program_id(0) == 0)
    def _():
        run_sc[...] = jnp.zeros(run_sc.shape, F32)

    lane = lax.broadcasted_iota(jnp.int32, (tm, LANES), 1)
    lane_f = lane.astype(F32)
    work = jnp.where(lane < N_EXPERTS, lg_ref[...], -jnp.inf)
    tops, onehots = [], []
    eidx = jnp.zeros((tm, LANES), jnp.int32)
    for k in range(TOP_K):
        top = jnp.max(work, axis=1, keepdims=True)
        idx = jnp.min(jnp.where(work == top, lane_f, float(LANES)), axis=1, keepdims=True)
        hot = lane_f == idx
        work = jnp.where(hot, -jnp.inf, work)
        eidx = jnp.where(lane == k, idx.astype(jnp.int32), eidx)
        tops.append(top)
        onehots.append(hot)
    exps = [jnp.exp(top - tops[0]) for top in tops]
    denom = exps[0]
    for e in exps[1:]:
        denom = denom + e
    gate = jnp.zeros((tm, LANES), F32)
    for k in range(TOP_K):
        gate = jnp.where(lane == k, exps[k] / denom, gate)
    hits = onehots[0].astype(F32)
    for hot in onehots[1:]:
        hits = hits + hot.astype(F32)
    earlier = (lax.broadcasted_iota(jnp.int32, (tm, tm), 1) < lax.broadcasted_iota(jnp.int32, (tm, tm), 0))
    before = _dot(earlier.astype(BF16), hits.astype(BF16)) + run_sc[...]
    rank = jnp.zeros((tm, LANES), jnp.int32)
    for k in range(TOP_K):
        r_k = jnp.sum(jnp.where(onehots[k], before, 0.0), axis=1, keepdims=True)
        rank = jnp.where(lane == k, r_k.astype(jnp.int32), rank)
    run_sc[...] = run_sc[...] + jnp.sum(hits, axis=0, keepdims=True)
    eidx_ref[...] = eidx
    rank_ref[...] = rank
    gate_ref[...] = gate
    cnt_ref[...] = run_sc[...].astype(jnp.int32)


def _route(logits, tm):
    t = logits.shape[0]
    row = pl.BlockSpec((tm, LANES), lambda i: (i, 0))
    return pl.pallas_call(
        _route_kernel,
        grid=(t // tm,),
        in_specs=[row],
        out_specs=[row, row, row, pl.BlockSpec((1, LANES), lambda i: (0, 0))],
        out_shape=[jax.ShapeDtypeStruct((t, LANES), jnp.int32), jax.ShapeDtypeStruct((t, LANES), jnp.int32),
                   jax.ShapeDtypeStruct((t, LANES), F32), jax.ShapeDtypeStruct((1, LANES), jnp.int32)],
        scratch_shapes=[pltpu.VMEM((1, LANES), F32)],
        compiler_params=_params("arbitrary"),
        name="moe_route",
    )(logits)


def _row_copy(src_hbm, src_row, dst, dst_row, sem):
    return pltpu.make_async_copy(src_hbm.at[pl.ds(src_row, 1)], dst.at[pl.ds(dst_row, 1)], sem)


def _dispatch_kernel(dest_ref, x_ref, xs_hbm, sem, *, cols):
    rows = x_ref.shape[0]
    c = pl.program_id(1)

    def body(r, carry):
        base = (r * cols + c) * TOP_K
        for k in range(TOP_K):
            _row_copy(x_ref, r, xs_hbm, dest_ref[base + k], sem).start()
        return carry

    lax.fori_loop(0, rows, body, 0)
    for k in range(TOP_K):
        pltpu.make_async_copy(x_ref, xs_hbm.at[pl.ds(0, rows)], sem).wait()


def _dispatch(dest, x1, d, rows):
    n_rows = x1.shape[0]
    cols = x1.shape[1] // d
    return pl.pallas_call(
        functools.partial(_dispatch_kernel, cols=cols),
        grid=(n_rows // rows, cols),
        in_specs=[pl.BlockSpec((rows * cols * TOP_K,), lambda i, c: (i,), memory_space=pltpu.SMEM),
                  pl.BlockSpec((rows, d), lambda i, c: (i, c))],
        out_specs=pl.BlockSpec(memory_space=pl.ANY),
        out_shape=jax.ShapeDtypeStruct((n_rows * cols * TOP_K, d), x1.dtype),
        scratch_shapes=[pltpu.SemaphoreType.DMA(())],
        compiler_params=_params("arbitrary", "arbitrary"),
        name="moe_dispatch",
    )(dest, x1)


def _expert_kernel(ie_ref, ib_ref, lo_ref, hi_ref, nw_ref, xs_ref, wup_ref, bup_ref, wdn_ref, bdn_ref, ys_ref,
                   wup_sc, wdn_sc):
    w = pl.program_id(0)
    bm = xs_ref.shape[0]
    d_ff = wdn_ref.shape[1]
    valid = w < nw_ref[0]
    new_expert = jnp.logical_or(w == 0, ie_ref[w] != ie_ref[jnp.maximum(w - 1, 0)])

    @pl.when(jnp.logical_and(valid, new_expert))
    def _():
        wup_sc[...] = wup_ref[0].astype(BF16)
        wdn_sc[...] = wdn_ref[0].astype(BF16)

    @pl.when(valid)
    def _():
        hb = _dot(xs_ref[...].astype(BF16), wup_sc[...]) + bup_ref[0]
        gl = jnp.minimum(hb[:, :d_ff], SWIGLU_LIMIT)
        up = jnp.clip(hb[:, d_ff:], -SWIGLU_LIMIT, SWIGLU_LIMIT)
        yb = (up + 1.0) * (gl * _sigmoid(SWIGLU_ALPHA * gl))
        y = _dot(yb.astype(BF16), wdn_sc[...]) + bdn_ref[0]
        rows = lax.broadcasted_iota(jnp.int32, (bm, 1), 0)
        mine = jnp.logical_and(rows >= lo_ref[w], rows < hi_ref[w])
        first_visit = lo_ref[w] == 0

        @pl.when(first_visit)
        def _():
            ys_ref[...] = jnp.where(mine, y, 0.0)

        @pl.when(jnp.logical_not(first_visit))
        def _():
            ys_ref[...] = jnp.where(mine, y, ys_ref[...])


def _experts(item_e, item_blk, item_lo, item_hi, n_items, xs, w_up, b_up, w_down, b_down, layer, bm):
    n, d = xs.shape
    f2 = w_up.shape[3]
    n_max = item_e.shape[0]
    cur = lambda w, nw: jnp.minimum(w, nw[0] - 1)
    blk = lambda w, ie, ib, lo, hi, nw: (ib[cur(w, nw)], 0)
    exp3 = lambda w, ie, ib, lo, hi, nw: (layer, ie[cur(w, nw)], 0, 0)
    grid_spec = pltpu.PrefetchScalarGridSpec(
        num_scalar_prefetch=5,
        grid=(n_max,),
        in_specs=[pl.BlockSpec((bm, d), blk),
                  pl.BlockSpec((None, 1, d, f2), exp3), pl.BlockSpec((None, 1, 1, f2), exp3),
                  pl.BlockSpec((None, 1, f2 // 2, d), exp3), pl.BlockSpec((None, 1, 1, d), exp3)],
        out_specs=pl.BlockSpec((bm, d), blk),
        scratch_shapes=[pltpu.VMEM((d, f2), BF16), pltpu.VMEM((f2 // 2, d), BF16)],
    )
    return pl.pallas_call(
        _expert_kernel,
        grid_spec=grid_spec,
        out_shape=jax.ShapeDtypeStruct((n, d), F32),
        compiler_params=_params("arbitrary"),
        name="moe_experts",
    )(item_e, item_blk, item_lo, item_hi, n_items, xs, w_up, b_up, w_down, b_down)


def _post_moe_kernel(dest_ref, x1_ref, gate_ref, p_ref, g_ref, b_ref, wg_ref, wp_ref, ys_hbm, out_ref,
                     buf, sem, *, alpha, batch):
    ts = x1_ref.shape[0]
    b = pl.program_id(1)

    def body(s, carry):
        base = (s * batch + b) * TOP_K
        for k in range(TOP_K):
            _row_copy(ys_hbm, dest_ref[base + k], buf.at[k], s, sem).start()
        return carry

    lax.fori_loop(0, ts, body, 0)
    for k in range(TOP_K):
        pltpu.make_async_copy(ys_hbm.at[pl.ds(0, ts)], buf.at[k], sem).wait()
    gates = gate_ref[...]
    ffn = gates[:, 0:1] * buf[0]
    for k in range(1, TOP_K):
        ffn = ffn + gates[:, k:k + 1] * buf[k]
    x2 = _layer_norm(alpha * x1_ref[...] + ffn, g_ref[...], b_ref[...])
    gate = _sigmoid(_dot(x2.astype(BF16), wg_ref[...]))
    out_ref[...] = x2 + gate * _dot(p_ref[...].astype(BF16), wp_ref[...])


def _post_moe(dest, x1, gates, ys, p, layer, g, b, wg, wp, alpha, seq, batch, ts, batch_major_out):
    d = wg.shape[0]
    col = functools.partial(_col_spec, ts)
    if batch_major_out:
        out_spec = pl.BlockSpec((None, ts, d), lambda i, bb: (bb, i, 0))
        out_shape = jax.ShapeDtypeStruct((batch, seq, d), F32)
    else:
        out_spec = col(d)
        out_shape = jax.ShapeDtypeStruct((seq, batch * d), F32)
    return pl.pallas_call(
        functools.partial(_post_moe_kernel, alpha=alpha, batch=batch),
        grid=(seq // ts, batch),
        in_specs=[pl.BlockSpec((ts * batch * TOP_K,), lambda i, bb: (i,), memory_space=pltpu.SMEM),
                  col(d), col(LANES),
                  pl.BlockSpec((None, None, ts, p.shape[-1]), lambda i, bb: (layer, bb, i, 0)),
                  _full(g.shape), _full(b.shape), _full(wg.shape), _full(wp.shape),
                  pl.BlockSpec(memory_space=pl.ANY)],
        out_specs=out_spec,
        out_shape=out_shape,
        scratch_shapes=[pltpu.VMEM((TOP_K, ts, d), F32), pltpu.SemaphoreType.DMA(())],
        compiler_params=_params("arbitrary", "arbitrary"),
        name="post_moe_ple",
    )(dest, x1, gates, p, g, b, wg, wp, ys)


def _moe_plan(counts, n, bm):
    n_blocks = n // bm
    n_max = n_blocks + N_EXPERTS - 1
    ends = jnp.cumsum(counts)
    starts = ends - counts
    first_blk = starts // bm
    n_items_e = jnp.where(counts > 0, (ends - 1) // bm - first_blk + 1, 0)
    item_end = jnp.cumsum(n_items_e)
    item_start = item_end - n_items_e
    n_items = item_end[-1]
    w = jnp.minimum(jnp.arange(n_max, dtype=jnp.int32), n_items - 1)
    item_e = jnp.sum((w[:, None] >= item_end[None, :]).astype(jnp.int32), axis=1)
    item_blk = (first_blk[item_e] + w - item_start[item_e]).astype(jnp.int32)
    item_lo = jnp.maximum(starts[item_e] - item_blk * bm, 0).astype(jnp.int32)
    item_hi = jnp.minimum(ends[item_e] - item_blk * bm, bm).astype(jnp.int32)
    return starts, item_e.astype(jnp.int32), item_blk, item_lo, item_hi, n_items.astype(jnp.int32).reshape(1)


def _moe_dispatch(x1, logits, w_up, b_up, w_down, b_down, layer, tm, rows, bm):
    t = logits.shape[0]
    d = w_up.shape[2]
    eidx, rank, gates, counts = _route(logits, tm)
    starts, item_e, item_blk, item_lo, item_hi, n_items = _moe_plan(counts[0, :N_EXPERTS], t * TOP_K, bm)
    dest = (starts[eidx[:, :TOP_K]] + rank[:, :TOP_K]).reshape(t * TOP_K).astype(jnp.int32)
    xs = _dispatch(dest, x1, d, rows)
    ys = _experts(item_e, item_blk, item_lo, item_hi, n_items, xs, w_up, b_up, w_down, b_down, layer, bm)
    return dest, gates, ys


def _pick_tile(n, target, quantum):
    tile = min(n, target)
    while n % tile or tile % quantum:
        tile -= quantum
    return tile


def kernel(x, p, positions, mla_w_in, mla_q_norm, mla_kv_norm, mla_w_uq, mla_w_ukv, mla_w_o, lru_w_in, lru_conv_w, lru_conv_b, lru_w_a, lru_b_a, lru_w_x, lru_b_x, lru_lambda, lru_w_out, ln1_g, ln1_b, ln2_g, ln2_b, moe_w_router, moe_b_router, moe_w_up, moe_b_up, moe_w_down, moe_b_down, ple_w_gate, ple_w_proj):
    batch, seq, d = x.shape
    depth = ln1_g.shape[0]
    t = batch * seq
    assert batch == SUBLANES, "the recurrence kernel maps the batch onto the sublanes of a vreg"
    alpha = (2.0 * depth) ** 0.25
    tm = _pick_tile(t, 512, SUBLANES * batch)
    attn_blk = _pick_tile(seq, 512, LANES)
    moe_bm = _pick_tile(t * TOP_K, 512, SUBLANES)
    combine_ts = _pick_tile(seq, 256, SUBLANES)

    xt = x
    half = QK_ROPE_DIM // 2
    inv_freq = jnp.exp(-math.log(ROPE_THETA) * jnp.arange(half, dtype=F32) / half)
    rope_freq = jnp.concatenate([inv_freq, inv_freq, jnp.zeros((LANES - QK_ROPE_DIM,), F32)]).reshape(1, LANES)
    pos_lanes = jnp.broadcast_to(jnp.transpose(positions).astype(F32)[:, :, None],
                                 (seq, batch, LANES)).reshape(seq, batch * LANES)

    row2 = lambda a: a.reshape(1, -1)
    scale = 1.0 / math.sqrt(QK_NOPE_DIM + QK_ROPE_DIM)

    for layer in range(depth):
        j = layer // 2
        wr = jnp.pad(moe_w_router[layer], ((0, 0), (0, LANES - N_EXPERTS)))
        wr_hi, wr_lo = _split_bf16(wr)
        br = row2(jnp.pad(moe_b_router[layer], (0, LANES - N_EXPERTS)))
        g1, b1 = row2(ln1_g[layer]), row2(ln1_b[layer])
        if layer % 2 == 0:
            q_lora, kv_lora = mla_q_norm.shape[1], mla_kv_norm.shape[1]
            w_in = mla_w_in[j]
            win_p = jnp.pad(w_in, ((0, 0), (0, LANES - QK_ROPE_DIM))).astype(BF16)
            wuq = mla_w_uq[j].reshape(q_lora, MLA_HEADS, QK_NOPE_DIM + QK_ROPE_DIM)
            wuq_p = jnp.pad(wuq, ((0, 0), (0, 0), (0, QK_PAD - QK_NOPE_DIM - QK_ROPE_DIM)))
            wuq_p = wuq_p.reshape(q_lora, MLA_HEADS * QK_PAD).astype(BF16)
            q, k, vt = _mla_proj(xt, pos_lanes, rope_freq, win_p,
                                 row2(mla_q_norm[j] * scale), row2(mla_kv_norm[j]),
                                 wuq_p, mla_w_ukv[j].astype(BF16), seq, batch, attn_blk)
            o = _attention(q, k, vt, seq, batch, attn_blk, 2)
            x1, logits = _post_attn(xt, o, mla_w_o[j].astype(BF16), g1, b1,
                                    wr_hi, wr_lo, br, alpha, seq, batch, attn_blk)
        else:
            x1, logits = _lru_block(
                xt.reshape(t, d), lru_w_in[j].astype(BF16), lru_conv_w[j], row2(lru_conv_b[j]),
                lru_w_a[j].astype(BF16), row2(lru_b_a[j]), lru_w_x[j].astype(BF16), row2(lru_b_x[j]),
                row2(lru_lambda[j]), lru_w_out[j].astype(BF16), g1, b1, wr_hi, wr_lo, br, alpha, batch, tm)
        rows = combine_ts if x1.shape[1] != d else tm
        dest, gates, ys = _moe_dispatch(x1, logits.reshape(t, LANES), moe_w_up, moe_b_up[:, :, None, :],
                                        moe_w_down, moe_b_down[:, :, None, :], layer, tm, rows, moe_bm)
        xt = _post_moe(dest, x1.reshape(seq, batch * d), gates.reshape(seq, batch * LANES), ys, p, layer,
                       row2(ln2_g[layer]), row2(ln2_b[layer]), ple_w_gate[layer].astype(BF16),
                       ple_w_proj[layer].astype(BF16), alpha, seq, batch, combine_ts, layer == depth - 1)
    return xt
```

```python
import functools
import math

import jax
import jax.numpy as jnp
from jax import lax
from jax.experimental import pallas as pl
from jax.experimental.pallas import tpu as pltpu

F32 = jnp.float32
BF16 = jnp.bfloat16

MLA_HEADS = 8
QK_NOPE_DIM = 128
QK_ROPE_DIM = 64
V_HEAD_DIM = 128
ROPE_THETA = 10000.0
LRU_BLOCKS = 4
CONV_WIDTH = 4
LRU_C = 8.0
N_EXPERTS = 32
TOP_K = 4
SWIGLU_LIMIT = 7.0
SWIGLU_ALPHA = 1.702
LN_EPS = 1e-5
RMS_EPS = 1e-6

LANES = 128
SUBLANES = 8
QK_PAD = 2 * LANES
VMEM_LIMIT_BYTES = 56 * 1024 * 1024


def _params(*sem, flags=None):
    return pltpu.CompilerParams(dimension_semantics=sem, vmem_limit_bytes=VMEM_LIMIT_BYTES, flags=flags)


def _dot(a, b):
    return jnp.dot(a, b, preferred_element_type=F32)


def _sigmoid(z):
    return 1.0 / (1.0 + jnp.exp(-z))


def _layer_norm(y, g, b):
    mu = jnp.mean(y, axis=-1, keepdims=True)
    yc = y - mu
    var = jnp.mean(yc * yc, axis=-1, keepdims=True)
    return yc * lax.rsqrt(var + LN_EPS) * g + b


def _rms_norm(y, g):
    ms = jnp.mean(y * y, axis=-1, keepdims=True)
    return y * lax.rsqrt(ms + RMS_EPS) * g


def _split_bf16(a):
    hi = a.astype(BF16)
    lo = (a - hi.astype(F32)).astype(BF16)
    return hi, lo


def _router_logits(x1, wr_hi, wr_lo, br):
    hi, lo = _split_bf16(x1)
    return _dot(hi, wr_hi) + (_dot(hi, wr_lo) + _dot(lo, wr_hi)) + br


def _full(shape):
    return pl.BlockSpec(shape, lambda *_: (0,) * len(shape))


def _rope(blk, cos, sin_a, sin_b):
    return (blk * cos + pltpu.roll(blk, LANES - QK_ROPE_DIM // 2, 1) * sin_a
            + pltpu.roll(blk, QK_ROPE_DIM // 2, 1) * sin_b)


def _mla_proj_kernel(x_ref, pos_ref, freq_ref, win_ref, qn_ref, kvn_ref, wuq_ref, wukv_ref,
                     q_ref, k_ref, vt_ref, *, q_lora, kv_lora):
    lat = _dot(x_ref[...].astype(BF16), win_ref[...])
    c_q = _rms_norm(lat[:, :q_lora], qn_ref[...])
    c_kv = _rms_norm(lat[:, q_lora:q_lora + kv_lora], kvn_ref[...])
    ang = pos_ref[...] * freq_ref[...]
    lane = lax.broadcasted_iota(jnp.int32, ang.shape, 1)
    cos, sin = jnp.cos(ang), jnp.sin(ang)
    half = QK_ROPE_DIM // 2
    sin_a = jnp.where(lane < half, -sin, 0.0)
    sin_b = jnp.where(jnp.logical_and(lane >= half, lane < QK_ROPE_DIM), sin, 0.0)
    k_rope = _rope(lat[:, q_lora + kv_lora:], cos, sin_a, sin_b).astype(BF16)
    q = _dot(c_q.astype(BF16), wuq_ref[...])
    kv = _dot(c_kv.astype(BF16), wukv_ref[...])
    for h in range(MLA_HEADS):
        lo = h * QK_PAD
        mid = lo + LANES
        hi = lo + QK_PAD
        q_ref[:, lo:mid] = q[:, lo:mid].astype(BF16)
        q_ref[:, mid:hi] = _rope(q[:, mid:hi], cos, sin_a, sin_b).astype(BF16)
        k_ref[:, lo:mid] = kv[:, lo:mid].astype(BF16)
        k_ref[:, mid:hi] = k_rope
        vt_ref[h * V_HEAD_DIM:(h + 1) * V_HEAD_DIM, :] = jnp.transpose(kv[:, mid:hi]).astype(BF16)


def _col_spec(ts, c):
    return pl.BlockSpec((ts, c), lambda i, b: (i, b))


def _x_spec(x, ts, d):
    if x.ndim == 3:
        return pl.BlockSpec((None, ts, d), lambda i, b: (b, i, 0))
    return _col_spec(ts, d)


def _mla_proj(x, pos, freq, win_p, qn, kvn, wuq_p, wukv, seq, batch, ts):
    d = win_p.shape[0]
    q_lora, kv_lora = qn.shape[1], kvn.shape[1]
    hq = MLA_HEADS * QK_PAD
    hv = MLA_HEADS * V_HEAD_DIM
    col = functools.partial(_col_spec, ts)
    return pl.pallas_call(
        functools.partial(_mla_proj_kernel, q_lora=q_lora, kv_lora=kv_lora),
        grid=(seq // ts, batch),
        in_specs=[_x_spec(x, ts, d), col(LANES), _full(freq.shape), _full(win_p.shape), _full(qn.shape),
                  _full(kvn.shape), _full(wuq_p.shape), _full(wukv.shape)],
        out_specs=[col(hq), col(hq), pl.BlockSpec((hv, ts), lambda i, b: (b, i))],
        out_shape=[jax.ShapeDtypeStruct((seq, batch * hq), BF16), jax.ShapeDtypeStruct((seq, batch * hq), BF16),
                   jax.ShapeDtypeStruct((batch * hv, seq), BF16)],
        compiler_params=_params("parallel", "parallel"),
        name="mla_proj",
    )(x, pos, freq, win_p, qn, kvn, wuq_p, wukv)


ONES_ROWS = 16


def _attn_kernel(q_ref, k_ref, vt_ref, o_ref, m_sc, acc_sc, st_a, st_b, *, blk, heads):
    qi = pl.program_id(1)
    m_sc[...] = jnp.full(m_sc.shape, -jnp.inf, F32)
    acc_sc[...] = jnp.zeros(acc_sc.shape, F32)
    ones = jnp.ones((ONES_ROWS, blk), BF16)

    def scores(ki, st_ref):
        start = pl.multiple_of(ki * blk, blk)
        for h in range(heads):
            k = k_ref[pl.ds(start, blk), h * QK_PAD:(h + 1) * QK_PAD]
            st_ref[h] = lax.dot_general(k, q_ref[:, h * QK_PAD:(h + 1) * QK_PAD], (((1,), (1,)), ((), ())),
                                        preferred_element_type=F32)

    def consume(ki, st_ref, diagonal):
        start = pl.multiple_of(ki * blk, blk)
        for h in range(heads):
            vt = vt_ref[h * V_HEAD_DIM:(h + 1) * V_HEAD_DIM, pl.ds(start, blk)]
            vt = jnp.concatenate([vt, ones], axis=0)
            st = st_ref[h]
            if diagonal:
                keys = lax.broadcasted_iota(jnp.int32, st.shape, 0)
                queries = lax.broadcasted_iota(jnp.int32, st.shape, 1)
                st = jnp.where(keys <= queries, st, -jnp.inf)
            m_prev = m_sc[h]
            m_new = jnp.maximum(m_prev, jnp.max(st, axis=0, keepdims=True))
            alpha = jnp.exp(m_prev - m_new)
            p = jnp.exp(st - m_new).astype(BF16)
            acc_sc[h] = alpha * acc_sc[h] + _dot(vt, p)
            m_sc[h] = m_new

    pairs = qi // 2
    scores(0, st_a)

    def body(j, carry):
        scores(2 * j + 1, st_b)
        consume(2 * j, st_a, False)
        scores(2 * j + 2, st_a)
        consume(2 * j + 1, st_b, False)
        return carry

    lax.fori_loop(0, pairs, body, 0)

    @pl.when(qi == 2 * pairs)
    def _():
        consume(qi, st_a, True)

    @pl.when(qi != 2 * pairs)
    def _():
        scores(qi, st_b)
        consume(qi - 1, st_a, False)
        consume(qi, st_b, True)

    for h in range(heads):
        acc = acc_sc[h]
        out_t = acc[:V_HEAD_DIM, :] / acc[V_HEAD_DIM:V_HEAD_DIM + 1, :]
        o_ref[:, h * V_HEAD_DIM:(h + 1) * V_HEAD_DIM] = jnp.transpose(out_t).astype(o_ref.dtype)


def _attention(q, k, vt, seq, batch, blk, heads):
    groups = batch * MLA_HEADS // heads
    return pl.pallas_call(
        functools.partial(_attn_kernel, blk=blk, heads=heads),
        grid=(groups, seq // blk),
        in_specs=[pl.BlockSpec((blk, heads * QK_PAD), lambda g, i: (i, g)),
                  pl.BlockSpec((seq, heads * QK_PAD), lambda g, i: (0, g)),
                  pl.BlockSpec((heads * V_HEAD_DIM, seq), lambda g, i: (g, 0))],
        out_specs=pl.BlockSpec((blk, heads * V_HEAD_DIM), lambda g, i: (i, g)),
        out_shape=jax.ShapeDtypeStruct((seq, batch * MLA_HEADS * V_HEAD_DIM), BF16),
        scratch_shapes=[pltpu.VMEM((heads, 1, blk), F32),
                        pltpu.VMEM((heads, V_HEAD_DIM + ONES_ROWS, blk), F32),
                        pltpu.VMEM((heads, blk, blk), F32), pltpu.VMEM((heads, blk, blk), F32)],
        compiler_params=_params("parallel", "arbitrary"),
        name="mla_attention",
    )(q, k, vt)


def _mix_epilogue(x, mix, g_ref, b_ref, wrh_ref, wrl_ref, br_ref, x1_ref, lg_ref, alpha):
    x1 = _layer_norm(alpha * x + mix, g_ref[...], b_ref[...])
    x1_ref[...] = x1
    lg_ref[...] = _router_logits(x1, wrh_ref[...], wrl_ref[...], br_ref[...])


def _post_attn_kernel(x_ref, o_ref, wo_ref, g_ref, b_ref, wrh_ref, wrl_ref, br_ref, x1_ref, lg_ref, *, alpha):
    mix = _dot(o_ref[...], wo_ref[...])
    _mix_epilogue(x_ref[...], mix, g_ref, b_ref, wrh_ref, wrl_ref, br_ref, x1_ref, lg_ref, alpha)


def _post_attn(x, o, wo, g, b, wr_hi, wr_lo, br, alpha, seq, batch, ts):
    hv, d = wo.shape
    col = functools.partial(_col_spec, ts)
    return pl.pallas_call(
        functools.partial(_post_attn_kernel, alpha=alpha),
        grid=(seq // ts, batch),
        in_specs=[_x_spec(x, ts, d), col(hv), _full(wo.shape), _full(g.shape), _full(b.shape),
                  _full(wr_hi.shape), _full(wr_lo.shape), _full(br.shape)],
        out_specs=[col(d), col(LANES)],
        out_shape=[jax.ShapeDtypeStruct((seq, batch * d), F32), jax.ShapeDtypeStruct((seq, batch * LANES), F32)],
        compiler_params=_params("parallel", "parallel"),
        name="post_attention",
    )(x, o, wo, g, b, wr_hi, wr_lo, br)


def _lru_kernel(x_ref, win_ref, cw_ref, cb_ref, wa_ref, ba_ref, wx_ref, bx_ref, lam_ref, wout_ref,
                g_ref, b_ref, wrh_ref, wrl_ref, br_ref, x1_ref, lg_ref,
                ucarry_sc, hcarry_sc, a_sc, b_sc, h_sc, *, alpha, batch):
    tm, width = a_sc.shape
    halo = (CONV_WIDTH - 1) * batch
    blk_w = width // LRU_BLOCKS

    @pl.when(pl.program_id(0) == 0)
    def _():
        ucarry_sc[...] = jnp.zeros(ucarry_sc.shape, F32)
        hcarry_sc[...] = jnp.zeros(hcarry_sc.shape, F32)

    x = x_ref[...]
    gu = _dot(x.astype(BF16), win_ref[...])
    gate = gu[:, :width]
    u = gu[:, width:]
    u_ext = jnp.concatenate([ucarry_sc[...], u], axis=0)
    ucarry_sc[...] = u[tm - halo:, :]
    cw = cw_ref[...]
    uc = cb_ref[...] + cw[0:1, :] * u_ext[0:tm, :]
    for j in range(1, CONV_WIDTH):
        uc = uc + cw[j:j + 1, :] * u_ext[j * batch:j * batch + tm, :]
    ucb = uc.astype(BF16)
    ra = jnp.concatenate([_dot(ucb[:, n * blk_w:(n + 1) * blk_w], wa_ref[n]) for n in range(LRU_BLOCKS)], axis=1)
    rx = jnp.concatenate([_dot(ucb[:, n * blk_w:(n + 1) * blk_w], wx_ref[n]) for n in range(LRU_BLOCKS)], axis=1)
    r = _sigmoid(ra + ba_ref[...])
    gi = _sigmoid(rx + bx_ref[...])
    z = -lam_ref[...]
    softplus = jnp.maximum(z, 0.0) + jnp.log1p(jnp.exp(-jnp.abs(z)))
    log_a = (-LRU_C) * r * softplus
    a_sc[...] = jnp.exp(log_a)
    th = jnp.tanh(log_a)
    b_sc[...] = jnp.sqrt(-2.0 * th / (1.0 - th)) * gi * uc

    def body(t, h):
        rows = pl.ds(pl.multiple_of(t * batch, batch), batch)
        h = a_sc[rows, :] * h + b_sc[rows, :]
        h_sc[rows, :] = h
        return h

    hcarry_sc[...] = lax.fori_loop(0, tm // batch, body, hcarry_sc[...], unroll=8)
    c = math.sqrt(2.0 / math.pi)
    gelu = 0.5 * gate * (1.0 + jnp.tanh(c * (gate + 0.044715 * (gate * gate * gate))))
    y = gelu * h_sc[...]
    mix = _dot(y.astype(BF16), wout_ref[...])
    _mix_epilogue(x, mix, g_ref, b_ref, wrh_ref, wrl_ref, br_ref, x1_ref, lg_ref, alpha)


def _lru_block(x, win, cw, cb, wa, ba, wx, bx, lam, wout, g, b, wr_hi, wr_lo, br, alpha, batch, tm):
    t, d = x.shape
    width = wout.shape[0]
    row = lambda c: pl.BlockSpec((tm, c), lambda i: (i, 0))
    consts = (win, cw, cb, wa, ba, wx, bx, lam, wout, g, b, wr_hi, wr_lo, br)
    return pl.pallas_call(
        functools.partial(_lru_kernel, alpha=alpha, batch=batch),
        grid=(t // tm,),
        in_specs=[row(d)] + [_full(c.shape) for c in consts],
        out_specs=[row(d), row(LANES)],
        out_shape=[jax.ShapeDtypeStruct((t, d), F32), jax.ShapeDtypeStruct((t, LANES), F32)],
        scratch_shapes=[pltpu.VMEM(((CONV_WIDTH - 1) * batch, width), F32), pltpu.VMEM((batch, width), F32),
                        pltpu.VMEM((tm, width), F32), pltpu.VMEM((tm, width), F32), pltpu.VMEM((tm, width), F32)],
        compiler_params=_params("arbitrary"),
        name="rglru_block",
    )(x, *consts)


SEG_ROWS = SUBLANES


def _route_kernel(lg_ref, gate_ref, spos_ref, spt_ref, seg_ref, cnt_ref, run_sc):
    tm = lg_ref.shape[0]

    @pl.when(pl.program_id(0) == 0)
    def _():
        run_sc[...] = jnp.zeros(run_sc.shape, F32)

    lane = lax.broadcasted_iota(jnp.int32, (tm, LANES), 1)
    lane_f = lane.astype(F32)
    work = jnp.where(lane < N_EXPERTS, lg_ref[...], -jnp.inf)
    tops, onehots = [], []
    for k in range(TOP_K):
        top = jnp.max(work, axis=1, keepdims=True)
        idx = jnp.min(jnp.where(work == top, lane_f, float(LANES)), axis=1, keepdims=True)
        hot = lane_f == idx
        work = jnp.where(hot, -jnp.inf, work)
        tops.append(top)
        onehots.append(hot)
    exps = [jnp.exp(top - tops[0]) for top in tops]
    denom = exps[0]
    for e in exps[1:]:
        denom = denom + e
    gate = jnp.zeros((tm, LANES), F32)
    for k in range(TOP_K):
        gate = jnp.where(lane == k, exps[k] / denom, gate)
    hits = onehots[0].astype(F32)
    for hot in onehots[1:]:
        hits = hits + hot.astype(F32)
    earlier = (lax.broadcasted_iota(jnp.int32, (tm, tm), 1) < lax.broadcasted_iota(jnp.int32, (tm, tm), 0))
    before = _dot(earlier.astype(BF16), hits.astype(BF16))
    count = jnp.sum(hits, axis=0, keepdims=True)
    chunks = jnp.floor((count + (SEG_ROWS - 1.0)) * (1.0 / SEG_ROWS))
    lower = (lax.broadcasted_iota(jnp.int32, (LANES, LANES), 0) < lax.broadcasted_iota(jnp.int32, (LANES, LANES), 1))
    chunks8 = jnp.broadcast_to(chunks, (SUBLANES, LANES))
    slab_off = _dot(chunks8.astype(BF16), lower.astype(BF16))[0:1, :] * SEG_ROWS
    run = run_sc[...]
    spos = jnp.zeros((tm, LANES), F32)
    for k in range(TOP_K):
        s_k = jnp.sum(jnp.where(onehots[k], before + slab_off, 0.0), axis=1, keepdims=True)
        spos = jnp.where(lane == k, s_k, spos)
    run_sc[...] = run + chunks * SEG_ROWS
    gate_ref[...] = gate
    spos_ref[...] = spos.astype(jnp.int32)
    spt_ref[0] = jnp.transpose(spos)[0:SUBLANES, :].astype(jnp.int32)
    row = lax.broadcasted_iota(jnp.int32, (SUBLANES, LANES), 0)
    seg = jnp.where(row == 0, chunks, jnp.where(row == 1, slab_off, jnp.where(row == 2, run, 0.0)))
    seg_ref[0] = seg.astype(jnp.int32)
    cnt_ref[...] = run_sc[...].astype(jnp.int32)


def _route(logits, tm):
    t = logits.shape[0]
    n_tiles = t // tm
    row = pl.BlockSpec((tm, LANES), lambda i: (i, 0))
    return pl.pallas_call(
        _route_kernel,
        grid=(n_tiles,),
        in_specs=[row],
        out_specs=[row, row, pl.BlockSpec((1, SUBLANES, tm), lambda i: (i, 0, 0)),
                   pl.BlockSpec((1, SUBLANES, LANES), lambda i: (i, 0, 0)),
                   pl.BlockSpec((1, LANES), lambda i: (0, 0))],
        out_shape=[jax.ShapeDtypeStruct((t, LANES), F32), jax.ShapeDtypeStruct((t, LANES), jnp.int32),
                   jax.ShapeDtypeStruct((n_tiles, SUBLANES, tm), jnp.int32),
                   jax.ShapeDtypeStruct((n_tiles, SUBLANES, LANES), jnp.int32),
                   jax.ShapeDtypeStruct((1, LANES), jnp.int32)],
        scratch_shapes=[pltpu.VMEM((1, LANES), F32)],
        compiler_params=_params("arbitrary"),
        name="moe_route",
    )(logits)


def _chunk_copy(src, src_row, dst, dst_row, sem):
    return pltpu.make_async_copy(src.at[pl.ds(pl.multiple_of(src_row, SEG_ROWS), SEG_ROWS)],
                                 dst.at[pl.ds(pl.multiple_of(dst_row, SEG_ROWS), SEG_ROWS)], sem)


def _for_each_segment_chunk(tile, nch_ref, loc_ref, glob_ref, fn):
    def per_expert(e, carry):
        s = tile * N_EXPERTS + e
        lo, go = loc_ref[s], glob_ref[s]

        def per_chunk(c, carry2):
            fn(lo + c * SEG_ROWS, go + c * SEG_ROWS)
            return carry2

        lax.fori_loop(0, nch_ref[s], per_chunk, 0)
        return carry

    lax.fori_loop(0, N_EXPERTS, per_expert, 0)


SLAB_CHUNK = 256


def _slab_rows(tm):
    worst = tm * TOP_K + N_EXPERTS * (SEG_ROWS - 1)
    return -(-worst // SLAB_CHUNK) * SLAB_CHUNK


def _dispatch_kernel(nch_ref, loc_ref, glob_ref, tail_ref, spt_ref, x_ref, xs_hbm, slab_sc, zero_sc, sem):
    i = pl.program_id(0)
    tm = x_ref.shape[0]
    slab_rows = slab_sc.shape[0]
    xb = x_ref[...].astype(BF16)
    sp = spt_ref[0]
    for c0 in range(0, slab_rows, SLAB_CHUNK):
        j = lax.broadcasted_iota(jnp.int32, (SLAB_CHUNK, tm), 0) + c0
        hit = j == sp[0:1, :]
        for k in range(1, TOP_K):
            hit = jnp.logical_or(hit, j == sp[k:k + 1, :])
        slab_sc[c0:c0 + SLAB_CHUNK, :] = _dot(hit.astype(BF16), xb)

    def for_each_chunk(fn):
        _for_each_segment_chunk(i, nch_ref, loc_ref, glob_ref,
                                lambda lo, go: fn(_chunk_copy(slab_sc, lo, xs_hbm, go, sem)))

    for_each_chunk(lambda cp: cp.start())

    last = i == pl.num_programs(0) - 1

    def for_each_tail_chunk(fn):
        def per_chunk(c, carry):
            fn(_chunk_copy(zero_sc, 0, xs_hbm, tail_ref[0] + c * SEG_ROWS, sem))
            return carry

        lax.fori_loop(0, tail_ref[1], per_chunk, 0)

    @pl.when(last)
    def _():
        zero_sc[...] = jnp.zeros(zero_sc.shape, zero_sc.dtype)
        for_each_tail_chunk(lambda cp: cp.start())

    for_each_chunk(lambda cp: cp.wait())

    @pl.when(last)
    def _():
        for_each_tail_chunk(lambda cp: cp.wait())


def _dispatch(n_chunks, slab_off, dest_off, tail, spt, x1, n_pad):
    t, d = x1.shape
    n_tiles, _, tm = spt.shape
    slab_rows = _slab_rows(tm)
    grid_spec = pltpu.PrefetchScalarGridSpec(
        num_scalar_prefetch=4,
        grid=(n_tiles,),
        in_specs=[pl.BlockSpec((1, SUBLANES, tm), lambda i, *_: (i, 0, 0)),
                  pl.BlockSpec((tm, d), lambda i, *_: (i, 0))],
        out_specs=pl.BlockSpec(memory_space=pl.ANY),
        scratch_shapes=[pltpu.VMEM((slab_rows, d), F32), pltpu.VMEM((SEG_ROWS, d), F32),
                        pltpu.SemaphoreType.DMA(())],
    )
    return pl.pallas_call(
        _dispatch_kernel,
        grid_spec=grid_spec,
        out_shape=jax.ShapeDtypeStruct((n_pad, d), F32),
        compiler_params=_params("arbitrary"),
        name="moe_dispatch",
    )(n_chunks, slab_off, dest_off, tail, spt, x1)


def _expert_kernel(ie_ref, ix_ref, iy_ref, lo_ref, hi_ref, xs_ref, wup_ref, bup_ref, wdn_ref, bdn_ref, ys_ref,
                   wup_sc, wdn_sc):
    w = pl.program_id(0)
    bm = xs_ref.shape[0]
    d_ff = wdn_ref.shape[1]
    valid = hi_ref[w] > lo_ref[w]
    new_expert = jnp.logical_or(w == 0, ie_ref[w] != ie_ref[jnp.maximum(w - 1, 0)])

    @pl.when(jnp.logical_and(valid, new_expert))
    def _():
        wup_sc[...] = wup_ref[0].astype(BF16)
        wdn_sc[...] = wdn_ref[0].astype(BF16)

    @pl.when(jnp.logical_and(lo_ref[w] == 0, hi_ref[w] == 0))
    def _():
        ys_ref[...] = jnp.zeros(ys_ref.shape, ys_ref.dtype)

    @pl.when(valid)
    def _():
        hb = _dot(xs_ref[...].astype(BF16), wup_sc[...]) + bup_ref[0]
        gl = jnp.minimum(hb[:, :d_ff], SWIGLU_LIMIT)
        up = jnp.clip(hb[:, d_ff:], -SWIGLU_LIMIT, SWIGLU_LIMIT)
        yb = (up + 1.0) * (gl * _sigmoid(SWIGLU_ALPHA * gl))
        y = _dot(yb.astype(BF16), wdn_sc[...]) + bdn_ref[0]
        rows = lax.broadcasted_iota(jnp.int32, (bm, 1), 0)
        mine = jnp.logical_and(rows >= lo_ref[w], rows < hi_ref[w])
        first_visit = lo_ref[w] == 0

        @pl.when(first_visit)
        def _():
            ys_ref[...] = jnp.where(mine, y, 0.0)

        @pl.when(jnp.logical_not(first_visit))
        def _():
            ys_ref[...] = jnp.where(mine, y, ys_ref[...])


def _experts(item_e, item_x, item_y, item_lo, item_hi, xs, w_up, b_up, w_down, b_down, layer, bm):
    n, d = xs.shape
    f2 = w_up.shape[3]
    n_max = item_e.shape[0]
    xblk = lambda w, ie, ix, iy, lo, hi: (ix[w], 0)
    yblk = lambda w, ie, ix, iy, lo, hi: (iy[w], 0)
    exp3 = lambda w, ie, ix, iy, lo, hi: (layer, ie[w], 0, 0)
    grid_spec = pltpu.PrefetchScalarGridSpec(
        num_scalar_prefetch=5,
        grid=(n_max,),
        in_specs=[pl.BlockSpec((bm, d), xblk),
                  pl.BlockSpec((None, 1, d, f2), exp3), pl.BlockSpec((None, 1, 1, f2), exp3),
                  pl.BlockSpec((None, 1, f2 // 2, d), exp3), pl.BlockSpec((None, 1, 1, d), exp3)],
        out_specs=pl.BlockSpec((bm, d), yblk),
        scratch_shapes=[pltpu.VMEM((d, f2), BF16), pltpu.VMEM((f2 // 2, d), BF16)],
    )
    return pl.pallas_call(
        _expert_kernel,
        grid_spec=grid_spec,
        out_shape=jax.ShapeDtypeStruct((n, d), F32),
        compiler_params=_params("arbitrary"),
        name="moe_experts",
    )(item_e, item_x, item_y, item_lo, item_hi, xs, w_up, b_up, w_down, b_down)


def _post_moe_kernel(nch_ref, loc_ref, glob_ref, x1_ref, gate_ref, spos_ref, p_ref, g_ref, b_ref, wg_ref, wp_ref,
                     ys_hbm, out_ref, slab_sc, sem, *, alpha):
    i = pl.program_id(0)
    tm = x1_ref.shape[0]
    slab_rows = slab_sc.shape[0]

    @pl.when(i == 0)
    def _():
        slab_sc[...] = jnp.zeros(slab_sc.shape, slab_sc.dtype)

    def for_each_chunk(fn):
        _for_each_segment_chunk(i, nch_ref, loc_ref, glob_ref,
                                lambda lo, go: fn(_chunk_copy(ys_hbm, go, slab_sc, lo, sem)))

    for_each_chunk(lambda cp: cp.start())
    for_each_chunk(lambda cp: cp.wait())
    gates = gate_ref[...]
    spos = spos_ref[...]
    ffn = jnp.zeros(x1_ref.shape, F32)
    for c0 in range(0, slab_rows, SLAB_CHUNK):
        col = lax.broadcasted_iota(jnp.int32, (tm, SLAB_CHUNK), 1) + c0
        weight = jnp.where(col == spos[:, 0:1], gates[:, 0:1], 0.0)
        for k in range(1, TOP_K):
            weight = weight + jnp.where(col == spos[:, k:k + 1], gates[:, k:k + 1], 0.0)
        ffn = ffn + _dot(weight.astype(BF16), slab_sc[c0:c0 + SLAB_CHUNK, :].astype(BF16))
    x2 = _layer_norm(alpha * x1_ref[...] + ffn, g_ref[...], b_ref[...])
    gate = _sigmoid(_dot(x2.astype(BF16), wg_ref[...]))
    batch, ts = p_ref.shape[0], p_ref.shape[1]
    p_bm = p_ref[...].reshape(tm, p_ref.shape[2]).astype(BF16)
    tok = lax.broadcasted_iota(jnp.int32, (tm, tm), 0)
    src = lax.broadcasted_iota(jnp.int32, (tm, tm), 1)
    shift = batch.bit_length() - 1
    pick = src == jnp.bitwise_and(tok, batch - 1) * ts + lax.shift_right_logical(tok, shift)
    p_tok = _dot(pick.astype(BF16), p_bm).astype(BF16)
    out_ref[...] = x2 + gate * _dot(p_tok, wp_ref[...])


def _post_moe(n_chunks, slab_off, dest_off, x1, gates, spos, ys, p, layer, g, b, wg, wp, alpha, batch, tm):
    t, d = x1.shape
    ts = tm // batch
    row = lambda c: pl.BlockSpec((tm, c), lambda i, *_: (i, 0))
    grid_spec = pltpu.PrefetchScalarGridSpec(
        num_scalar_prefetch=3,
        grid=(t // tm,),
        in_specs=[row(d), row(LANES), row(LANES),
                  pl.BlockSpec((None, batch, ts, p.shape[-1]), lambda i, *_: (layer, 0, i, 0)),
                  _full(g.shape), _full(b.shape), _full(wg.shape), _full(wp.shape),
                  pl.BlockSpec(memory_space=pl.ANY)],
        out_specs=row(d),
        scratch_shapes=[pltpu.VMEM((_slab_rows(tm), d), F32), pltpu.SemaphoreType.DMA(())],
    )
    return pl.pallas_call(
        functools.partial(_post_moe_kernel, alpha=alpha),
        grid_spec=grid_spec,
        out_shape=jax.ShapeDtypeStruct((t, d), F32),
        compiler_params=_params("arbitrary"),
        name="post_moe_ple",
    )(n_chunks, slab_off, dest_off, x1, gates, spos, p, g, b, wg, wp, ys)


def _moe_plan(counts, n_pad, bm):
    i32 = jnp.int32
    n_blocks = n_pad // bm
    n_max = n_blocks + N_EXPERTS - 1
    ends = jnp.cumsum(counts)
    starts = ends - counts
    total = ends[-1]
    first_blk = starts // bm
    n_items_e = jnp.where(counts > 0, (ends - 1) // bm - first_blk + 1, 0)
    item_end = jnp.cumsum(n_items_e)
    item_start = item_end - n_items_e
    n_compute = item_end[-1]
    n_used = (total + bm - 1) // bm
    n_active = n_compute + (n_blocks - n_used)
    w_all = jnp.arange(n_max, dtype=i32)
    w = jnp.minimum(w_all, n_active - 1)
    is_compute = w < n_compute
    wc = jnp.minimum(w, n_compute - 1)
    item_e = jnp.sum((wc[:, None] >= item_end[None, :]).astype(i32), axis=1)
    blk_c = first_blk[item_e] + wc - item_start[item_e]
    item_y = jnp.where(is_compute, blk_c, n_used + (w - n_compute)).astype(i32)
    item_x = jnp.where(is_compute, blk_c, 0).astype(i32)
    lo_c = jnp.maximum(starts[item_e] - blk_c * bm, 0)
    hi_c = jnp.minimum(ends[item_e] - blk_c * bm, bm)
    item_lo = jnp.where(w_all >= n_active, -1, jnp.where(is_compute, lo_c, 0)).astype(i32)
    item_hi = jnp.where(w_all >= n_active, -1, jnp.where(is_compute, hi_c, 0)).astype(i32)
    tail = jnp.stack([total, (n_pad - total) // SEG_ROWS]).astype(i32)
    return starts, tail, item_e.astype(i32), item_x, item_y, item_lo, item_hi


def _moe_dispatch(x1, logits, w_up, b_up, w_down, b_down, layer, tm, bm):
    t, d = x1.shape
    n_tiles = t // tm
    worst = t * TOP_K + n_tiles * N_EXPERTS * (SEG_ROWS - 1)
    n_pad = -(-worst // bm) * bm
    gates, spos, spt, seg, counts = _route(logits, tm)
    starts, tail, item_e, item_x, item_y, item_lo, item_hi = _moe_plan(counts[0, :N_EXPERTS], n_pad, bm)
    n_chunks = seg[:, 0, :N_EXPERTS].reshape(-1)
    slab_off = seg[:, 1, :N_EXPERTS].reshape(-1)
    dest_off = (starts[None, :] + seg[:, 2, :N_EXPERTS]).reshape(-1).astype(jnp.int32)
    xs = _dispatch(n_chunks, slab_off, dest_off, tail, spt, x1, n_pad)
    ys = _experts(item_e, item_x, item_y, item_lo, item_hi, xs, w_up, b_up, w_down, b_down, layer, bm)
    return (n_chunks, slab_off, dest_off), gates, spos, ys


def _pick_tile(n, target, quantum):
    tile = min(n, target)
    while n % tile or tile % quantum:
        tile -= quantum
    return tile


def kernel(x, p, positions, mla_w_in, mla_q_norm, mla_kv_norm, mla_w_uq, mla_w_ukv, mla_w_o, lru_w_in, lru_conv_w, lru_conv_b, lru_w_a, lru_b_a, lru_w_x, lru_b_x, lru_lambda, lru_w_out, ln1_g, ln1_b, ln2_g, ln2_b, moe_w_router, moe_b_router, moe_w_up, moe_b_up, moe_w_down, moe_b_down, ple_w_gate, ple_w_proj):
    batch, seq, d = x.shape
    depth = ln1_g.shape[0]
    t = batch * seq
    assert batch == SUBLANES, "the recurrence kernel maps the batch onto the sublanes of a vreg"
    alpha = (2.0 * depth) ** 0.25
    tm = _pick_tile(t, 512, SUBLANES * batch)
    attn_blk = _pick_tile(seq, 512, LANES)
    moe_bm = _pick_tile(t * TOP_K, 512, SUBLANES)

    xt = x
    half = QK_ROPE_DIM // 2
    inv_freq = jnp.exp(-math.log(ROPE_THETA) * jnp.arange(half, dtype=F32) / half)
    rope_freq = jnp.concatenate([inv_freq, inv_freq, jnp.zeros((LANES - QK_ROPE_DIM,), F32)]).reshape(1, LANES)
    pos_lanes = jnp.broadcast_to(jnp.transpose(positions).astype(F32)[:, :, None],
                                 (seq, batch, LANES)).reshape(seq, batch * LANES)

    row2 = lambda a: a.reshape(1, -1)
    scale = 1.0 / math.sqrt(QK_NOPE_DIM + QK_ROPE_DIM)

    for layer in range(depth):
        j = layer // 2
        wr = jnp.pad(moe_w_router[layer], ((0, 0), (0, LANES - N_EXPERTS)))
        wr_hi, wr_lo = _split_bf16(wr)
        br = row2(jnp.pad(moe_b_router[layer], (0, LANES - N_EXPERTS)))
        g1, b1 = row2(ln1_g[layer]), row2(ln1_b[layer])
        if layer % 2 == 0:
            q_lora, kv_lora = mla_q_norm.shape[1], mla_kv_norm.shape[1]
            w_in = mla_w_in[j]
            win_p = jnp.pad(w_in, ((0, 0), (0, LANES - QK_ROPE_DIM))).astype(BF16)
            wuq = mla_w_uq[j].reshape(q_lora, MLA_HEADS, QK_NOPE_DIM + QK_ROPE_DIM)
            wuq_p = jnp.pad(wuq, ((0, 0), (0, 0), (0, QK_PAD - QK_NOPE_DIM - QK_ROPE_DIM)))
            wuq_p = wuq_p.reshape(q_lora, MLA_HEADS * QK_PAD).astype(BF16)
            q, k, vt = _mla_proj(xt, pos_lanes, rope_freq, win_p,
                                 row2(mla_q_norm[j] * scale), row2(mla_kv_norm[j]),
                                 wuq_p, mla_w_ukv[j].astype(BF16), seq, batch, attn_blk)
            o = _attention(q, k, vt, seq, batch, attn_blk, 2)
            x1, logits = _post_attn(xt, o, mla_w_o[j].astype(BF16), g1, b1,
                                    wr_hi, wr_lo, br, alpha, seq, batch, attn_blk)
        else:
            x1, logits = _lru_block(
                xt.reshape(t, d), lru_w_in[j].astype(BF16), lru_conv_w[j], row2(lru_conv_b[j]),
                lru_w_a[j].astype(BF16), row2(lru_b_a[j]), lru_w_x[j].astype(BF16), row2(lru_b_x[j]),
                row2(lru_lambda[j]), lru_w_out[j].astype(BF16), g1, b1, wr_hi, wr_lo, br, alpha, batch, tm)
        x1 = x1.reshape(t, d)
        tables, gates, spos, ys = _moe_dispatch(x1, logits.reshape(t, LANES), moe_w_up,
                                                moe_b_up[:, :, None, :], moe_w_down, moe_b_down[:, :, None, :],
                                                layer, tm, moe_bm)
        xt = _post_moe(*tables, x1, gates, spos, ys, p, layer, row2(ln2_g[layer]), row2(ln2_b[layer]),
                       ple_w_gate[layer].astype(BF16), ple_w_proj[layer].astype(BF16), alpha, batch, tm)
        if layer + 1 < depth and (layer + 1) % 2 == 0:
            xt = xt.reshape(seq, batch * d)
    return jnp.transpose(xt.reshape(seq, batch, d), (1, 0, 2))
```

```python
import functools
import math

import jax
import jax.numpy as jnp
from jax import lax
from jax.experimental import pallas as pl
from jax.experimental.pallas import tpu as pltpu

F32 = jnp.float32
BF16 = jnp.bfloat16

MLA_HEADS = 8
QK_NOPE_DIM = 128
QK_ROPE_DIM = 64
V_HEAD_DIM = 128
ROPE_THETA = 10000.0
LRU_BLOCKS = 4
CONV_WIDTH = 4
LRU_C = 8.0
N_EXPERTS = 32
TOP_K = 4
SWIGLU_LIMIT = 7.0
SWIGLU_ALPHA = 1.702
LN_EPS = 1e-5
RMS_EPS = 1e-6

LANES = 128
SUBLANES = 8
QK_PAD = 2 * LANES
VMEM_LIMIT_BYTES = 56 * 1024 * 1024


def _params(*sem, flags=None):
    return pltpu.CompilerParams(dimension_semantics=sem, vmem_limit_bytes=VMEM_LIMIT_BYTES, flags=flags)


def _dot(a, b):
    return jnp.dot(a, b, preferred_element_type=F32)


def _sigmoid(z):
    return 1.0 / (1.0 + jnp.exp(-z))


def _layer_norm(y, g, b):
    mu = jnp.mean(y, axis=-1, keepdims=True)
    yc = y - mu
    var = jnp.mean(yc * yc, axis=-1, keepdims=True)
    return yc * lax.rsqrt(var + LN_EPS) * g + b


def _rms_norm(y, g):
    ms = jnp.mean(y * y, axis=-1, keepdims=True)
    return y * lax.rsqrt(ms + RMS_EPS) * g


def _split_bf16(a):
    hi = a.astype(BF16)
    lo = (a - hi.astype(F32)).astype(BF16)
    return hi, lo


def _router_logits(x1, wr_hi, wr_lo, br):
    hi, lo = _split_bf16(x1)
    return _dot(hi, wr_hi) + (_dot(hi, wr_lo) + _dot(lo, wr_hi)) + br


def _full(shape):
    return pl.BlockSpec(shape, lambda *_: (0,) * len(shape))


def _rope(blk, cos, sin_a, sin_b):
    return (blk * cos + pltpu.roll(blk, LANES - QK_ROPE_DIM // 2, 1) * sin_a
            + pltpu.roll(blk, QK_ROPE_DIM // 2, 1) * sin_b)


def _mla_proj_kernel(x_ref, pos_ref, freq_ref, win_ref, qn_ref, kvn_ref, wuq_ref, wukv_ref,
                     q_ref, k_ref, vt_ref, *, q_lora, kv_lora):
    lat = _dot(x_ref[...].astype(BF16), win_ref[...])
    c_q = _rms_norm(lat[:, :q_lora], qn_ref[...])
    c_kv = _rms_norm(lat[:, q_lora:q_lora + kv_lora], kvn_ref[...])
    ang = pos_ref[...] * freq_ref[...]
    lane = lax.broadcasted_iota(jnp.int32, ang.shape, 1)
    cos, sin = jnp.cos(ang), jnp.sin(ang)
    half = QK_ROPE_DIM // 2
    sin_a = jnp.where(lane < half, -sin, 0.0)
    sin_b = jnp.where(jnp.logical_and(lane >= half, lane < QK_ROPE_DIM), sin, 0.0)
    k_rope = _rope(lat[:, q_lora + kv_lora:], cos, sin_a, sin_b).astype(BF16)
    q = _dot(c_q.astype(BF16), wuq_ref[...])
    kv = _dot(c_kv.astype(BF16), wukv_ref[...])
    for h in range(MLA_HEADS):
        lo = h * QK_PAD
        mid = lo + LANES
        hi = lo + QK_PAD
        q_ref[:, lo:mid] = q[:, lo:mid].astype(BF16)
        q_ref[:, mid:hi] = _rope(q[:, mid:hi], cos, sin_a, sin_b).astype(BF16)
        k_ref[:, lo:mid] = kv[:, lo:mid].astype(BF16)
        k_ref[:, mid:hi] = k_rope
        vt_ref[h * V_HEAD_DIM:(h + 1) * V_HEAD_DIM, :] = jnp.transpose(kv[:, mid:hi]).astype(BF16)


def _col_spec(ts, c):
    return pl.BlockSpec((ts, c), lambda i, b: (i, b))


def _x_spec(x, ts, d):
    if x.ndim == 3:
        return pl.BlockSpec((None, ts, d), lambda i, b: (b, i, 0))
    return _col_spec(ts, d)


def _mla_proj(x, pos, freq, win_p, qn, kvn, wuq_p, wukv, seq, batch, ts):
    d = win_p.shape[0]
    q_lora, kv_lora = qn.shape[1], kvn.shape[1]
    hq = MLA_HEADS * QK_PAD
    hv = MLA_HEADS * V_HEAD_DIM
    col = functools.partial(_col_spec, ts)
    return pl.pallas_call(
        functools.partial(_mla_proj_kernel, q_lora=q_lora, kv_lora=kv_lora),
        grid=(seq // ts, batch),
        in_specs=[_x_spec(x, ts, d), col(LANES), _full(freq.shape), _full(win_p.shape), _full(qn.shape),
                  _full(kvn.shape), _full(wuq_p.shape), _full(wukv.shape)],
        out_specs=[col(hq), col(hq), pl.BlockSpec((hv, ts), lambda i, b: (b, i))],
        out_shape=[jax.ShapeDtypeStruct((seq, batch * hq), BF16), jax.ShapeDtypeStruct((seq, batch * hq), BF16),
                   jax.ShapeDtypeStruct((batch * hv, seq), BF16)],
        compiler_params=_params("parallel", "parallel"),
        name="mla_proj",
    )(x, pos, freq, win_p, qn, kvn, wuq_p, wukv)


ONES_ROWS = 16


def _attn_kernel(q_ref, k_ref, vt_ref, o_ref, m_sc, acc_sc, st_a, st_b, *, blk, heads):
    qi = pl.program_id(1)
    m_sc[...] = jnp.full(m_sc.shape, -jnp.inf, F32)
    acc_sc[...] = jnp.zeros(acc_sc.shape, F32)
    ones = jnp.ones((ONES_ROWS, blk), BF16)

    def scores(ki, st_ref):
        start = pl.multiple_of(ki * blk, blk)
        for h in range(heads):
            k = k_ref[pl.ds(start, blk), h * QK_PAD:(h + 1) * QK_PAD]
            st_ref[h] = lax.dot_general(k, q_ref[:, h * QK_PAD:(h + 1) * QK_PAD], (((1,), (1,)), ((), ())),
                                        preferred_element_type=F32)

    def consume(ki, st_ref, diagonal):
        start = pl.multiple_of(ki * blk, blk)
        for h in range(heads):
            vt = vt_ref[h * V_HEAD_DIM:(h + 1) * V_HEAD_DIM, pl.ds(start, blk)]
            vt = jnp.concatenate([vt, ones], axis=0)
            st = st_ref[h]
            if diagonal:
                keys = lax.broadcasted_iota(jnp.int32, st.shape, 0)
                queries = lax.broadcasted_iota(jnp.int32, st.shape, 1)
                st = jnp.where(keys <= queries, st, -jnp.inf)
            m_prev = m_sc[h]
            m_new = jnp.maximum(m_prev, jnp.max(st, axis=0, keepdims=True))
            alpha = jnp.exp(m_prev - m_new)
            p = jnp.exp(st - m_new).astype(BF16)
            acc_sc[h] = alpha * acc_sc[h] + _dot(vt, p)
            m_sc[h] = m_new

    pairs = qi // 2
    scores(0, st_a)

    def body(j, carry):
        scores(2 * j + 1, st_b)
        consume(2 * j, st_a, False)
        scores(2 * j + 2, st_a)
        consume(2 * j + 1, st_b, False)
        return carry

    lax.fori_loop(0, pairs, body, 0)

    @pl.when(qi == 2 * pairs)
    def _():
        consume(qi, st_a, True)

    @pl.when(qi != 2 * pairs)
    def _():
        scores(qi, st_b)
        consume(qi - 1, st_a, False)
        consume(qi, st_b, True)

    for h in range(heads):
        acc = acc_sc[h]
        out_t = acc[:V_HEAD_DIM, :] / acc[V_HEAD_DIM:V_HEAD_DIM + 1, :]
        o_ref[:, h * V_HEAD_DIM:(h + 1) * V_HEAD_DIM] = jnp.transpose(out_t).astype(o_ref.dtype)


def _attention(q, k, vt, seq, batch, blk, heads):
    groups = batch * MLA_HEADS // heads
    return pl.pallas_call(
        functools.partial(_attn_kernel, blk=blk, heads=heads),
        grid=(groups, seq // blk),
        in_specs=[pl.BlockSpec((blk, heads * QK_PAD), lambda g, i: (i, g)),
                  pl.BlockSpec((seq, heads * QK_PAD), lambda g, i: (0, g)),
                  pl.BlockSpec((heads * V_HEAD_DIM, seq), lambda g, i: (g, 0))],
        out_specs=pl.BlockSpec((blk, heads * V_HEAD_DIM), lambda g, i: (i, g)),
        out_shape=jax.ShapeDtypeStruct((seq, batch * MLA_HEADS * V_HEAD_DIM), BF16),
        scratch_shapes=[pltpu.VMEM((heads, 1, blk), F32),
                        pltpu.VMEM((heads, V_HEAD_DIM + ONES_ROWS, blk), F32),
                        pltpu.VMEM((heads, blk, blk), F32), pltpu.VMEM((heads, blk, blk), F32)],
        compiler_params=_params("parallel", "arbitrary"),
        name="mla_attention",
    )(q, k, vt)


def _mix_epilogue(x, mix, g_ref, b_ref, wrh_ref, wrl_ref, br_ref, x1_ref, lg_ref, alpha):
    x1 = _layer_norm(alpha * x + mix, g_ref[...], b_ref[...])
    x1_ref[...] = x1
    lg_ref[...] = _router_logits(x1, wrh_ref[...], wrl_ref[...], br_ref[...])


def _post_attn_kernel(x_ref, o_ref, wo_ref, g_ref, b_ref, wrh_ref, wrl_ref, br_ref, x1_ref, lg_ref, *, alpha):
    mix = _dot(o_ref[...], wo_ref[...])
    _mix_epilogue(x_ref[...], mix, g_ref, b_ref, wrh_ref, wrl_ref, br_ref, x1_ref, lg_ref, alpha)


def _post_attn(x, o, wo, g, b, wr_hi, wr_lo, br, alpha, seq, batch, ts):
    hv, d = wo.shape
    col = functools.partial(_col_spec, ts)
    return pl.pallas_call(
        functools.partial(_post_attn_kernel, alpha=alpha),
        grid=(seq // ts, batch),
        in_specs=[_x_spec(x, ts, d), col(hv), _full(wo.shape), _full(g.shape), _full(b.shape),
                  _full(wr_hi.shape), _full(wr_lo.shape), _full(br.shape)],
        out_specs=[col(d), col(LANES)],
        out_shape=[jax.ShapeDtypeStruct((seq, batch * d), F32), jax.ShapeDtypeStruct((seq, batch * LANES), F32)],
        compiler_params=_params("parallel", "parallel"),
        name="post_attention",
    )(x, o, wo, g, b, wr_hi, wr_lo, br)


def _lru_kernel(x_ref, win_ref, cw_ref, cb_ref, wa_ref, ba_ref, wx_ref, bx_ref, lam_ref, wout_ref,
                g_ref, b_ref, wrh_ref, wrl_ref, br_ref, x1_ref, lg_ref,
                ucarry_sc, hcarry_sc, a_sc, b_sc, h_sc, *, alpha, batch):
    tm, width = a_sc.shape
    halo = (CONV_WIDTH - 1) * batch
    blk_w = width // LRU_BLOCKS

    @pl.when(pl.program_id(0) == 0)
    def _():
        ucarry_sc[...] = jnp.zeros(ucarry_sc.shape, F32)
        hcarry_sc[...] = jnp.zeros(hcarry_sc.shape, F32)

    x = x_ref[...]
    gu = _dot(x.astype(BF16), win_ref[...])
    gate = gu[:, :width]
    u = gu[:, width:]
    u_ext = jnp.concatenate([ucarry_sc[...], u], axis=0)
    ucarry_sc[...] = u[tm - halo:, :]
    cw = cw_ref[...]
    uc = cb_ref[...] + cw[0:1, :] * u_ext[0:tm, :]
    for j in range(1, CONV_WIDTH):
        uc = uc + cw[j:j + 1, :] * u_ext[j * batch:j * batch + tm, :]
    ucb = uc.astype(BF16)
    ra = jnp.concatenate([_dot(ucb[:, n * blk_w:(n + 1) * blk_w], wa_ref[n]) for n in range(LRU_BLOCKS)], axis=1)
    rx = jnp.concatenate([_dot(ucb[:, n * blk_w:(n + 1) * blk_w], wx_ref[n]) for n in range(LRU_BLOCKS)], axis=1)
    r = _sigmoid(ra + ba_ref[...])
    gi = _sigmoid(rx + bx_ref[...])
    z = -lam_ref[...]
    softplus = jnp.maximum(z, 0.0) + jnp.log1p(jnp.exp(-jnp.abs(z)))
    log_a = (-LRU_C) * r * softplus
    a_sc[...] = jnp.exp(log_a)
    th = jnp.tanh(log_a)
    b_sc[...] = jnp.sqrt(-2.0 * th / (1.0 - th)) * gi * uc

    def body(t, h):
        rows = pl.ds(pl.multiple_of(t * batch, batch), batch)
        h = a_sc[rows, :] * h + b_sc[rows, :]
        h_sc[rows, :] = h
        return h

    hcarry_sc[...] = lax.fori_loop(0, tm // batch, body, hcarry_sc[...], unroll=8)
    c = math.sqrt(2.0 / math.pi)
    gelu = 0.5 * gate * (1.0 + jnp.tanh(c * (gate + 0.044715 * (gate * gate * gate))))
    y = gelu * h_sc[...]
    mix = _dot(y.astype(BF16), wout_ref[...])
    _mix_epilogue(x, mix, g_ref, b_ref, wrh_ref, wrl_ref, br_ref, x1_ref, lg_ref, alpha)


def _lru_block(x, win, cw, cb, wa, ba, wx, bx, lam, wout, g, b, wr_hi, wr_lo, br, alpha, batch, tm):
    t, d = x.shape
    width = wout.shape[0]
    row = lambda c: pl.BlockSpec((tm, c), lambda i: (i, 0))
    consts = (win, cw, cb, wa, ba, wx, bx, lam, wout, g, b, wr_hi, wr_lo, br)
    return pl.pallas_call(
        functools.partial(_lru_kernel, alpha=alpha, batch=batch),
        grid=(t // tm,),
        in_specs=[row(d)] + [_full(c.shape) for c in consts],
        out_specs=[row(d), row(LANES)],
        out_shape=[jax.ShapeDtypeStruct((t, d), F32), jax.ShapeDtypeStruct((t, LANES), F32)],
        scratch_shapes=[pltpu.VMEM(((CONV_WIDTH - 1) * batch, width), F32), pltpu.VMEM((batch, width), F32),
                        pltpu.VMEM((tm, width), F32), pltpu.VMEM((tm, width), F32), pltpu.VMEM((tm, width), F32)],
        compiler_params=_params("arbitrary"),
        name="rglru_block",
    )(x, *consts)


SEG_ROWS = SUBLANES


def _route_kernel(lg_ref, gate_ref, spos_ref, spt_ref, seg_ref, cnt_ref, run_sc):
    tm = lg_ref.shape[0]

    @pl.when(pl.program_id(0) == 0)
    def _():
        run_sc[...] = jnp.zeros(run_sc.shape, F32)

    lane = lax.broadcasted_iota(jnp.int32, (tm, LANES), 1)
    lane_f = lane.astype(F32)
    work = jnp.where(lane < N_EXPERTS, lg_ref[...], -jnp.inf)
    tops, onehots = [], []
    for k in range(TOP_K):
        top = jnp.max(work, axis=1, keepdims=True)
        idx = jnp.min(jnp.where(work == top, lane_f, float(LANES)), axis=1, keepdims=True)
        hot = lane_f == idx
        work = jnp.where(hot, -jnp.inf, work)
        tops.append(top)
        onehots.append(hot)
    exps = [jnp.exp(top - tops[0]) for top in tops]
    denom = exps[0]
    for e in exps[1:]:
        denom = denom + e
    gate = jnp.zeros((tm, LANES), F32)
    for k in range(TOP_K):
        gate = jnp.where(lane == k, exps[k] / denom, gate)
    hits = onehots[0].astype(F32)
    for hot in onehots[1:]:
        hits = hits + hot.astype(F32)
    earlier = (lax.broadcasted_iota(jnp.int32, (tm, tm), 1) < lax.broadcasted_iota(jnp.int32, (tm, tm), 0))
    before = _dot(earlier.astype(BF16), hits.astype(BF16))
    count = jnp.sum(hits, axis=0, keepdims=True)
    chunks = jnp.floor((count + (SEG_ROWS - 1.0)) * (1.0 / SEG_ROWS))
    lower = (lax.broadcasted_iota(jnp.int32, (LANES, LANES), 0) < lax.broadcasted_iota(jnp.int32, (LANES, LANES), 1))
    chunks8 = jnp.broadcast_to(chunks, (SUBLANES, LANES))
    slab_off = _dot(chunks8.astype(BF16), lower.astype(BF16))[0:1, :] * SEG_ROWS
    run = run_sc[...]
    spos = jnp.zeros((tm, LANES), F32)
    for k in range(TOP_K):
        s_k = jnp.sum(jnp.where(onehots[k], before + slab_off, 0.0), axis=1, keepdims=True)
        spos = jnp.where(lane == k, s_k, spos)
    run_sc[...] = run + chunks * SEG_ROWS
    gate_ref[...] = gate
    spos_ref[...] = spos.astype(jnp.int32)
    spt_ref[0] = jnp.transpose(spos)[0:SUBLANES, :].astype(jnp.int32)
    row = lax.broadcasted_iota(jnp.int32, (SUBLANES, LANES), 0)
    seg = jnp.where(row == 0, chunks, jnp.where(row == 1, slab_off, jnp.where(row == 2, run, 0.0)))
    seg_ref[0] = seg.astype(jnp.int32)
    cnt_ref[...] = run_sc[...].astype(jnp.int32)


def _route(logits, tm):
    t = logits.shape[0]
    n_tiles = t // tm
    row = pl.BlockSpec((tm, LANES), lambda i: (i, 0))
    return pl.pallas_call(
        _route_kernel,
        grid=(n_tiles,),
        in_specs=[row],
        out_specs=[row, row, pl.BlockSpec((1, SUBLANES, tm), lambda i: (i, 0, 0)),
                   pl.BlockSpec((1, SUBLANES, LANES), lambda i: (i, 0, 0)),
                   pl.BlockSpec((1, LANES), lambda i: (0, 0))],
        out_shape=[jax.ShapeDtypeStruct((t, LANES), F32), jax.ShapeDtypeStruct((t, LANES), jnp.int32),
                   jax.ShapeDtypeStruct((n_tiles, SUBLANES, tm), jnp.int32),
                   jax.ShapeDtypeStruct((n_tiles, SUBLANES, LANES), jnp.int32),
                   jax.ShapeDtypeStruct((1, LANES), jnp.int32)],
        scratch_shapes=[pltpu.VMEM((1, LANES), F32)],
        compiler_params=_params("arbitrary"),
        name="moe_route",
    )(logits)


def _chunk_copy(src, src_row, dst, dst_row, sem):
    return pltpu.make_async_copy(src.at[pl.ds(pl.multiple_of(src_row, SEG_ROWS), SEG_ROWS)],
                                 dst.at[pl.ds(pl.multiple_of(dst_row, SEG_ROWS), SEG_ROWS)], sem)


def _for_each_segment_chunk(tile, nch_ref, loc_ref, glob_ref, fn):
    def per_expert(e, carry):
        s = tile * N_EXPERTS + e
        lo, go = loc_ref[s], glob_ref[s]

        def per_chunk(c, carry2):
            fn(lo + c * SEG_ROWS, go + c * SEG_ROWS)
            return carry2

        lax.fori_loop(0, nch_ref[s], per_chunk, 0)
        return carry

    lax.fori_loop(0, N_EXPERTS, per_expert, 0)


SLAB_CHUNK = 256


def _slab_rows(tm):
    worst = tm * TOP_K + N_EXPERTS * (SEG_ROWS - 1)
    return -(-worst // SLAB_CHUNK) * SLAB_CHUNK


def _dispatch_kernel(nch_ref, loc_ref, glob_ref, tail_ref, spt_ref, x_ref, xs_hbm, slab_sc, zero_sc, sem):
    i = pl.program_id(0)
    tm = x_ref.shape[0]
    slab_rows = slab_sc.shape[1]
    slot = lax.rem(i, 2)
    slab = slab_sc.at[slot]
    xb = x_ref[...].astype(BF16)
    sp = spt_ref[0]
    for c0 in range(0, slab_rows, SLAB_CHUNK):
        j = lax.broadcasted_iota(jnp.int32, (SLAB_CHUNK, tm), 0) + c0
        hit = j == sp[0:1, :]
        for k in range(1, TOP_K):
            hit = jnp.logical_or(hit, j == sp[k:k + 1, :])
        slab[c0:c0 + SLAB_CHUNK, :] = _dot(hit.astype(BF16), xb)

    def for_each_chunk(tile, fn):
        s = lax.rem(tile, 2)
        _for_each_segment_chunk(tile, nch_ref, loc_ref, glob_ref,
                                lambda lo, go: fn(_chunk_copy(slab_sc.at[s], lo, xs_hbm, go, sem.at[s])))

    @pl.when(i > 0)
    def _():
        for_each_chunk(i - 1, lambda cp: cp.wait())

    for_each_chunk(i, lambda cp: cp.start())

    def for_each_tail_chunk(fn):
        def per_chunk(c, carry):
            fn(_chunk_copy(zero_sc, 0, xs_hbm, tail_ref[0] + c * SEG_ROWS, sem.at[2]))
            return carry

        lax.fori_loop(0, tail_ref[1], per_chunk, 0)

    @pl.when(i == pl.num_programs(0) - 1)
    def _():
        zero_sc[...] = jnp.zeros(zero_sc.shape, zero_sc.dtype)
        for_each_tail_chunk(lambda cp: cp.start())
        for_each_chunk(i, lambda cp: cp.wait())
        for_each_tail_chunk(lambda cp: cp.wait())


def _dispatch(n_chunks, slab_off, dest_off, tail, spt, x1, n_pad):
    t, d = x1.shape
    n_tiles, _, tm = spt.shape
    slab_rows = _slab_rows(tm)
    grid_spec = pltpu.PrefetchScalarGridSpec(
        num_scalar_prefetch=4,
        grid=(n_tiles,),
        in_specs=[pl.BlockSpec((1, SUBLANES, tm), lambda i, *_: (i, 0, 0)),
                  pl.BlockSpec((tm, d), lambda i, *_: (i, 0))],
        out_specs=pl.BlockSpec(memory_space=pl.ANY),
        scratch_shapes=[pltpu.VMEM((2, slab_rows, d), F32), pltpu.VMEM((SEG_ROWS, d), F32),
                        pltpu.SemaphoreType.DMA((3,))],
    )
    return pl.pallas_call(
        _dispatch_kernel,
        grid_spec=grid_spec,
        out_shape=jax.ShapeDtypeStruct((n_pad, d), F32),
        compiler_params=_params("arbitrary"),
        name="moe_dispatch",
    )(n_chunks, slab_off, dest_off, tail, spt, x1)


def _expert_kernel(ie_ref, ix_ref, iy_ref, lo_ref, hi_ref, xs_ref, wup_ref, bup_ref, wdn_ref, bdn_ref, ys_ref,
                   wup_sc, wdn_sc):
    w = pl.program_id(0)
    bm = xs_ref.shape[0]
    d_ff = wdn_ref.shape[1]
    valid = hi_ref[w] > lo_ref[w]
    new_expert = jnp.logical_or(w == 0, ie_ref[w] != ie_ref[jnp.maximum(w - 1, 0)])

    @pl.when(jnp.logical_and(valid, new_expert))
    def _():
        wup_sc[...] = wup_ref[0].astype(BF16)
        wdn_sc[...] = wdn_ref[0].astype(BF16)

    @pl.when(jnp.logical_and(lo_ref[w] == 0, hi_ref[w] == 0))
    def _():
        ys_ref[...] = jnp.zeros(ys_ref.shape, ys_ref.dtype)

    @pl.when(valid)
    def _():
        hb = _dot(xs_ref[...].astype(BF16), wup_sc[...]) + bup_ref[0]
        gl = jnp.minimum(hb[:, :d_ff], SWIGLU_LIMIT)
        up = jnp.clip(hb[:, d_ff:], -SWIGLU_LIMIT, SWIGLU_LIMIT)
        yb = (up + 1.0) * (gl * _sigmoid(SWIGLU_ALPHA * gl))
        y = _dot(yb.astype(BF16), wdn_sc[...]) + bdn_ref[0]
        rows = lax.broadcasted_iota(jnp.int32, (bm, 1), 0)
        mine = jnp.logical_and(rows >= lo_ref[w], rows < hi_ref[w])
        first_visit = lo_ref[w] == 0

        @pl.when(first_visit)
        def _():
            ys_ref[...] = jnp.where(mine, y, 0.0)

        @pl.when(jnp.logical_not(first_visit))
        def _():
            ys_ref[...] = jnp.where(mine, y, ys_ref[...])


def _experts(item_e, item_x, item_y, item_lo, item_hi, xs, w_up, b_up, w_down, b_down, layer, bm):
    n, d = xs.shape
    f2 = w_up.shape[3]
    n_max = item_e.shape[0]
    xblk = lambda w, ie, ix, iy, lo, hi: (ix[w], 0)
    yblk = lambda w, ie, ix, iy, lo, hi: (iy[w], 0)
    exp3 = lambda w, ie, ix, iy, lo, hi: (layer, ie[w], 0, 0)
    grid_spec = pltpu.PrefetchScalarGridSpec(
        num_scalar_prefetch=5,
        grid=(n_max,),
        in_specs=[pl.BlockSpec((bm, d), xblk),
                  pl.BlockSpec((None, 1, d, f2), exp3), pl.BlockSpec((None, 1, 1, f2), exp3),
                  pl.BlockSpec((None, 1, f2 // 2, d), exp3), pl.BlockSpec((None, 1, 1, d), exp3)],
        out_specs=pl.BlockSpec((bm, d), yblk),
        scratch_shapes=[pltpu.VMEM((d, f2), BF16), pltpu.VMEM((f2 // 2, d), BF16)],
    )
    return pl.pallas_call(
        _expert_kernel,
        grid_spec=grid_spec,
        out_shape=jax.ShapeDtypeStruct((n, d), F32),
        compiler_params=_params("arbitrary"),
        name="moe_experts",
    )(item_e, item_x, item_y, item_lo, item_hi, xs, w_up, b_up, w_down, b_down)


def _post_moe_kernel(nch_ref, loc_ref, glob_ref, x1_ref, gate_ref, spos_ref, p_ref, g_ref, b_ref, wg_ref, wp_ref,
                     ys_hbm, out_ref, slab_sc, sem, *, alpha):
    i = pl.program_id(0)
    n_tiles = pl.num_programs(0)
    tm = x1_ref.shape[0]
    slab_rows = slab_sc.shape[1]

    def for_each_chunk(tile, fn):
        s = lax.rem(tile, 2)
        _for_each_segment_chunk(tile, nch_ref, loc_ref, glob_ref,
                                lambda lo, go: fn(_chunk_copy(ys_hbm, go, slab_sc.at[s], lo, sem.at[s])))

    @pl.when(i == 0)
    def _():
        slab_sc[...] = jnp.zeros(slab_sc.shape, slab_sc.dtype)
        for_each_chunk(i, lambda cp: cp.start())

    @pl.when(i + 1 < n_tiles)
    def _():
        for_each_chunk(i + 1, lambda cp: cp.start())

    for_each_chunk(i, lambda cp: cp.wait())
    slab = slab_sc.at[lax.rem(i, 2)]
    gates = gate_ref[...]
    spos = spos_ref[...]
    ffn = jnp.zeros(x1_ref.shape, F32)
    for c0 in range(0, slab_rows, SLAB_CHUNK):
        col = lax.broadcasted_iota(jnp.int32, (tm, SLAB_CHUNK), 1) + c0
        weight = jnp.where(col == spos[:, 0:1], gates[:, 0:1], 0.0)
        for k in range(1, TOP_K):
            weight = weight + jnp.where(col == spos[:, k:k + 1], gates[:, k:k + 1], 0.0)
        ffn = ffn + _dot(weight.astype(BF16), slab[c0:c0 + SLAB_CHUNK, :].astype(BF16))
    x2 = _layer_norm(alpha * x1_ref[...] + ffn, g_ref[...], b_ref[...])
    gate = _sigmoid(_dot(x2.astype(BF16), wg_ref[...]))
    batch, ts = p_ref.shape[0], p_ref.shape[1]
    p_bm = p_ref[...].reshape(tm, p_ref.shape[2]).astype(BF16)
    tok = lax.broadcasted_iota(jnp.int32, (tm, tm), 0)
    src = lax.broadcasted_iota(jnp.int32, (tm, tm), 1)
    shift = batch.bit_length() - 1
    pick = src == jnp.bitwise_and(tok, batch - 1) * ts + lax.shift_right_logical(tok, shift)
    p_tok = _dot(pick.astype(BF16), p_bm).astype(BF16)
    out_ref[...] = x2 + gate * _dot(p_tok, wp_ref[...])


def _post_moe(n_chunks, slab_off, dest_off, x1, gates, spos, ys, p, layer, g, b, wg, wp, alpha, batch, tm):
    t, d = x1.shape
    ts = tm // batch
    row = lambda c: pl.BlockSpec((tm, c), lambda i, *_: (i, 0))
    grid_spec = pltpu.PrefetchScalarGridSpec(
        num_scalar_prefetch=3,
        grid=(t // tm,),
        in_specs=[row(d), row(LANES), row(LANES),
                  pl.BlockSpec((None, batch, ts, p.shape[-1]), lambda i, *_: (layer, 0, i, 0)),
                  _full(g.shape), _full(b.shape), _full(wg.shape), _full(wp.shape),
                  pl.BlockSpec(memory_space=pl.ANY)],
        out_specs=row(d),
        scratch_shapes=[pltpu.VMEM((2, _slab_rows(tm), d), F32), pltpu.SemaphoreType.DMA((2,))],
    )
    return pl.pallas_call(
        functools.partial(_post_moe_kernel, alpha=alpha),
        grid_spec=grid_spec,
        out_shape=jax.ShapeDtypeStruct((t, d), F32),
        compiler_params=_params("arbitrary"),
        name="post_moe_ple",
    )(n_chunks, slab_off, dest_off, x1, gates, spos, p, g, b, wg, wp, ys)


def _moe_plan(counts, n_pad, bm):
    i32 = jnp.int32
    n_blocks = n_pad // bm
    n_max = n_blocks + N_EXPERTS - 1
    ends = jnp.cumsum(counts)
    starts = ends - counts
    total = ends[-1]
    first_blk = starts // bm
    n_items_e = jnp.where(counts > 0, (ends - 1) // bm - first_blk + 1, 0)
    item_end = jnp.cumsum(n_items_e)
    item_start = item_end - n_items_e
    n_compute = item_end[-1]
    n_used = (total + bm - 1) // bm
    n_active = n_compute + (n_blocks - n_used)
    w_all = jnp.arange(n_max, dtype=i32)
    w = jnp.minimum(w_all, n_active - 1)
    is_compute = w < n_compute
    wc = jnp.minimum(w, n_compute - 1)
    item_e = jnp.sum((wc[:, None] >= item_end[None, :]).astype(i32), axis=1)
    blk_c = first_blk[item_e] + wc - item_start[item_e]
    item_y = jnp.where(is_compute, blk_c, n_used + (w - n_compute)).astype(i32)
    item_x = jnp.where(is_compute, blk_c, 0).astype(i32)
    lo_c = jnp.maximum(starts[item_e] - blk_c * bm, 0)
    hi_c = jnp.minimum(ends[item_e] - blk_c * bm, bm)
    item_lo = jnp.where(w_all >= n_active, -1, jnp.where(is_compute, lo_c, 0)).astype(i32)
    item_hi = jnp.where(w_all >= n_active, -1, jnp.where(is_compute, hi_c, 0)).astype(i32)
    tail = jnp.stack([total, (n_pad - total) // SEG_ROWS]).astype(i32)
    return starts, tail, item_e.astype(i32), item_x, item_y, item_lo, item_hi


def _moe_dispatch(x1, logits, w_up, b_up, w_down, b_down, layer, tm, bm):
    t, d = x1.shape
    n_tiles = t // tm
    worst = t * TOP_K + n_tiles * N_EXPERTS * (SEG_ROWS - 1)
    n_pad = -(-worst // bm) * bm
    gates, spos, spt, seg, counts = _route(logits, tm)
    starts, tail, item_e, item_x, item_y, item_lo, item_hi = _moe_plan(counts[0, :N_EXPERTS], n_pad, bm)
    n_chunks = seg[:, 0, :N_EXPERTS].reshape(-1)
    slab_off = seg[:, 1, :N_EXPERTS].reshape(-1)
    dest_off = (starts[None, :] + seg[:, 2, :N_EXPERTS]).reshape(-1).astype(jnp.int32)
    xs = _dispatch(n_chunks, slab_off, dest_off, tail, spt, x1, n_pad)
    ys = _experts(item_e, item_x, item_y, item_lo, item_hi, xs, w_up, b_up, w_down, b_down, layer, bm)
    return (n_chunks, slab_off, dest_off), gates, spos, ys


def _pick_tile(n, target, quantum):
    tile = min(n, target)
    while n % tile or tile % quantum:
        tile -= quantum
    return tile


def kernel(x, p, positions, mla_w_in, mla_q_norm, mla_kv_norm, mla_w_uq, mla_w_ukv, mla_w_o, lru_w_in, lru_conv_w, lru_conv_b, lru_w_a, lru_b_a, lru_w_x, lru_b_x, lru_lambda, lru_w_out, ln1_g, ln1_b, ln2_g, ln2_b, moe_w_router, moe_b_router, moe_w_up, moe_b_up, moe_w_down, moe_b_down, ple_w_gate, ple_w_proj):
    batch, seq, d = x.shape
    depth = ln1_g.shape[0]
    t = batch * seq
    assert batch == SUBLANES, "the recurrence kernel maps the batch onto the sublanes of a vreg"
    alpha = (2.0 * depth) ** 0.25
    tm = _pick_tile(t, 512, SUBLANES * batch)
    attn_blk = _pick_tile(seq, 512, LANES)
    moe_bm = _pick_tile(t * TOP_K, 512, SUBLANES)

    xt = x
    half = QK_ROPE_DIM // 2
    inv_freq = jnp.exp(-math.log(ROPE_THETA) * jnp.arange(half, dtype=F32) / half)
    rope_freq = jnp.concatenate([inv_freq, inv_freq, jnp.zeros((LANES - QK_ROPE_DIM,), F32)]).reshape(1, LANES)
    pos_lanes = jnp.broadcast_to(jnp.transpose(positions).astype(F32)[:, :, None],
                                 (seq, batch, LANES)).reshape(seq, batch * LANES)

    row2 = lambda a: a.reshape(1, -1)
    scale = 1.0 / math.sqrt(QK_NOPE_DIM + QK_ROPE_DIM)

    for layer in range(depth):
        j = layer // 2
        wr = jnp.pad(moe_w_router[layer], ((0, 0), (0, LANES - N_EXPERTS)))
        wr_hi, wr_lo = _split_bf16(wr)
        br = row2(jnp.pad(moe_b_router[layer], (0, LANES - N_EXPERTS)))
        g1, b1 = row2(ln1_g[layer]), row2(ln1_b[layer])
        if layer % 2 == 0:
            q_lora, kv_lora = mla_q_norm.shape[1], mla_kv_norm.shape[1]
            w_in = mla_w_in[j]
            win_p = jnp.pad(w_in, ((0, 0), (0, LANES - QK_ROPE_DIM))).astype(BF16)
            wuq = mla_w_uq[j].reshape(q_lora, MLA_HEADS, QK_NOPE_DIM + QK_ROPE_DIM)
            wuq_p = jnp.pad(wuq, ((0, 0), (0, 0), (0, QK_PAD - QK_NOPE_DIM - QK_ROPE_DIM)))
            wuq_p = wuq_p.reshape(q_lora, MLA_HEADS * QK_PAD).astype(BF16)
            q, k, vt = _mla_proj(xt, pos_lanes, rope_freq, win_p,
                                 row2(mla_q_norm[j] * scale), row2(mla_kv_norm[j]),
                                 wuq_p, mla_w_ukv[j].astype(BF16), seq, batch, attn_blk)
            o = _attention(q, k, vt, seq, batch, attn_blk, 2)
            x1, logits = _post_attn(xt, o, mla_w_o[j].astype(BF16), g1, b1,
                                    wr_hi, wr_lo, br, alpha, seq, batch, attn_blk)
        else:
            x1, logits = _lru_block(
                xt.reshape(t, d), lru_w_in[j].astype(BF16), lru_conv_w[j], row2(lru_conv_b[j]),
                lru_w_a[j].astype(BF16), row2(lru_b_a[j]), lru_w_x[j].astype(BF16), row2(lru_b_x[j]),
                row2(lru_lambda[j]), lru_w_out[j].astype(BF16), g1, b1, wr_hi, wr_lo, br, alpha, batch, tm)
        x1 = x1.reshape(t, d)
        tables, gates, spos, ys = _moe_dispatch(x1, logits.reshape(t, LANES), moe_w_up,
                                                moe_b_up[:, :, None, :], moe_w_down, moe_b_down[:, :, None, :],
                                                layer, tm, moe_bm)
        xt = _post_moe(*tables, x1, gates, spos, ys, p, layer, row2(ln2_g[layer]), row2(ln2_b[layer]),
                       ple_w_gate[layer].astype(BF16), ple_w_proj[layer].astype(BF16), alpha, batch, tm)
        if layer + 1 < depth and (layer + 1) % 2 == 0:
            xt = xt.reshape(seq, batch * d)
    return jnp.transpose(xt.reshape(seq, batch, d), (1, 0, 2))
```

```python
import functools
import math

import jax
import jax.numpy as jnp
from jax import lax
from jax.experimental import pallas as pl
from jax.experimental.pallas import tpu as pltpu

F32 = jnp.float32
BF16 = jnp.bfloat16

MLA_HEADS = 8
QK_NOPE_DIM = 128
QK_ROPE_DIM = 64
V_HEAD_DIM = 128
ROPE_THETA = 10000.0
LRU_BLOCKS = 4
CONV_WIDTH = 4
LRU_C = 8.0
N_EXPERTS = 32
TOP_K = 4
SWIGLU_LIMIT = 7.0
SWIGLU_ALPHA = 1.702
LN_EPS = 1e-5
RMS_EPS = 1e-6

LANES = 128
SUBLANES = 8
QK_PAD = 2 * LANES
VMEM_LIMIT_BYTES = 56 * 1024 * 1024


def _params(*sem, flags=None):
    return pltpu.CompilerParams(dimension_semantics=sem, vmem_limit_bytes=VMEM_LIMIT_BYTES, flags=flags)


def _dot(a, b):
    return jnp.dot(a, b, preferred_element_type=F32)


def _sigmoid(z):
    return 1.0 / (1.0 + jnp.exp(-z))


def _layer_norm(y, g, b):
    mu = jnp.mean(y, axis=-1, keepdims=True)
    yc = y - mu
    var = jnp.mean(yc * yc, axis=-1, keepdims=True)
    return yc * lax.rsqrt(var + LN_EPS) * g + b


def _rms_norm(y, g):
    ms = jnp.mean(y * y, axis=-1, keepdims=True)
    return y * lax.rsqrt(ms + RMS_EPS) * g


def _split_bf16(a):
    hi = a.astype(BF16)
    lo = (a - hi.astype(F32)).astype(BF16)
    return hi, lo


def _router_logits(x1, wr_hi, wr_lo, br):
    hi, lo = _split_bf16(x1)
    return _dot(hi, wr_hi) + (_dot(hi, wr_lo) + _dot(lo, wr_hi)) + br


def _full(shape):
    return pl.BlockSpec(shape, lambda *_: (0,) * len(shape))


def _rope(blk, cos, sin_a, sin_b):
    return (blk * cos + pltpu.roll(blk, LANES - QK_ROPE_DIM // 2, 1) * sin_a
            + pltpu.roll(blk, QK_ROPE_DIM // 2, 1) * sin_b)


def _mla_proj_kernel(x_ref, pos_ref, freq_ref, win_ref, qn_ref, kvn_ref, wuq_ref, wukv_ref,
                     q_ref, k_ref, vt_ref, *, q_lora, kv_lora):
    lat = _dot(x_ref[...].astype(BF16), win_ref[...])
    c_q = _rms_norm(lat[:, :q_lora], qn_ref[...])
    c_kv = _rms_norm(lat[:, q_lora:q_lora + kv_lora], kvn_ref[...])
    ang = pos_ref[...] * freq_ref[...]
    lane = lax.broadcasted_iota(jnp.int32, ang.shape, 1)
    cos, sin = jnp.cos(ang), jnp.sin(ang)
    half = QK_ROPE_DIM // 2
    sin_a = jnp.where(lane < half, -sin, 0.0)
    sin_b = jnp.where(jnp.logical_and(lane >= half, lane < QK_ROPE_DIM), sin, 0.0)
    k_rope = _rope(lat[:, q_lora + kv_lora:], cos, sin_a, sin_b).astype(BF16)
    q = _dot(c_q.astype(BF16), wuq_ref[...])
    kv = _dot(c_kv.astype(BF16), wukv_ref[...])
    for h in range(MLA_HEADS):
        lo = h * QK_PAD
        mid = lo + LANES
        hi = lo + QK_PAD
        q_ref[:, lo:mid] = q[:, lo:mid].astype(BF16)
        q_ref[:, mid:hi] = _rope(q[:, mid:hi], cos, sin_a, sin_b).astype(BF16)
        k_ref[:, lo:mid] = kv[:, lo:mid].astype(BF16)
        k_ref[:, mid:hi] = k_rope
        vt_ref[h * V_HEAD_DIM:(h + 1) * V_HEAD_DIM, :] = jnp.transpose(kv[:, mid:hi]).astype(BF16)


def _col_spec(ts, c):
    return pl.BlockSpec((ts, c), lambda i, b: (i, b))


def _x_spec(x, ts, d):
    if x.ndim == 3:
        return pl.BlockSpec((None, ts, d), lambda i, b: (b, i, 0))
    return _col_spec(ts, d)


def _mla_proj(x, pos, freq, win_p, qn, kvn, wuq_p, wukv, seq, batch, ts):
    d = win_p.shape[0]
    q_lora, kv_lora = qn.shape[1], kvn.shape[1]
    hq = MLA_HEADS * QK_PAD
    hv = MLA_HEADS * V_HEAD_DIM
    col = functools.partial(_col_spec, ts)
    return pl.pallas_call(
        functools.partial(_mla_proj_kernel, q_lora=q_lora, kv_lora=kv_lora),
        grid=(seq // ts, batch),
        in_specs=[_x_spec(x, ts, d), col(LANES), _full(freq.shape), _full(win_p.shape), _full(qn.shape),
                  _full(kvn.shape), _full(wuq_p.shape), _full(wukv.shape)],
        out_specs=[col(hq), col(hq), pl.BlockSpec((hv, ts), lambda i, b: (b, i))],
        out_shape=[jax.ShapeDtypeStruct((seq, batch * hq), BF16), jax.ShapeDtypeStruct((seq, batch * hq), BF16),
                   jax.ShapeDtypeStruct((batch * hv, seq), BF16)],
        compiler_params=_params("parallel", "parallel"),
        name="mla_proj",
    )(x, pos, freq, win_p, qn, kvn, wuq_p, wukv)


ONES_ROWS = 16


def _attn_kernel(q_ref, k_ref, vt_ref, o_ref, m_sc, acc_sc, st_a, st_b, *, blk, heads):
    qi = pl.program_id(1)
    m_sc[...] = jnp.full(m_sc.shape, -jnp.inf, F32)
    acc_sc[...] = jnp.zeros(acc_sc.shape, F32)
    ones = jnp.ones((ONES_ROWS, blk), BF16)

    def scores(ki, st_ref):
        start = pl.multiple_of(ki * blk, blk)
        for h in range(heads):
            k = k_ref[pl.ds(start, blk), h * QK_PAD:(h + 1) * QK_PAD]
            st_ref[h] = lax.dot_general(k, q_ref[:, h * QK_PAD:(h + 1) * QK_PAD], (((1,), (1,)), ((), ())),
                                        preferred_element_type=F32)

    def consume(ki, st_ref, diagonal):
        start = pl.multiple_of(ki * blk, blk)
        for h in range(heads):
            vt = vt_ref[h * V_HEAD_DIM:(h + 1) * V_HEAD_DIM, pl.ds(start, blk)]
            vt = jnp.concatenate([vt, ones], axis=0)
            st = st_ref[h]
            if diagonal:
                keys = lax.broadcasted_iota(jnp.int32, st.shape, 0)
                queries = lax.broadcasted_iota(jnp.int32, st.shape, 1)
                st = jnp.where(keys <= queries, st, -jnp.inf)
            m_prev = m_sc[h]
            m_new = jnp.maximum(m_prev, jnp.max(st, axis=0, keepdims=True))
            alpha = jnp.exp2(m_prev - m_new)
            p = jnp.exp2(st - m_new).astype(BF16)
            acc_sc[h] = alpha * acc_sc[h] + _dot(vt, p)
            m_sc[h] = m_new

    pairs = qi // 2
    scores(0, st_a)

    def body(j, carry):
        scores(2 * j + 1, st_b)
        consume(2 * j, st_a, False)
        scores(2 * j + 2, st_a)
        consume(2 * j + 1, st_b, False)
        return carry

    lax.fori_loop(0, pairs, body, 0)

    @pl.when(qi == 2 * pairs)
    def _():
        consume(qi, st_a, True)

    @pl.when(qi != 2 * pairs)
    def _():
        scores(qi, st_b)
        consume(qi - 1, st_a, False)
        consume(qi, st_b, True)

    for h in range(heads):
        acc = acc_sc[h]
        out_t = acc[:V_HEAD_DIM, :] / acc[V_HEAD_DIM:V_HEAD_DIM + 1, :]
        o_ref[:, h * V_HEAD_DIM:(h + 1) * V_HEAD_DIM] = jnp.transpose(out_t).astype(o_ref.dtype)


def _attention(q, k, vt, seq, batch, blk, heads):
    groups = batch * MLA_HEADS // heads
    return pl.pallas_call(
        functools.partial(_attn_kernel, blk=blk, heads=heads),
        grid=(groups, seq // blk),
        in_specs=[pl.BlockSpec((blk, heads * QK_PAD), lambda g, i: (i, g)),
                  pl.BlockSpec((seq, heads * QK_PAD), lambda g, i: (0, g)),
                  pl.BlockSpec((heads * V_HEAD_DIM, seq), lambda g, i: (g, 0))],
        out_specs=pl.BlockSpec((blk, heads * V_HEAD_DIM), lambda g, i: (i, g)),
        out_shape=jax.ShapeDtypeStruct((seq, batch * MLA_HEADS * V_HEAD_DIM), BF16),
        scratch_shapes=[pltpu.VMEM((heads, 1, blk), F32),
                        pltpu.VMEM((heads, V_HEAD_DIM + ONES_ROWS, blk), F32),
                        pltpu.VMEM((heads, blk, blk), F32), pltpu.VMEM((heads, blk, blk), F32)],
        compiler_params=_params("parallel", "arbitrary"),
        name="mla_attention",
    )(q, k, vt)


def _mix_epilogue(x, mix, g_ref, b_ref, wrh_ref, wrl_ref, br_ref, x1_ref, lg_ref, alpha):
    x1 = _layer_norm(alpha * x + mix, g_ref[...], b_ref[...])
    x1_ref[...] = x1
    lg_ref[...] = _router_logits(x1, wrh_ref[...], wrl_ref[...], br_ref[...])


def _post_attn_kernel(x_ref, o_ref, wo_ref, g_ref, b_ref, wrh_ref, wrl_ref, br_ref, x1_ref, lg_ref, *, alpha):
    mix = _dot(o_ref[...], wo_ref[...])
    _mix_epilogue(x_ref[...], mix, g_ref, b_ref, wrh_ref, wrl_ref, br_ref, x1_ref, lg_ref, alpha)


def _post_attn(x, o, wo, g, b, wr_hi, wr_lo, br, alpha, seq, batch, ts):
    hv, d = wo.shape
    col = functools.partial(_col_spec, ts)
    return pl.pallas_call(
        functools.partial(_post_attn_kernel, alpha=alpha),
        grid=(seq // ts, batch),
        in_specs=[_x_spec(x, ts, d), col(hv), _full(wo.shape), _full(g.shape), _full(b.shape),
                  _full(wr_hi.shape), _full(wr_lo.shape), _full(br.shape)],
        out_specs=[col(d), col(LANES)],
        out_shape=[jax.ShapeDtypeStruct((seq, batch * d), F32), jax.ShapeDtypeStruct((seq, batch * LANES), F32)],
        compiler_params=_params("parallel", "parallel"),
        name="post_attention",
    )(x, o, wo, g, b, wr_hi, wr_lo, br)


def _lru_kernel(x_ref, win_ref, cw_ref, cb_ref, wa_ref, ba_ref, wx_ref, bx_ref, lam_ref, wout_ref,
                g_ref, b_ref, wrh_ref, wrl_ref, br_ref, x1_ref, lg_ref,
                ucarry_sc, hcarry_sc, a_sc, b_sc, h_sc, *, alpha, batch):
    tm, width = a_sc.shape
    halo = (CONV_WIDTH - 1) * batch
    blk_w = width // LRU_BLOCKS

    @pl.when(pl.program_id(0) == 0)
    def _():
        ucarry_sc[...] = jnp.zeros(ucarry_sc.shape, F32)
        hcarry_sc[...] = jnp.zeros(hcarry_sc.shape, F32)

    x = x_ref[...]
    gu = _dot(x.astype(BF16), win_ref[...])
    gate = gu[:, :width]
    u = gu[:, width:]
    u_ext = jnp.concatenate([ucarry_sc[...], u], axis=0)
    ucarry_sc[...] = u[tm - halo:, :]
    cw = cw_ref[...]
    uc = cb_ref[...] + cw[0:1, :] * u_ext[0:tm, :]
    for j in range(1, CONV_WIDTH):
        uc = uc + cw[j:j + 1, :] * u_ext[j * batch:j * batch + tm, :]
    ucb = uc.astype(BF16)
    ra = jnp.concatenate([_dot(ucb[:, n * blk_w:(n + 1) * blk_w], wa_ref[n]) for n in range(LRU_BLOCKS)], axis=1)
    rx = jnp.concatenate([_dot(ucb[:, n * blk_w:(n + 1) * blk_w], wx_ref[n]) for n in range(LRU_BLOCKS)], axis=1)
    r = _sigmoid(ra + ba_ref[...])
    gi = _sigmoid(rx + bx_ref[...])
    z = -lam_ref[...]
    softplus = jnp.maximum(z, 0.0) + jnp.log1p(jnp.exp(-jnp.abs(z)))
    log_a = (-LRU_C) * r * softplus
    a_sc[...] = jnp.exp(log_a)
    th = jnp.tanh(log_a)
    b_sc[...] = jnp.sqrt(-2.0 * th / (1.0 - th)) * gi * uc

    def body(t, h):
        rows = pl.ds(pl.multiple_of(t * batch, batch), batch)
        h = a_sc[rows, :] * h + b_sc[rows, :]
        h_sc[rows, :] = h
        return h

    hcarry_sc[...] = lax.fori_loop(0, tm // batch, body, hcarry_sc[...], unroll=8)
    c = math.sqrt(2.0 / math.pi)
    gelu = 0.5 * gate * (1.0 + jnp.tanh(c * (gate + 0.044715 * (gate * gate * gate))))
    y = gelu * h_sc[...]
    mix = _dot(y.astype(BF16), wout_ref[...])
    _mix_epilogue(x, mix, g_ref, b_ref, wrh_ref, wrl_ref, br_ref, x1_ref, lg_ref, alpha)


def _lru_block(x, win, cw, cb, wa, ba, wx, bx, lam, wout, g, b, wr_hi, wr_lo, br, alpha, batch, tm):
    t, d = x.shape
    width = wout.shape[0]
    row = lambda c: pl.BlockSpec((tm, c), lambda i: (i, 0))
    consts = (win, cw, cb, wa, ba, wx, bx, lam, wout, g, b, wr_hi, wr_lo, br)
    return pl.pallas_call(
        functools.partial(_lru_kernel, alpha=alpha, batch=batch),
        grid=(t // tm,),
        in_specs=[row(d)] + [_full(c.shape) for c in consts],
        out_specs=[row(d), row(LANES)],
        out_shape=[jax.ShapeDtypeStruct((t, d), F32), jax.ShapeDtypeStruct((t, LANES), F32)],
        scratch_shapes=[pltpu.VMEM(((CONV_WIDTH - 1) * batch, width), F32), pltpu.VMEM((batch, width), F32),
                        pltpu.VMEM((tm, width), F32), pltpu.VMEM((tm, width), F32), pltpu.VMEM((tm, width), F32)],
        compiler_params=_params("arbitrary"),
        name="rglru_block",
    )(x, *consts)


SEG_ROWS = SUBLANES


def _route_kernel(lg_ref, gate_ref, spos_ref, spt_ref, seg_ref, cnt_ref, run_sc):
    tm = lg_ref.shape[0]

    @pl.when(pl.program_id(0) == 0)
    def _():
        run_sc[...] = jnp.zeros(run_sc.shape, F32)

    lane = lax.broadcasted_iota(jnp.int32, (tm, LANES), 1)
    lane_f = lane.astype(F32)
    work = jnp.where(lane < N_EXPERTS, lg_ref[...], -jnp.inf)
    tops, onehots = [], []
    for k in range(TOP_K):
        top = jnp.max(work, axis=1, keepdims=True)
        idx = jnp.min(jnp.where(work == top, lane_f, float(LANES)), axis=1, keepdims=True)
        hot = lane_f == idx
        work = jnp.where(hot, -jnp.inf, work)
        tops.append(top)
        onehots.append(hot)
    exps = [jnp.exp(top - tops[0]) for top in tops]
    denom = exps[0]
    for e in exps[1:]:
        denom = denom + e
    gate = jnp.zeros((tm, LANES), F32)
    for k in range(TOP_K):
        gate = jnp.where(lane == k, exps[k] / denom, gate)
    hits = onehots[0].astype(F32)
    for hot in onehots[1:]:
        hits = hits + hot.astype(F32)
    earlier = (lax.broadcasted_iota(jnp.int32, (tm, tm), 1) < lax.broadcasted_iota(jnp.int32, (tm, tm), 0))
    before = _dot(earlier.astype(BF16), hits.astype(BF16))
    count = jnp.sum(hits, axis=0, keepdims=True)
    chunks = jnp.floor((count + (SEG_ROWS - 1.0)) * (1.0 / SEG_ROWS))
    lower = (lax.broadcasted_iota(jnp.int32, (LANES, LANES), 0) < lax.broadcasted_iota(jnp.int32, (LANES, LANES), 1))
    chunks8 = jnp.broadcast_to(chunks, (SUBLANES, LANES))
    slab_off = _dot(chunks8.astype(BF16), lower.astype(BF16))[0:1, :] * SEG_ROWS
    run = run_sc[...]
    spos = jnp.zeros((tm, LANES), F32)
    for k in range(TOP_K):
        s_k = jnp.sum(jnp.where(onehots[k], before + slab_off, 0.0), axis=1, keepdims=True)
        spos = jnp.where(lane == k, s_k, spos)
    run_sc[...] = run + chunks * SEG_ROWS
    gate_ref[...] = gate
    spos_ref[...] = spos.astype(jnp.int32)
    spt_ref[0] = jnp.transpose(spos)[0:SUBLANES, :].astype(jnp.int32)
    row = lax.broadcasted_iota(jnp.int32, (SUBLANES, LANES), 0)
    seg = jnp.where(row == 0, chunks, jnp.where(row == 1, slab_off, jnp.where(row == 2, run, 0.0)))
    seg_ref[0] = seg.astype(jnp.int32)
    cnt_ref[...] = run_sc[...].astype(jnp.int32)


def _route(logits, tm):
    t = logits.shape[0]
    n_tiles = t // tm
    row = pl.BlockSpec((tm, LANES), lambda i: (i, 0))
    return pl.pallas_call(
        _route_kernel,
        grid=(n_tiles,),
        in_specs=[row],
        out_specs=[row, row, pl.BlockSpec((1, SUBLANES, tm), lambda i: (i, 0, 0)),
                   pl.BlockSpec((1, SUBLANES, LANES), lambda i: (i, 0, 0)),
                   pl.BlockSpec((1, LANES), lambda i: (0, 0))],
        out_shape=[jax.ShapeDtypeStruct((t, LANES), F32), jax.ShapeDtypeStruct((t, LANES), jnp.int32),
                   jax.ShapeDtypeStruct((n_tiles, SUBLANES, tm), jnp.int32),
                   jax.ShapeDtypeStruct((n_tiles, SUBLANES, LANES), jnp.int32),
                   jax.ShapeDtypeStruct((1, LANES), jnp.int32)],
        scratch_shapes=[pltpu.VMEM((1, LANES), F32)],
        compiler_params=_params("arbitrary"),
        name="moe_route",
    )(logits)


BIG_COPY_CHUNKS = 4


def _chunk_copy(src, src_row, dst, dst_row, sem, rows=SEG_ROWS):
    return pltpu.make_async_copy(src.at[pl.ds(pl.multiple_of(src_row, SEG_ROWS), rows)],
                                 dst.at[pl.ds(pl.multiple_of(dst_row, SEG_ROWS), rows)], sem)


def _for_each_segment_copy(tile, nch_ref, loc_ref, glob_ref, fn):
    big = BIG_COPY_CHUNKS * SEG_ROWS
    shift = BIG_COPY_CHUNKS.bit_length() - 1

    def per_expert(e, carry):
        s = tile * N_EXPERTS + e
        lo, go, n = loc_ref[s], glob_ref[s], nch_ref[s]
        n_big = lax.shift_right_logical(n, shift)

        def big_copy(c, carry2):
            fn(lo + c * big, go + c * big, big)
            return carry2

        def small_copy(c, carry2):
            fn(lo + n_big * big + c * SEG_ROWS, go + n_big * big + c * SEG_ROWS, SEG_ROWS)
            return carry2

        lax.fori_loop(0, n_big, big_copy, 0)
        lax.fori_loop(0, n - n_big * BIG_COPY_CHUNKS, small_copy, 0)
        return carry

    lax.fori_loop(0, N_EXPERTS, per_expert, 0)


SLAB_CHUNK = 256


def _slab_rows(tm):
    worst = tm * TOP_K + N_EXPERTS * (SEG_ROWS - 1)
    return -(-worst // SLAB_CHUNK) * SLAB_CHUNK


def _dispatch_kernel(nch_ref, loc_ref, glob_ref, tail_ref, spt_ref, x_ref, xs_hbm, slab_sc, zero_sc, sem):
    i = pl.program_id(0)
    tm = x_ref.shape[0]
    slab_rows = slab_sc.shape[1]
    slot = lax.rem(i, 2)
    slab = slab_sc.at[slot]
    xb = x_ref[...].astype(BF16)
    sp = spt_ref[0]
    for c0 in range(0, slab_rows, SLAB_CHUNK):
        j = lax.broadcasted_iota(jnp.int32, (SLAB_CHUNK, tm), 0) + c0
        hit = j == sp[0:1, :]
        for k in range(1, TOP_K):
            hit = jnp.logical_or(hit, j == sp[k:k + 1, :])
        slab[c0:c0 + SLAB_CHUNK, :] = _dot(hit.astype(BF16), xb)

    def for_each_chunk(tile, fn):
        s = lax.rem(tile, 2)
        _for_each_segment_copy(tile, nch_ref, loc_ref, glob_ref,
                               lambda lo, go, rows: fn(_chunk_copy(slab_sc.at[s], lo, xs_hbm, go, sem.at[s], rows)))

    @pl.when(i > 0)
    def _():
        for_each_chunk(i - 1, lambda cp: cp.wait())

    for_each_chunk(i, lambda cp: cp.start())

    def for_each_tail_chunk(fn):
        def per_chunk(c, carry):
            fn(_chunk_copy(zero_sc, 0, xs_hbm, tail_ref[0] + c * SEG_ROWS, sem.at[2]))
            return carry

        lax.fori_loop(0, tail_ref[1], per_chunk, 0)

    @pl.when(i == pl.num_programs(0) - 1)
    def _():
        zero_sc[...] = jnp.zeros(zero_sc.shape, zero_sc.dtype)
        for_each_tail_chunk(lambda cp: cp.start())
        for_each_chunk(i, lambda cp: cp.wait())
        for_each_tail_chunk(lambda cp: cp.wait())


def _dispatch(n_chunks, slab_off, dest_off, tail, spt, x1, n_pad):
    t, d = x1.shape
    n_tiles, _, tm = spt.shape
    slab_rows = _slab_rows(tm)
    grid_spec = pltpu.PrefetchScalarGridSpec(
        num_scalar_prefetch=4,
        grid=(n_tiles,),
        in_specs=[pl.BlockSpec((1, SUBLANES, tm), lambda i, *_: (i, 0, 0)),
                  pl.BlockSpec((tm, d), lambda i, *_: (i, 0))],
        out_specs=pl.BlockSpec(memory_space=pl.ANY),
        scratch_shapes=[pltpu.VMEM((2, slab_rows, d), F32), pltpu.VMEM((SEG_ROWS, d), F32),
                        pltpu.SemaphoreType.DMA((3,))],
    )
    return pl.pallas_call(
        _dispatch_kernel,
        grid_spec=grid_spec,
        out_shape=jax.ShapeDtypeStruct((n_pad, d), F32),
        compiler_params=_params("arbitrary"),
        name="moe_dispatch",
    )(n_chunks, slab_off, dest_off, tail, spt, x1)


def _expert_kernel(ie_ref, ix_ref, iy_ref, lo_ref, hi_ref, xs_ref, wup_ref, bup_ref, wdn_ref, bdn_ref, ys_ref,
                   wup_sc, wdn_sc):
    w = pl.program_id(0)
    bm = xs_ref.shape[0]
    d_ff = wdn_ref.shape[1]
    valid = hi_ref[w] > lo_ref[w]
    new_expert = jnp.logical_or(w == 0, ie_ref[w] != ie_ref[jnp.maximum(w - 1, 0)])

    @pl.when(jnp.logical_and(valid, new_expert))
    def _():
        wup_sc[...] = wup_ref[0].astype(BF16)
        wdn_sc[...] = wdn_ref[0].astype(BF16)

    @pl.when(jnp.logical_and(lo_ref[w] == 0, hi_ref[w] == 0))
    def _():
        ys_ref[...] = jnp.zeros(ys_ref.shape, ys_ref.dtype)

    @pl.when(valid)
    def _():
        hb = _dot(xs_ref[...].astype(BF16), wup_sc[...]) + bup_ref[0]
        gl = jnp.minimum(hb[:, :d_ff], SWIGLU_LIMIT)
        up = jnp.clip(hb[:, d_ff:], -SWIGLU_LIMIT, SWIGLU_LIMIT)
        yb = (up + 1.0) * (gl * _sigmoid(SWIGLU_ALPHA * gl))
        y = _dot(yb.astype(BF16), wdn_sc[...]) + bdn_ref[0]
        rows = lax.broadcasted_iota(jnp.int32, (bm, 1), 0)
        mine = jnp.logical_and(rows >= lo_ref[w], rows < hi_ref[w])
        first_visit = lo_ref[w] == 0

        @pl.when(first_visit)
        def _():
            ys_ref[...] = jnp.where(mine, y, 0.0)

        @pl.when(jnp.logical_not(first_visit))
        def _():
            ys_ref[...] = jnp.where(mine, y, ys_ref[...])


def _experts(item_e, item_x, item_y, item_lo, item_hi, xs, w_up, b_up, w_down, b_down, layer, bm):
    n, d = xs.shape
    f2 = w_up.shape[3]
    n_max = item_e.shape[0]
    xblk = lambda w, ie, ix, iy, lo, hi: (ix[w], 0)
    yblk = lambda w, ie, ix, iy, lo, hi: (iy[w], 0)
    exp3 = lambda w, ie, ix, iy, lo, hi: (layer, ie[w], 0, 0)
    grid_spec = pltpu.PrefetchScalarGridSpec(
        num_scalar_prefetch=5,
        grid=(n_max,),
        in_specs=[pl.BlockSpec((bm, d), xblk),
                  pl.BlockSpec((None, 1, d, f2), exp3), pl.BlockSpec((None, 1, 1, f2), exp3),
                  pl.BlockSpec((None, 1, f2 // 2, d), exp3), pl.BlockSpec((None, 1, 1, d), exp3)],
        out_specs=pl.BlockSpec((bm, d), yblk),
        scratch_shapes=[pltpu.VMEM((d, f2), BF16), pltpu.VMEM((f2 // 2, d), BF16)],
    )
    return pl.pallas_call(
        _expert_kernel,
        grid_spec=grid_spec,
        out_shape=jax.ShapeDtypeStruct((n, d), F32),
        compiler_params=_params("arbitrary"),
        name="moe_experts",
    )(item_e, item_x, item_y, item_lo, item_hi, xs, w_up, b_up, w_down, b_down)


def _post_moe_kernel(nch_ref, loc_ref, glob_ref, x1_ref, gate_ref, spos_ref, p_ref, g_ref, b_ref, wg_ref, wp_ref,
                     ys_hbm, out_ref, slab_sc, sem, *, alpha):
    i = pl.program_id(0)
    n_tiles = pl.num_programs(0)
    tm = x1_ref.shape[0]
    slab_rows = slab_sc.shape[1]

    def for_each_chunk(tile, fn):
        s = lax.rem(tile, 2)
        _for_each_segment_copy(tile, nch_ref, loc_ref, glob_ref,
                               lambda lo, go, rows: fn(_chunk_copy(ys_hbm, go, slab_sc.at[s], lo, sem.at[s], rows)))

    @pl.when(i == 0)
    def _():
        slab_sc[...] = jnp.zeros(slab_sc.shape, slab_sc.dtype)
        for_each_chunk(i, lambda cp: cp.start())

    @pl.when(i + 1 < n_tiles)
    def _():
        for_each_chunk(i + 1, lambda cp: cp.start())

    for_each_chunk(i, lambda cp: cp.wait())
    slab = slab_sc.at[lax.rem(i, 2)]
    gates = gate_ref[...]
    spos = spos_ref[...]
    ffn = jnp.zeros(x1_ref.shape, F32)
    for c0 in range(0, slab_rows, SLAB_CHUNK):
        col = lax.broadcasted_iota(jnp.int32, (tm, SLAB_CHUNK), 1) + c0
        weight = jnp.zeros((tm, SLAB_CHUNK), F32)
        for k in range(TOP_K):
            weight = jnp.where(col == spos[:, k:k + 1], gates[:, k:k + 1], weight)
        ffn = ffn + _dot(weight.astype(BF16), slab[c0:c0 + SLAB_CHUNK, :].astype(BF16))
    x2 = _layer_norm(alpha * x1_ref[...] + ffn, g_ref[...], b_ref[...])
    gate = _sigmoid(_dot(x2.astype(BF16), wg_ref[...]))
    batch, ts = p_ref.shape[0], p_ref.shape[1]
    p_bm = p_ref[...].reshape(tm, p_ref.shape[2]).astype(BF16)
    tok = lax.broadcasted_iota(jnp.int32, (tm, tm), 0)
    src = lax.broadcasted_iota(jnp.int32, (tm, tm), 1)
    shift = batch.bit_length() - 1
    pick = src == jnp.bitwise_and(tok, batch - 1) * ts + lax.shift_right_logical(tok, shift)
    p_tok = _dot(pick.astype(BF16), p_bm).astype(BF16)
    out_ref[...] = x2 + gate * _dot(p_tok, wp_ref[...])


def _post_moe(n_chunks, slab_off, dest_off, x1, gates, spos, ys, p, layer, g, b, wg, wp, alpha, batch, tm):
    t, d = x1.shape
    ts = tm // batch
    row = lambda c: pl.BlockSpec((tm, c), lambda i, *_: (i, 0))
    grid_spec = pltpu.PrefetchScalarGridSpec(
        num_scalar_prefetch=3,
        grid=(t // tm,),
        in_specs=[row(d), row(LANES), row(LANES),
                  pl.BlockSpec((None, batch, ts, p.shape[-1]), lambda i, *_: (layer, 0, i, 0)),
                  _full(g.shape), _full(b.shape), _full(wg.shape), _full(wp.shape),
                  pl.BlockSpec(memory_space=pl.ANY)],
        out_specs=row(d),
        scratch_shapes=[pltpu.VMEM((2, _slab_rows(tm), d), F32), pltpu.SemaphoreType.DMA((2,))],
    )
    return pl.pallas_call(
        functools.partial(_post_moe_kernel, alpha=alpha),
        grid_spec=grid_spec,
        out_shape=jax.ShapeDtypeStruct((t, d), F32),
        compiler_params=_params("arbitrary"),
        name="post_moe_ple",
    )(n_chunks, slab_off, dest_off, x1, gates, spos, p, g, b, wg, wp, ys)


def _moe_plan(counts, n_pad, bm):
    i32 = jnp.int32
    n_blocks = n_pad // bm
    n_max = n_blocks + N_EXPERTS - 1
    ends = jnp.cumsum(counts)
    starts = ends - counts
    total = ends[-1]
    first_blk = starts // bm
    n_items_e = jnp.where(counts > 0, (ends - 1) // bm - first_blk + 1, 0)
    item_end = jnp.cumsum(n_items_e)
    item_start = item_end - n_items_e
    n_compute = item_end[-1]
    n_used = (total + bm - 1) // bm
    n_active = n_compute + (n_blocks - n_used)
    w_all = jnp.arange(n_max, dtype=i32)
    w = jnp.minimum(w_all, n_active - 1)
    is_compute = w < n_compute
    wc = jnp.minimum(w, n_compute - 1)
    item_e = jnp.sum((wc[:, None] >= item_end[None, :]).astype(i32), axis=1)
    blk_c = first_blk[item_e] + wc - item_start[item_e]
    item_y = jnp.where(is_compute, blk_c, n_used + (w - n_compute)).astype(i32)
    item_x = jnp.where(is_compute, blk_c, 0).astype(i32)
    lo_c = jnp.maximum(starts[item_e] - blk_c * bm, 0)
    hi_c = jnp.minimum(ends[item_e] - blk_c * bm, bm)
    item_lo = jnp.where(w_all >= n_active, -1, jnp.where(is_compute, lo_c, 0)).astype(i32)
    item_hi = jnp.where(w_all >= n_active, -1, jnp.where(is_compute, hi_c, 0)).astype(i32)
    tail = jnp.stack([total, (n_pad - total) // SEG_ROWS]).astype(i32)
    return starts, tail, item_e.astype(i32), item_x, item_y, item_lo, item_hi


def _moe_dispatch(x1, logits, w_up, b_up, w_down, b_down, layer, tm, bm):
    t, d = x1.shape
    n_tiles = t // tm
    worst = t * TOP_K + n_tiles * N_EXPERTS * (SEG_ROWS - 1)
    n_pad = -(-worst // bm) * bm
    gates, spos, spt, seg, counts = _route(logits, tm)
    starts, tail, item_e, item_x, item_y, item_lo, item_hi = _moe_plan(counts[0, :N_EXPERTS], n_pad, bm)
    n_chunks = seg[:, 0, :N_EXPERTS].reshape(-1)
    slab_off = seg[:, 1, :N_EXPERTS].reshape(-1)
    dest_off = (starts[None, :] + seg[:, 2, :N_EXPERTS]).reshape(-1).astype(jnp.int32)
    xs = _dispatch(n_chunks, slab_off, dest_off, tail, spt, x1, n_pad)
    ys = _experts(item_e, item_x, item_y, item_lo, item_hi, xs, w_up, b_up, w_down, b_down, layer, bm)
    return (n_chunks, slab_off, dest_off), gates, spos, ys


def _pick_tile(n, target, quantum):
    tile = min(n, target)
    while n % tile or tile % quantum:
        tile -= quantum
    return tile


def kernel(x, p, positions, mla_w_in, mla_q_norm, mla_kv_norm, mla_w_uq, mla_w_ukv, mla_w_o, lru_w_in, lru_conv_w, lru_conv_b, lru_w_a, lru_b_a, lru_w_x, lru_b_x, lru_lambda, lru_w_out, ln1_g, ln1_b, ln2_g, ln2_b, moe_w_router, moe_b_router, moe_w_up, moe_b_up, moe_w_down, moe_b_down, ple_w_gate, ple_w_proj):
    batch, seq, d = x.shape
    depth = ln1_g.shape[0]
    t = batch * seq
    assert batch == SUBLANES, "the recurrence kernel maps the batch onto the sublanes of a vreg"
    alpha = (2.0 * depth) ** 0.25
    tm = _pick_tile(t, 512, SUBLANES * batch)
    attn_blk = _pick_tile(seq, 512, LANES)
    moe_bm = _pick_tile(t * TOP_K, 512, SUBLANES)

    xt = x
    half = QK_ROPE_DIM // 2
    inv_freq = jnp.exp(-math.log(ROPE_THETA) * jnp.arange(half, dtype=F32) / half)
    rope_freq = jnp.concatenate([inv_freq, inv_freq, jnp.zeros((LANES - QK_ROPE_DIM,), F32)]).reshape(1, LANES)
    pos_lanes = jnp.broadcast_to(jnp.transpose(positions).astype(F32)[:, :, None],
                                 (seq, batch, LANES)).reshape(seq, batch * LANES)

    row2 = lambda a: a.reshape(1, -1)
    scale = math.log2(math.e) / math.sqrt(QK_NOPE_DIM + QK_ROPE_DIM)

    for layer in range(depth):
        j = layer // 2
        wr = jnp.pad(moe_w_router[layer], ((0, 0), (0, LANES - N_EXPERTS)))
        wr_hi, wr_lo = _split_bf16(wr)
        br = row2(jnp.pad(moe_b_router[layer], (0, LANES - N_EXPERTS)))
        g1, b1 = row2(ln1_g[layer]), row2(ln1_b[layer])
        if layer % 2 == 0:
            q_lora, kv_lora = mla_q_norm.shape[1], mla_kv_norm.shape[1]
            w_in = mla_w_in[j]
            win_p = jnp.pad(w_in, ((0, 0), (0, LANES - QK_ROPE_DIM))).astype(BF16)
            wuq = mla_w_uq[j].reshape(q_lora, MLA_HEADS, QK_NOPE_DIM + QK_ROPE_DIM)
            wuq_p = jnp.pad(wuq, ((0, 0), (0, 0), (0, QK_PAD - QK_NOPE_DIM - QK_ROPE_DIM)))
            wuq_p = wuq_p.reshape(q_lora, MLA_HEADS * QK_PAD).astype(BF16)
            q, k, vt = _mla_proj(xt, pos_lanes, rope_freq, win_p,
                                 row2(mla_q_norm[j] * scale), row2(mla_kv_norm[j]),
                                 wuq_p, mla_w_ukv[j].astype(BF16), seq, batch, attn_blk)
            o = _attention(q, k, vt, seq, batch, attn_blk, 2)
            x1, logits = _post_attn(xt, o, mla_w_o[j].astype(BF16), g1, b1,
                                    wr_hi, wr_lo, br, alpha, seq, batch, attn_blk)
        else:
            x1, logits = _lru_block(
                xt.reshape(t, d), lru_w_in[j].astype(BF16), lru_conv_w[j], row2(lru_conv_b[j]),
                lru_w_a[j].astype(BF16), row2(lru_b_a[j]), lru_w_x[j].astype(BF16), row2(lru_b_x[j]),
                row2(lru_lambda[j]), lru_w_out[j].astype(BF16), g1, b1, wr_hi, wr_lo, br, alpha, batch, tm)
        x1 = x1.reshape(t, d)
        tables, gates, spos, ys = _moe_dispatch(x1, logits.reshape(t, LANES), moe_w_up,
                                                moe_b_up[:, :, None, :], moe_w_down, moe_b_down[:, :, None, :],
                                                layer, tm, moe_bm)
        xt = _post_moe(*tables, x1, gates, spos, ys, p, layer, row2(ln2_g[layer]), row2(ln2_b[layer]),
                       ple_w_gate[layer].astype(BF16), ple_w_proj[layer].astype(BF16), alpha, batch, tm)
        if layer + 1 < depth and (layer + 1) % 2 == 0:
            xt = xt.reshape(seq, batch * d)
    return jnp.transpose(xt.reshape(seq, batch, d), (1, 0, 2))
```

```python
import functools
import math

import jax
import jax.numpy as jnp
from jax import lax
from jax.experimental import pallas as pl
from jax.experimental.pallas import tpu as pltpu

F32 = jnp.float32
BF16 = jnp.bfloat16

MLA_HEADS = 8
QK_NOPE_DIM = 128
QK_ROPE_DIM = 64
V_HEAD_DIM = 128
ROPE_THETA = 10000.0
LRU_BLOCKS = 4
CONV_WIDTH = 4
LRU_C = 8.0
N_EXPERTS = 32
TOP_K = 4
SWIGLU_LIMIT = 7.0
SWIGLU_ALPHA = 1.702
LN_EPS = 1e-5
RMS_EPS = 1e-6

LANES = 128
SUBLANES = 8
QK_PAD = 2 * LANES
VMEM_LIMIT_BYTES = 56 * 1024 * 1024


def _params(*sem, flags=None):
    return pltpu.CompilerParams(dimension_semantics=sem, vmem_limit_bytes=VMEM_LIMIT_BYTES, flags=flags)


def _dot(a, b):
    return jnp.dot(a, b, preferred_element_type=F32)


def _sigmoid(z):
    return 1.0 / (1.0 + jnp.exp(-z))


def _layer_norm(y, g, b):
    mu = jnp.mean(y, axis=-1, keepdims=True)
    yc = y - mu
    var = jnp.mean(yc * yc, axis=-1, keepdims=True)
    return yc * lax.rsqrt(var + LN_EPS) * g + b


def _rms_norm(y, g):
    ms = jnp.mean(y * y, axis=-1, keepdims=True)
    return y * lax.rsqrt(ms + RMS_EPS) * g


def _split_bf16(a):
    hi = a.astype(BF16)
    lo = (a - hi.astype(F32)).astype(BF16)
    return hi, lo


def _router_logits(x1, wr_hi, wr_lo, br):
    hi, lo = _split_bf16(x1)
    return _dot(hi, wr_hi) + (_dot(hi, wr_lo) + _dot(lo, wr_hi)) + br


def _full(shape):
    return pl.BlockSpec(shape, lambda *_: (0,) * len(shape))


def _rope(blk, cos, sin_a, sin_b):
    return (blk * cos + pltpu.roll(blk, LANES - QK_ROPE_DIM // 2, 1) * sin_a
            + pltpu.roll(blk, QK_ROPE_DIM // 2, 1) * sin_b)


def _mla_proj_kernel(x_ref, pos_ref, freq_ref, win_ref, qn_ref, kvn_ref, wuq_ref, wukv_ref,
                     q_ref, k_ref, vt_ref, *, q_lora, kv_lora):
    lat = _dot(x_ref[...].astype(BF16), win_ref[...])
    c_q = _rms_norm(lat[:, :q_lora], qn_ref[...])
    c_kv = _rms_norm(lat[:, q_lora:q_lora + kv_lora], kvn_ref[...])
    ang = pos_ref[...] * freq_ref[...]
    lane = lax.broadcasted_iota(jnp.int32, ang.shape, 1)
    cos, sin = jnp.cos(ang), jnp.sin(ang)
    half = QK_ROPE_DIM // 2
    sin_a = jnp.where(lane < half, -sin, 0.0)
    sin_b = jnp.where(jnp.logical_and(lane >= half, lane < QK_ROPE_DIM), sin, 0.0)
    k_rope = _rope(lat[:, q_lora + kv_lora:], cos, sin_a, sin_b).astype(BF16)
    q = _dot(c_q.astype(BF16), wuq_ref[...])
    kv = _dot(c_kv.astype(BF16), wukv_ref[...])
    for h in range(MLA_HEADS):
        lo = h * QK_PAD
        mid = lo + LANES
        hi = lo + QK_PAD
        q_ref[:, lo:mid] = q[:, lo:mid].astype(BF16)
        q_ref[:, mid:hi] = _rope(q[:, mid:hi], cos, sin_a, sin_b).astype(BF16)
        k_ref[:, lo:mid] = kv[:, lo:mid].astype(BF16)
        k_ref[:, mid:hi] = k_rope
        vt_ref[h * V_HEAD_DIM:(h + 1) * V_HEAD_DIM, :] = jnp.transpose(kv[:, mid:hi]).astype(BF16)


def _col_spec(ts, c):
    return pl.BlockSpec((ts, c), lambda i, b: (i, b))


def _x_spec(x, ts, d):
    if x.ndim == 3:
        return pl.BlockSpec((None, ts, d), lambda i, b: (b, i, 0))
    return _col_spec(ts, d)


def _mla_proj(x, pos, freq, win_p, qn, kvn, wuq_p, wukv, seq, batch, ts):
    d = win_p.shape[0]
    q_lora, kv_lora = qn.shape[1], kvn.shape[1]
    hq = MLA_HEADS * QK_PAD
    hv = MLA_HEADS * V_HEAD_DIM
    col = functools.partial(_col_spec, ts)
    return pl.pallas_call(
        functools.partial(_mla_proj_kernel, q_lora=q_lora, kv_lora=kv_lora),
        grid=(seq // ts, batch),
        in_specs=[_x_spec(x, ts, d), col(LANES), _full(freq.shape), _full(win_p.shape), _full(qn.shape),
                  _full(kvn.shape), _full(wuq_p.shape), _full(wukv.shape)],
        out_specs=[col(hq), col(hq), pl.BlockSpec((hv, ts), lambda i, b: (b, i))],
        out_shape=[jax.ShapeDtypeStruct((seq, batch * hq), BF16), jax.ShapeDtypeStruct((seq, batch * hq), BF16),
                   jax.ShapeDtypeStruct((batch * hv, seq), BF16)],
        compiler_params=_params("parallel", "parallel"),
        name="mla_proj",
    )(x, pos, freq, win_p, qn, kvn, wuq_p, wukv)


ONES_ROWS = 16


def _attn_kernel(q_ref, k_ref, vt_ref, o_ref, m_sc, acc_sc, st_a, st_b, *, blk, heads):
    qi = pl.program_id(1)
    m_sc[...] = jnp.full(m_sc.shape, -jnp.inf, F32)
    acc_sc[...] = jnp.zeros(acc_sc.shape, F32)
    ones = jnp.ones((ONES_ROWS, blk), BF16)

    def scores(ki, st_ref):
        start = pl.multiple_of(ki * blk, blk)
        for h in range(heads):
            k = k_ref[pl.ds(start, blk), h * QK_PAD:(h + 1) * QK_PAD]
            st_ref[h] = lax.dot_general(k, q_ref[:, h * QK_PAD:(h + 1) * QK_PAD], (((1,), (1,)), ((), ())),
                                        preferred_element_type=F32)

    def consume(ki, st_ref, diagonal):
        start = pl.multiple_of(ki * blk, blk)
        for h in range(heads):
            vt = vt_ref[h * V_HEAD_DIM:(h + 1) * V_HEAD_DIM, pl.ds(start, blk)]
            vt = jnp.concatenate([vt, ones], axis=0)
            st = st_ref[h]
            if diagonal:
                keys = lax.broadcasted_iota(jnp.int32, st.shape, 0)
                queries = lax.broadcasted_iota(jnp.int32, st.shape, 1)
                st = jnp.where(keys <= queries, st, -jnp.inf)
            m_prev = m_sc[h]
            m_new = jnp.maximum(m_prev, jnp.max(st, axis=0, keepdims=True))
            alpha = jnp.exp2(m_prev - m_new)
            p = jnp.exp2(st - m_new).astype(BF16)
            acc_sc[h] = alpha * acc_sc[h] + _dot(vt, p)
            m_sc[h] = m_new

    pairs = qi // 2
    scores(0, st_a)

    def body(j, carry):
        scores(2 * j + 1, st_b)
        consume(2 * j, st_a, False)
        scores(2 * j + 2, st_a)
        consume(2 * j + 1, st_b, False)
        return carry

    lax.fori_loop(0, pairs, body, 0)

    @pl.when(qi == 2 * pairs)
    def _():
        consume(qi, st_a, True)

    @pl.when(qi != 2 * pairs)
    def _():
        scores(qi, st_b)
        consume(qi - 1, st_a, False)
        consume(qi, st_b, True)

    for h in range(heads):
        acc = acc_sc[h]
        out_t = acc[:V_HEAD_DIM, :] / acc[V_HEAD_DIM:V_HEAD_DIM + 1, :]
        o_ref[:, h * V_HEAD_DIM:(h + 1) * V_HEAD_DIM] = jnp.transpose(out_t).astype(o_ref.dtype)


def _attention(q, k, vt, seq, batch, blk, heads):
    groups = batch * MLA_HEADS // heads
    return pl.pallas_call(
        functools.partial(_attn_kernel, blk=blk, heads=heads),
        grid=(groups, seq // blk),
        in_specs=[pl.BlockSpec((blk, heads * QK_PAD), lambda g, i: (i, g)),
                  pl.BlockSpec((seq, heads * QK_PAD), lambda g, i: (0, g)),
                  pl.BlockSpec((heads * V_HEAD_DIM, seq), lambda g, i: (g, 0))],
        out_specs=pl.BlockSpec((blk, heads * V_HEAD_DIM), lambda g, i: (i, g)),
        out_shape=jax.ShapeDtypeStruct((seq, batch * MLA_HEADS * V_HEAD_DIM), BF16),
        scratch_shapes=[pltpu.VMEM((heads, 1, blk), F32),
                        pltpu.VMEM((heads, V_HEAD_DIM + ONES_ROWS, blk), F32),
                        pltpu.VMEM((heads, blk, blk), F32), pltpu.VMEM((heads, blk, blk), F32)],
        compiler_params=_params("parallel", "arbitrary"),
        name="mla_attention",
    )(q, k, vt)


def _mix_epilogue(x, mix, g_ref, b_ref, wrh_ref, wrl_ref, br_ref, x1_ref, lg_ref, alpha):
    x1 = _layer_norm(alpha * x + mix, g_ref[...], b_ref[...])
    x1_ref[...] = x1
    lg_ref[...] = _router_logits(x1, wrh_ref[...], wrl_ref[...], br_ref[...])


def _post_attn_kernel(x_ref, o_ref, wo_ref, g_ref, b_ref, wrh_ref, wrl_ref, br_ref, x1_ref, lg_ref, *, alpha):
    mix = _dot(o_ref[...], wo_ref[...])
    _mix_epilogue(x_ref[...], mix, g_ref, b_ref, wrh_ref, wrl_ref, br_ref, x1_ref, lg_ref, alpha)


def _post_attn(x, o, wo, g, b, wr_hi, wr_lo, br, alpha, seq, batch, ts):
    hv, d = wo.shape
    col = functools.partial(_col_spec, ts)
    return pl.pallas_call(
        functools.partial(_post_attn_kernel, alpha=alpha),
        grid=(seq // ts, batch),
        in_specs=[_x_spec(x, ts, d), col(hv), _full(wo.shape), _full(g.shape), _full(b.shape),
                  _full(wr_hi.shape), _full(wr_lo.shape), _full(br.shape)],
        out_specs=[col(d), col(LANES)],
        out_shape=[jax.ShapeDtypeStruct((seq, batch * d), F32), jax.ShapeDtypeStruct((seq, batch * LANES), F32)],
        compiler_params=_params("parallel", "parallel"),
        name="post_attention",
    )(x, o, wo, g, b, wr_hi, wr_lo, br)


def _lru_kernel(x_ref, win_ref, cw_ref, cb_ref, wa_ref, ba_ref, wx_ref, bx_ref, lam_ref, wout_ref,
                g_ref, b_ref, wrh_ref, wrl_ref, br_ref, x1_ref, lg_ref,
                ucarry_sc, hcarry_sc, a_sc, b_sc, h_sc, *, alpha, batch):
    tm, width = a_sc.shape
    halo = (CONV_WIDTH - 1) * batch
    blk_w = width // LRU_BLOCKS

    @pl.when(pl.program_id(0) == 0)
    def _():
        ucarry_sc[...] = jnp.zeros(ucarry_sc.shape, F32)
        hcarry_sc[...] = jnp.zeros(hcarry_sc.shape, F32)

    x = x_ref[...]
    gu = _dot(x.astype(BF16), win_ref[...])
    gate = gu[:, :width]
    u = gu[:, width:]
    u_ext = jnp.concatenate([ucarry_sc[...], u], axis=0)
    ucarry_sc[...] = u[tm - halo:, :]
    cw = cw_ref[...]
    uc = cb_ref[...] + cw[0:1, :] * u_ext[0:tm, :]
    for j in range(1, CONV_WIDTH):
        uc = uc + cw[j:j + 1, :] * u_ext[j * batch:j * batch + tm, :]
    ucb = uc.astype(BF16)
    ra = jnp.concatenate([_dot(ucb[:, n * blk_w:(n + 1) * blk_w], wa_ref[n]) for n in range(LRU_BLOCKS)], axis=1)
    rx = jnp.concatenate([_dot(ucb[:, n * blk_w:(n + 1) * blk_w], wx_ref[n]) for n in range(LRU_BLOCKS)], axis=1)
    r = _sigmoid(ra + ba_ref[...])
    gi = _sigmoid(rx + bx_ref[...])
    z = -lam_ref[...]
    softplus = jnp.maximum(z, 0.0) + jnp.log1p(jnp.exp(-jnp.abs(z)))
    log_a = (-LRU_C) * r * softplus
    a_sc[...] = jnp.exp(log_a)
    th = jnp.tanh(log_a)
    b_sc[...] = jnp.sqrt(-2.0 * th / (1.0 - th)) * gi * uc

    def body(t, h):
        rows = pl.ds(pl.multiple_of(t * batch, batch), batch)
        h = a_sc[rows, :] * h + b_sc[rows, :]
        h_sc[rows, :] = h
        return h

    hcarry_sc[...] = lax.fori_loop(0, tm // batch, body, hcarry_sc[...], unroll=8)
    c = math.sqrt(2.0 / math.pi)
    gelu = 0.5 * gate * (1.0 + jnp.tanh(c * (gate + 0.044715 * (gate * gate * gate))))
    y = gelu * h_sc[...]
    mix = _dot(y.astype(BF16), wout_ref[...])
    _mix_epilogue(x, mix, g_ref, b_ref, wrh_ref, wrl_ref, br_ref, x1_ref, lg_ref, alpha)


def _lru_block(x, win, cw, cb, wa, ba, wx, bx, lam, wout, g, b, wr_hi, wr_lo, br, alpha, batch, tm):
    t, d = x.shape
    width = wout.shape[0]
    row = lambda c: pl.BlockSpec((tm, c), lambda i: (i, 0))
    consts = (win, cw, cb, wa, ba, wx, bx, lam, wout, g, b, wr_hi, wr_lo, br)
    return pl.pallas_call(
        functools.partial(_lru_kernel, alpha=alpha, batch=batch),
        grid=(t // tm,),
        in_specs=[row(d)] + [_full(c.shape) for c in consts],
        out_specs=[row(d), row(LANES)],
        out_shape=[jax.ShapeDtypeStruct((t, d), F32), jax.ShapeDtypeStruct((t, LANES), F32)],
        scratch_shapes=[pltpu.VMEM(((CONV_WIDTH - 1) * batch, width), F32), pltpu.VMEM((batch, width), F32),
                        pltpu.VMEM((tm, width), F32), pltpu.VMEM((tm, width), F32), pltpu.VMEM((tm, width), F32)],
        compiler_params=_params("arbitrary"),
        name="rglru_block",
    )(x, *consts)


SEG_ROWS = SUBLANES


def _route_kernel(lg_ref, gate_ref, spos_ref, spt_ref, seg_ref, cnt_ref, run_sc):
    tm = lg_ref.shape[0]

    @pl.when(pl.program_id(0) == 0)
    def _():
        run_sc[...] = jnp.zeros(run_sc.shape, F32)

    lane = lax.broadcasted_iota(jnp.int32, (tm, LANES), 1)
    lane_f = lane.astype(F32)
    work = jnp.where(lane < N_EXPERTS, lg_ref[...], -jnp.inf)
    tops, onehots = [], []
    for k in range(TOP_K):
        top = jnp.max(work, axis=1, keepdims=True)
        idx = jnp.min(jnp.where(work == top, lane_f, float(LANES)), axis=1, keepdims=True)
        hot = lane_f == idx
        work = jnp.where(hot, -jnp.inf, work)
        tops.append(top)
        onehots.append(hot)
    exps = [jnp.exp(top - tops[0]) for top in tops]
    denom = exps[0]
    for e in exps[1:]:
        denom = denom + e
    gate = jnp.zeros((tm, LANES), F32)
    for k in range(TOP_K):
        gate = jnp.where(lane == k, exps[k] / denom, gate)
    hits = onehots[0].astype(F32)
    for hot in onehots[1:]:
        hits = hits + hot.astype(F32)
    earlier = (lax.broadcasted_iota(jnp.int32, (tm, tm), 1) < lax.broadcasted_iota(jnp.int32, (tm, tm), 0))
    before = _dot(earlier.astype(BF16), hits.astype(BF16))
    count = jnp.sum(hits, axis=0, keepdims=True)
    chunks = jnp.floor((count + (SEG_ROWS - 1.0)) * (1.0 / SEG_ROWS))
    lower = (lax.broadcasted_iota(jnp.int32, (LANES, LANES), 0) < lax.broadcasted_iota(jnp.int32, (LANES, LANES), 1))
    chunks8 = jnp.broadcast_to(chunks, (SUBLANES, LANES))
    slab_off = _dot(chunks8.astype(BF16), lower.astype(BF16))[0:1, :] * SEG_ROWS
    run = run_sc[...]
    spos = jnp.zeros((tm, LANES), F32)
    for k in range(TOP_K):
        s_k = jnp.sum(jnp.where(onehots[k], before + slab_off, 0.0), axis=1, keepdims=True)
        spos = jnp.where(lane == k, s_k, spos)
    run_sc[...] = run + chunks * SEG_ROWS
    gate_ref[...] = gate
    spos_ref[...] = spos.astype(jnp.int32)
    spt_ref[0] = jnp.transpose(spos)[0:SUBLANES, :].astype(jnp.int32)
    row = lax.broadcasted_iota(jnp.int32, (SUBLANES, LANES), 0)
    seg = jnp.where(row == 0, chunks, jnp.where(row == 1, slab_off, jnp.where(row == 2, run, 0.0)))
    seg_ref[0] = seg.astype(jnp.int32)
    cnt_ref[...] = run_sc[...].astype(jnp.int32)


def _route(logits, tm):
    t = logits.shape[0]
    n_tiles = t // tm
    row = pl.BlockSpec((tm, LANES), lambda i: (i, 0))
    return pl.pallas_call(
        _route_kernel,
        grid=(n_tiles,),
        in_specs=[row],
        out_specs=[row, row, pl.BlockSpec((1, SUBLANES, tm), lambda i: (i, 0, 0)),
                   pl.BlockSpec((1, SUBLANES, LANES), lambda i: (i, 0, 0)),
                   pl.BlockSpec((1, LANES), lambda i: (0, 0))],
        out_shape=[jax.ShapeDtypeStruct((t, LANES), F32), jax.ShapeDtypeStruct((t, LANES), jnp.int32),
                   jax.ShapeDtypeStruct((n_tiles, SUBLANES, tm), jnp.int32),
                   jax.ShapeDtypeStruct((n_tiles, SUBLANES, LANES), jnp.int32),
                   jax.ShapeDtypeStruct((1, LANES), jnp.int32)],
        scratch_shapes=[pltpu.VMEM((1, LANES), F32)],
        compiler_params=_params("arbitrary"),
        name="moe_route",
    )(logits)


BIG_COPY_CHUNKS = 4


def _chunk_copy(src, src_row, dst, dst_row, sem, rows=SEG_ROWS):
    return pltpu.make_async_copy(src.at[pl.ds(pl.multiple_of(src_row, SEG_ROWS), rows)],
                                 dst.at[pl.ds(pl.multiple_of(dst_row, SEG_ROWS), rows)], sem)


def _for_each_copy(n_seg, seg_of, fn):
    big = BIG_COPY_CHUNKS * SEG_ROWS
    shift = BIG_COPY_CHUNKS.bit_length() - 1

    def per_expert(e, carry):
        lo, go, n = seg_of(e)
        n_big = lax.shift_right_logical(n, shift)

        def big_copy(c, carry2):
            fn(lo + c * big, go + c * big, big)
            return carry2

        def small_copy(c, carry2):
            fn(lo + n_big * big + c * SEG_ROWS, go + n_big * big + c * SEG_ROWS, SEG_ROWS)
            return carry2

        lax.fori_loop(0, n_big, big_copy, 0)
        lax.fori_loop(0, n - n_big * BIG_COPY_CHUNKS, small_copy, 0)
        return carry

    lax.fori_loop(0, n_seg, per_expert, 0)


def _for_each_segment_copy(tile, nch_ref, loc_ref, glob_ref, fn):
    def seg_of(e):
        s = tile * N_EXPERTS + e
        return loc_ref[s], glob_ref[s], nch_ref[s]

    _for_each_copy(N_EXPERTS, seg_of, fn)


SLAB_CHUNK = 256


def _slab_rows(tm):
    worst = tm * TOP_K + N_EXPERTS * (SEG_ROWS - 1)
    return -(-worst // SLAB_CHUNK) * SLAB_CHUNK


def _dispatch_kernel(nch_ref, loc_ref, glob_ref, gap_ref, gapn_ref, spt_ref, x_ref, xs_hbm, slab_sc, zero_sc, sem):
    i = pl.program_id(0)
    tm = x_ref.shape[0]
    slab_rows = slab_sc.shape[1]
    slot = lax.rem(i, 2)
    slab = slab_sc.at[slot]
    xb = x_ref[...].astype(BF16)
    sp = spt_ref[0]
    for c0 in range(0, slab_rows, SLAB_CHUNK):
        j = lax.broadcasted_iota(jnp.int32, (SLAB_CHUNK, tm), 0) + c0
        hit = j == sp[0:1, :]
        for k in range(1, TOP_K):
            hit = jnp.logical_or(hit, j == sp[k:k + 1, :])
        slab[c0:c0 + SLAB_CHUNK, :] = _dot(hit.astype(BF16), xb)

    def for_each_chunk(tile, fn):
        s = lax.rem(tile, 2)
        _for_each_segment_copy(tile, nch_ref, loc_ref, glob_ref,
                               lambda lo, go, rows: fn(_chunk_copy(slab_sc.at[s], lo, xs_hbm, go, sem.at[s], rows)))

    @pl.when(i > 0)
    def _():
        for_each_chunk(i - 1, lambda cp: cp.wait())

    for_each_chunk(i, lambda cp: cp.start())

    def for_each_gap_copy(fn):
        _for_each_copy(gap_ref.shape[0], lambda e: (0, gap_ref[e], gapn_ref[e]),
                       lambda lo, go, rows: fn(_chunk_copy(zero_sc, 0, xs_hbm, go, sem.at[2], rows)))

    @pl.when(i == pl.num_programs(0) - 1)
    def _():
        zero_sc[...] = jnp.zeros(zero_sc.shape, zero_sc.dtype)
        for_each_gap_copy(lambda cp: cp.start())
        for_each_chunk(i, lambda cp: cp.wait())
        for_each_gap_copy(lambda cp: cp.wait())


def _dispatch(n_chunks, slab_off, dest_off, gap_start, gap_chunks, spt, x1, n_pad):
    t, d = x1.shape
    n_tiles, _, tm = spt.shape
    slab_rows = _slab_rows(tm)
    grid_spec = pltpu.PrefetchScalarGridSpec(
        num_scalar_prefetch=5,
        grid=(n_tiles,),
        in_specs=[pl.BlockSpec((1, SUBLANES, tm), lambda i, *_: (i, 0, 0)),
                  pl.BlockSpec((tm, d), lambda i, *_: (i, 0))],
        out_specs=pl.BlockSpec(memory_space=pl.ANY),
        scratch_shapes=[pltpu.VMEM((2, slab_rows, d), F32), pltpu.VMEM((BIG_COPY_CHUNKS * SEG_ROWS, d), F32),
                        pltpu.SemaphoreType.DMA((3,))],
    )
    return pl.pallas_call(
        _dispatch_kernel,
        grid_spec=grid_spec,
        out_shape=jax.ShapeDtypeStruct((n_pad, d), F32),
        compiler_params=_params("arbitrary"),
        name="moe_dispatch",
    )(n_chunks, slab_off, dest_off, gap_start, gap_chunks, spt, x1)


def _expert_kernel(ie_ref, ix_ref, xs_ref, wup_ref, bup_ref, wdn_ref, bdn_ref, ys_ref, wup_sc, wdn_sc):
    w = pl.program_id(0)
    d_ff = wdn_ref.shape[1]
    compute = ix_ref[w] == w
    new_expert = jnp.logical_or(w == 0, ie_ref[w] != ie_ref[jnp.maximum(w - 1, 0)])

    @pl.when(jnp.logical_and(compute, new_expert))
    def _():
        wup_sc[...] = wup_ref[0].astype(BF16)
        wdn_sc[...] = wdn_ref[0].astype(BF16)

    @pl.when(jnp.logical_not(compute))
    def _():
        ys_ref[...] = jnp.zeros(ys_ref.shape, ys_ref.dtype)

    @pl.when(compute)
    def _():
        hb = _dot(xs_ref[...].astype(BF16), wup_sc[...]) + bup_ref[0]
        gl = jnp.minimum(hb[:, :d_ff], SWIGLU_LIMIT)
        up = jnp.clip(hb[:, d_ff:], -SWIGLU_LIMIT, SWIGLU_LIMIT)
        yb = (up + 1.0) * (gl * _sigmoid(SWIGLU_ALPHA * gl))
        ys_ref[...] = _dot(yb.astype(BF16), wdn_sc[...]) + bdn_ref[0]


def _experts(item_e, item_x, xs, w_up, b_up, w_down, b_down, layer, bm):
    n, d = xs.shape
    f2 = w_up.shape[3]
    xblk = lambda w, ie, ix: (ix[w], 0)
    exp3 = lambda w, ie, ix: (layer, ie[w], 0, 0)
    grid_spec = pltpu.PrefetchScalarGridSpec(
        num_scalar_prefetch=2,
        grid=(n // bm,),
        in_specs=[pl.BlockSpec((bm, d), xblk),
                  pl.BlockSpec((None, 1, d, f2), exp3), pl.BlockSpec((None, 1, 1, f2), exp3),
                  pl.BlockSpec((None, 1, f2 // 2, d), exp3), pl.BlockSpec((None, 1, 1, d), exp3)],
        out_specs=pl.BlockSpec((bm, d), lambda w, ie, ix: (w, 0)),
        scratch_shapes=[pltpu.VMEM((d, f2), BF16), pltpu.VMEM((f2 // 2, d), BF16)],
    )
    return pl.pallas_call(
        _expert_kernel,
        grid_spec=grid_spec,
        out_shape=jax.ShapeDtypeStruct((n, d), F32),
        compiler_params=_params("arbitrary"),
        name="moe_experts",
    )(item_e, item_x, xs, w_up, b_up, w_down, b_down)


def _post_moe_kernel(nch_ref, loc_ref, glob_ref, x1_ref, gate_ref, spos_ref, p_ref, g_ref, b_ref, wg_ref, wp_ref,
                     ys_hbm, out_ref, slab_sc, sem, *, alpha):
    i = pl.program_id(0)
    n_tiles = pl.num_programs(0)
    tm = x1_ref.shape[0]
    slab_rows = slab_sc.shape[1]

    def for_each_chunk(tile, fn):
        s = lax.rem(tile, 2)
        _for_each_segment_copy(tile, nch_ref, loc_ref, glob_ref,
                               lambda lo, go, rows: fn(_chunk_copy(ys_hbm, go, slab_sc.at[s], lo, sem.at[s], rows)))

    @pl.when(i == 0)
    def _():
        slab_sc[...] = jnp.zeros(slab_sc.shape, slab_sc.dtype)
        for_each_chunk(i, lambda cp: cp.start())

    @pl.when(i + 1 < n_tiles)
    def _():
        for_each_chunk(i + 1, lambda cp: cp.start())

    for_each_chunk(i, lambda cp: cp.wait())
    slab = slab_sc.at[lax.rem(i, 2)]
    gates = gate_ref[...]
    spos = spos_ref[...]
    ffn = jnp.zeros(x1_ref.shape, F32)
    for c0 in range(0, slab_rows, SLAB_CHUNK):
        col = lax.broadcasted_iota(jnp.int32, (tm, SLAB_CHUNK), 1) + c0
        weight = jnp.zeros((tm, SLAB_CHUNK), F32)
        for k in range(TOP_K):
            weight = jnp.where(col == spos[:, k:k + 1], gates[:, k:k + 1], weight)
        ffn = ffn + _dot(weight.astype(BF16), slab[c0:c0 + SLAB_CHUNK, :].astype(BF16))
    x2 = _layer_norm(alpha * x1_ref[...] + ffn, g_ref[...], b_ref[...])
    gate = _sigmoid(_dot(x2.astype(BF16), wg_ref[...]))
    batch, ts = p_ref.shape[0], p_ref.shape[1]
    p_bm = p_ref[...].reshape(tm, p_ref.shape[2]).astype(BF16)
    tok = lax.broadcasted_iota(jnp.int32, (tm, tm), 0)
    src = lax.broadcasted_iota(jnp.int32, (tm, tm), 1)
    shift = batch.bit_length() - 1
    pick = src == jnp.bitwise_and(tok, batch - 1) * ts + lax.shift_right_logical(tok, shift)
    p_tok = _dot(pick.astype(BF16), p_bm).astype(BF16)
    out_ref[...] = x2 + gate * _dot(p_tok, wp_ref[...])


def _post_moe(n_chunks, slab_off, dest_off, x1, gates, spos, ys, p, layer, g, b, wg, wp, alpha, batch, tm):
    t, d = x1.shape
    ts = tm // batch
    row = lambda c: pl.BlockSpec((tm, c), lambda i, *_: (i, 0))
    grid_spec = pltpu.PrefetchScalarGridSpec(
        num_scalar_prefetch=3,
        grid=(t // tm,),
        in_specs=[row(d), row(LANES), row(LANES),
                  pl.BlockSpec((None, batch, ts, p.shape[-1]), lambda i, *_: (layer, 0, i, 0)),
                  _full(g.shape), _full(b.shape), _full(wg.shape), _full(wp.shape),
                  pl.BlockSpec(memory_space=pl.ANY)],
        out_specs=row(d),
        scratch_shapes=[pltpu.VMEM((2, _slab_rows(tm), d), F32), pltpu.SemaphoreType.DMA((2,))],
    )
    return pl.pallas_call(
        functools.partial(_post_moe_kernel, alpha=alpha),
        grid_spec=grid_spec,
        out_shape=jax.ShapeDtypeStruct((t, d), F32),
        compiler_params=_params("arbitrary"),
        name="post_moe_ple",
    )(n_chunks, slab_off, dest_off, x1, gates, spos, p, g, b, wg, wp, ys)


def _moe_plan(counts, n_pad, bm):
    i32 = jnp.int32
    n_blocks = n_pad // bm
    region = (counts + bm - 1) // bm * bm
    region_end = jnp.cumsum(region)
    starts = region_end - region
    total = region_end[-1]
    w = jnp.arange(n_blocks, dtype=i32)
    is_compute = w * bm < total
    item_e = jnp.sum((w[:, None] * bm >= region_end[None, :]).astype(i32), axis=1)
    item_e = jnp.minimum(item_e, N_EXPERTS - 1).astype(i32)
    item_x = jnp.where(is_compute, w, 0).astype(i32)
    gap_start = jnp.concatenate([starts + counts, total[None]]).astype(i32)
    gap_chunks = (jnp.concatenate([region - counts, (n_pad - total)[None]]) // SEG_ROWS).astype(i32)
    return starts, gap_start, gap_chunks, item_e, item_x


def _moe_dispatch(x1, logits, w_up, b_up, w_down, b_down, layer, tm, bm):
    t, d = x1.shape
    n_tiles = t // tm
    worst = t * TOP_K + n_tiles * N_EXPERTS * (SEG_ROWS - 1) + N_EXPERTS * (bm - SEG_ROWS)
    n_pad = -(-worst // bm) * bm
    gates, spos, spt, seg, counts = _route(logits, tm)
    starts, gap_start, gap_chunks, item_e, item_x = _moe_plan(counts[0, :N_EXPERTS], n_pad, bm)
    n_chunks = seg[:, 0, :N_EXPERTS].reshape(-1)
    slab_off = seg[:, 1, :N_EXPERTS].reshape(-1)
    dest_off = (starts[None, :] + seg[:, 2, :N_EXPERTS]).reshape(-1).astype(jnp.int32)
    xs = _dispatch(n_chunks, slab_off, dest_off, gap_start, gap_chunks, spt, x1, n_pad)
    ys = _experts(item_e, item_x, xs, w_up, b_up, w_down, b_down, layer, bm)
    return (n_chunks, slab_off, dest_off), gates, spos, ys


def _pick_tile(n, target, quantum):
    tile = min(n, target)
    while n % tile or tile % quantum:
        tile -= quantum
    return tile


def kernel(x, p, positions, mla_w_in, mla_q_norm, mla_kv_norm, mla_w_uq, mla_w_ukv, mla_w_o, lru_w_in, lru_conv_w, lru_conv_b, lru_w_a, lru_b_a, lru_w_x, lru_b_x, lru_lambda, lru_w_out, ln1_g, ln1_b, ln2_g, ln2_b, moe_w_router, moe_b_router, moe_w_up, moe_b_up, moe_w_down, moe_b_down, ple_w_gate, ple_w_proj):
    batch, seq, d = x.shape
    depth = ln1_g.shape[0]
    t = batch * seq
    assert batch == SUBLANES, "the recurrence kernel maps the batch onto the sublanes of a vreg"
    alpha = (2.0 * depth) ** 0.25
    tm = _pick_tile(t, 512, SUBLANES * batch)
    attn_blk = _pick_tile(seq, 512, LANES)
    moe_bm = _pick_tile(t * TOP_K, 512, SUBLANES)

    xt = x
    half = QK_ROPE_DIM // 2
    inv_freq = jnp.exp(-math.log(ROPE_THETA) * jnp.arange(half, dtype=F32) / half)
    rope_freq = jnp.concatenate([inv_freq, inv_freq, jnp.zeros((LANES - QK_ROPE_DIM,), F32)]).reshape(1, LANES)
    pos_lanes = jnp.broadcast_to(jnp.transpose(positions).astype(F32)[:, :, None],
                                 (seq, batch, LANES)).reshape(seq, batch * LANES)

    row2 = lambda a: a.reshape(1, -1)
    scale = math.log2(math.e) / math.sqrt(QK_NOPE_DIM + QK_ROPE_DIM)

    for layer in range(depth):
        j = layer // 2
        wr = jnp.pad(moe_w_router[layer], ((0, 0), (0, LANES - N_EXPERTS)))
        wr_hi, wr_lo = _split_bf16(wr)
        br = row2(jnp.pad(moe_b_router[layer], (0, LANES - N_EXPERTS)))
        g1, b1 = row2(ln1_g[layer]), row2(ln1_b[layer])
        if layer % 2 == 0:
            q_lora, kv_lora = mla_q_norm.shape[1], mla_kv_norm.shape[1]
            w_in = mla_w_in[j]
            win_p = jnp.pad(w_in, ((0, 0), (0, LANES - QK_ROPE_DIM))).astype(BF16)
            wuq = mla_w_uq[j].reshape(q_lora, MLA_HEADS, QK_NOPE_DIM + QK_ROPE_DIM)
            wuq_p = jnp.pad(wuq, ((0, 0), (0, 0), (0, QK_PAD - QK_NOPE_DIM - QK_ROPE_DIM)))
            wuq_p = wuq_p.reshape(q_lora, MLA_HEADS * QK_PAD).astype(BF16)
            q, k, vt = _mla_proj(xt, pos_lanes, rope_freq, win_p,
                                 row2(mla_q_norm[j] * scale), row2(mla_kv_norm[j]),
                                 wuq_p, mla_w_ukv[j].astype(BF16), seq, batch, attn_blk)
            o = _attention(q, k, vt, seq, batch, attn_blk, 4)
            x1, logits = _post_attn(xt, o, mla_w_o[j].astype(BF16), g1, b1,
                                    wr_hi, wr_lo, br, alpha, seq, batch, attn_blk)
        else:
            x1, logits = _lru_block(
                xt.reshape(t, d), lru_w_in[j].astype(BF16), lru_conv_w[j], row2(lru_conv_b[j]),
                lru_w_a[j].astype(BF16), row2(lru_b_a[j]), lru_w_x[j].astype(BF16), row2(lru_b_x[j]),
                row2(lru_lambda[j]), lru_w_out[j].astype(BF16), g1, b1, wr_hi, wr_lo, br, alpha, batch, tm)
        x1 = x1.reshape(t, d)
        tables, gates, spos, ys = _moe_dispatch(x1, logits.reshape(t, LANES), moe_w_up,
                                                moe_b_up[:, :, None, :], moe_w_down, moe_b_down[:, :, None, :],
                                                layer, tm, moe_bm)
        xt = _post_moe(*tables, x1, gates, spos, ys, p, layer, row2(ln2_g[layer]), row2(ln2_b[layer]),
                       ple_w_gate[layer].astype(BF16), ple_w_proj[layer].astype(BF16), alpha, batch, tm)
        if layer + 1 < depth and (layer + 1) % 2 == 0:
            xt = xt.reshape(seq, batch * d)
    return jnp.transpose(xt.reshape(seq, batch, d), (1, 0, 2))
```

```python
import functools
import math

import jax
import jax.numpy as jnp
from jax import lax
from jax.experimental import pallas as pl
from jax.experimental.pallas import tpu as pltpu

F32 = jnp.float32
BF16 = jnp.bfloat16

MLA_HEADS = 8
QK_NOPE_DIM = 128
QK_ROPE_DIM = 64
V_HEAD_DIM = 128
ROPE_THETA = 10000.0
LRU_BLOCKS = 4
CONV_WIDTH = 4
LRU_C = 8.0
N_EXPERTS = 32
TOP_K = 4
SWIGLU_LIMIT = 7.0
SWIGLU_ALPHA = 1.702
LN_EPS = 1e-5
RMS_EPS = 1e-6

LANES = 128
SUBLANES = 8
QK_PAD = 2 * LANES
VMEM_LIMIT_BYTES = 56 * 1024 * 1024


def _params(*sem, flags=None):
    return pltpu.CompilerParams(dimension_semantics=sem, vmem_limit_bytes=VMEM_LIMIT_BYTES, flags=flags)


def _dot(a, b):
    return jnp.dot(a, b, preferred_element_type=F32)


def _sigmoid(z):
    return 1.0 / (1.0 + jnp.exp(-z))


def _layer_norm(y, g, b):
    mu = jnp.mean(y, axis=-1, keepdims=True)
    yc = y - mu
    var = jnp.mean(yc * yc, axis=-1, keepdims=True)
    return yc * lax.rsqrt(var + LN_EPS) * g + b


def _rms_norm(y, g):
    ms = jnp.mean(y * y, axis=-1, keepdims=True)
    return y * lax.rsqrt(ms + RMS_EPS) * g


def _split_bf16(a):
    hi = a.astype(BF16)
    lo = (a - hi.astype(F32)).astype(BF16)
    return hi, lo


def _router_logits(x1, wr_hi, wr_lo, br):
    hi, lo = _split_bf16(x1)
    return _dot(hi, wr_hi) + (_dot(hi, wr_lo) + _dot(lo, wr_hi)) + br


def _full(shape):
    return pl.BlockSpec(shape, lambda *_: (0,) * len(shape))


def _rope(blk, cos, sin_a, sin_b):
    return (blk * cos + pltpu.roll(blk, LANES - QK_ROPE_DIM // 2, 1) * sin_a
            + pltpu.roll(blk, QK_ROPE_DIM // 2, 1) * sin_b)


def _mla_proj_kernel(x_ref, pos_ref, freq_ref, win_ref, qn_ref, kvn_ref, wuq_ref, wukv_ref,
                     q_ref, k_ref, vt_ref, *, q_lora, kv_lora):
    lat = _dot(x_ref[...].astype(BF16), win_ref[...])
    c_q = _rms_norm(lat[:, :q_lora], qn_ref[...])
    c_kv = _rms_norm(lat[:, q_lora:q_lora + kv_lora], kvn_ref[...])
    ang = pos_ref[...] * freq_ref[...]
    lane = lax.broadcasted_iota(jnp.int32, ang.shape, 1)
    cos, sin = jnp.cos(ang), jnp.sin(ang)
    half = QK_ROPE_DIM // 2
    sin_a = jnp.where(lane < half, -sin, 0.0)
    sin_b = jnp.where(jnp.logical_and(lane >= half, lane < QK_ROPE_DIM), sin, 0.0)
    k_rope = _rope(lat[:, q_lora + kv_lora:], cos, sin_a, sin_b).astype(BF16)
    q = _dot(c_q.astype(BF16), wuq_ref[...])
    kv = _dot(c_kv.astype(BF16), wukv_ref[...])
    for h in range(MLA_HEADS):
        lo = h * QK_PAD
        mid = lo + LANES
        hi = lo + QK_PAD
        q_ref[:, lo:mid] = q[:, lo:mid].astype(BF16)
        q_ref[:, mid:hi] = _rope(q[:, mid:hi], cos, sin_a, sin_b).astype(BF16)
        k_ref[:, lo:mid] = kv[:, lo:mid].astype(BF16)
        k_ref[:, mid:hi] = k_rope
        vt_ref[h * V_HEAD_DIM:(h + 1) * V_HEAD_DIM, :] = jnp.transpose(kv[:, mid:hi]).astype(BF16)


def _col_spec(ts, c):
    return pl.BlockSpec((ts, c), lambda i, b: (i, b))


def _x_spec(x, ts, d):
    if x.ndim == 3:
        return pl.BlockSpec((None, ts, d), lambda i, b: (b, i, 0))
    return _col_spec(ts, d)


def _mla_proj(x, pos, freq, win_p, qn, kvn, wuq_p, wukv, seq, batch, ts):
    d = win_p.shape[0]
    q_lora, kv_lora = qn.shape[1], kvn.shape[1]
    hq = MLA_HEADS * QK_PAD
    hv = MLA_HEADS * V_HEAD_DIM
    col = functools.partial(_col_spec, ts)
    return pl.pallas_call(
        functools.partial(_mla_proj_kernel, q_lora=q_lora, kv_lora=kv_lora),
        grid=(seq // ts, batch),
        in_specs=[_x_spec(x, ts, d), col(LANES), _full(freq.shape), _full(win_p.shape), _full(qn.shape),
                  _full(kvn.shape), _full(wuq_p.shape), _full(wukv.shape)],
        out_specs=[col(hq), col(hq), pl.BlockSpec((hv, ts), lambda i, b: (b, i))],
        out_shape=[jax.ShapeDtypeStruct((seq, batch * hq), BF16), jax.ShapeDtypeStruct((seq, batch * hq), BF16),
                   jax.ShapeDtypeStruct((batch * hv, seq), BF16)],
        compiler_params=_params("parallel", "parallel"),
        name="mla_proj",
    )(x, pos, freq, win_p, qn, kvn, wuq_p, wukv)


ONES_ROWS = 16


def _attn_kernel(q_ref, k_ref, vt_ref, o_ref, m_sc, acc_sc, st_a, st_b, *, blk, heads):
    qi = pl.program_id(1)
    m_sc[...] = jnp.full(m_sc.shape, -jnp.inf, F32)
    acc_sc[...] = jnp.zeros(acc_sc.shape, F32)
    ones = jnp.ones((ONES_ROWS, blk), BF16)

    def scores(ki, st_ref):
        start = pl.multiple_of(ki * blk, blk)
        for h in range(heads):
            k = k_ref[pl.ds(start, blk), h * QK_PAD:(h + 1) * QK_PAD]
            st_ref[h] = lax.dot_general(k, q_ref[:, h * QK_PAD:(h + 1) * QK_PAD], (((1,), (1,)), ((), ())),
                                        preferred_element_type=F32)

    def consume(ki, st_ref, diagonal):
        start = pl.multiple_of(ki * blk, blk)
        for h in range(heads):
            vt = vt_ref[h * V_HEAD_DIM:(h + 1) * V_HEAD_DIM, pl.ds(start, blk)]
            vt = jnp.concatenate([vt, ones], axis=0)
            st = st_ref[h]
            if diagonal:
                keys = lax.broadcasted_iota(jnp.int32, st.shape, 0)
                queries = lax.broadcasted_iota(jnp.int32, st.shape, 1)
                st = jnp.where(keys <= queries, st, -jnp.inf)
            m_prev = m_sc[h]
            m_new = jnp.maximum(m_prev, jnp.max(st, axis=0, keepdims=True))
            alpha = jnp.exp2(m_prev - m_new)
            p = jnp.exp2(st - m_new).astype(BF16)
            acc_sc[h] = alpha * acc_sc[h] + _dot(vt, p)
            m_sc[h] = m_new

    pairs = qi // 2
    scores(0, st_a)

    def body(j, carry):
        scores(2 * j + 1, st_b)
        consume(2 * j, st_a, False)
        scores(2 * j + 2, st_a)
        consume(2 * j + 1, st_b, False)
        return carry

    lax.fori_loop(0, pairs, body, 0)

    @pl.when(qi == 2 * pairs)
    def _():
        consume(qi, st_a, True)

    @pl.when(qi != 2 * pairs)
    def _():
        scores(qi, st_b)
        consume(qi - 1, st_a, False)
        consume(qi, st_b, True)

    for h in range(heads):
        acc = acc_sc[h]
        out_t = acc[:V_HEAD_DIM, :] / acc[V_HEAD_DIM:V_HEAD_DIM + 1, :]
        o_ref[:, h * V_HEAD_DIM:(h + 1) * V_HEAD_DIM] = jnp.transpose(out_t).astype(o_ref.dtype)


def _attention(q, k, vt, seq, batch, blk, heads):
    groups = batch * MLA_HEADS // heads
    return pl.pallas_call(
        functools.partial(_attn_kernel, blk=blk, heads=heads),
        grid=(groups, seq // blk),
        in_specs=[pl.BlockSpec((blk, heads * QK_PAD), lambda g, i: (i, g)),
                  pl.BlockSpec((seq, heads * QK_PAD), lambda g, i: (0, g)),
                  pl.BlockSpec((heads * V_HEAD_DIM, seq), lambda g, i: (g, 0))],
        out_specs=pl.BlockSpec((blk, heads * V_HEAD_DIM), lambda g, i: (i, g)),
        out_shape=jax.ShapeDtypeStruct((seq, batch * MLA_HEADS * V_HEAD_DIM), BF16),
        scratch_shapes=[pltpu.VMEM((heads, 1, blk), F32),
                        pltpu.VMEM((heads, V_HEAD_DIM + ONES_ROWS, blk), F32),
                        pltpu.VMEM((heads, blk, blk), F32), pltpu.VMEM((heads, blk, blk), F32)],
        compiler_params=_params("parallel", "arbitrary"),
        name="mla_attention",
    )(q, k, vt)


def _mix_epilogue(x, mix, g_ref, b_ref, wrh_ref, wrl_ref, br_ref, x1_ref, lg_ref, alpha):
    x1 = _layer_norm(alpha * x + mix, g_ref[...], b_ref[...])
    x1_ref[...] = x1
    lg_ref[...] = _router_logits(x1, wrh_ref[...], wrl_ref[...], br_ref[...])


def _post_attn_kernel(x_ref, o_ref, wo_ref, g_ref, b_ref, wrh_ref, wrl_ref, br_ref, x1_hbm, lg_ref,
                      buf, sem, *, alpha):
    i, b = pl.program_id(0), pl.program_id(1)
    ts = buf.shape[1]
    step = i * pl.num_programs(1) + b
    n_steps = pl.num_programs(0) * pl.num_programs(1)
    slot = lax.rem(step, 2)

    def tile_copy(s):
        return pltpu.make_async_copy(buf.at[s], x1_hbm.at[pl.ds(i * ts, ts), b], sem.at[s])

    @pl.when(step >= 2)
    def _():
        tile_copy(slot).wait()

    mix = _dot(o_ref[...], wo_ref[...])
    _mix_epilogue(x_ref[...], mix, g_ref, b_ref, wrh_ref, wrl_ref, br_ref, buf.at[slot], lg_ref, alpha)
    tile_copy(slot).start()

    @pl.when(step == n_steps - 1)
    def _():
        @pl.when(n_steps >= 2)
        def _():
            tile_copy(1 - slot).wait()
        tile_copy(slot).wait()


def _post_attn(x, o, wo, g, b, wr_hi, wr_lo, br, alpha, seq, batch, ts):
    hv, d = wo.shape
    col = functools.partial(_col_spec, ts)
    return pl.pallas_call(
        functools.partial(_post_attn_kernel, alpha=alpha),
        grid=(seq // ts, batch),
        in_specs=[_x_spec(x, ts, d), col(hv), _full(wo.shape), _full(g.shape), _full(b.shape),
                  _full(wr_hi.shape), _full(wr_lo.shape), _full(br.shape)],
        out_specs=[pl.BlockSpec(memory_space=pl.ANY), col(LANES)],
        out_shape=[jax.ShapeDtypeStruct((seq, batch, d), F32), jax.ShapeDtypeStruct((seq, batch * LANES), F32)],
        scratch_shapes=[pltpu.VMEM((2, ts, d), F32), pltpu.SemaphoreType.DMA((2,))],
        compiler_params=_params("arbitrary", "arbitrary"),
        name="post_attention",
    )(x, o, wo, g, b, wr_hi, wr_lo, br)


def _lru_kernel(x_ref, win_ref, cw_ref, cb_ref, wa_ref, ba_ref, wx_ref, bx_ref, lam_ref, wout_ref,
                g_ref, b_ref, wrh_ref, wrl_ref, br_ref, x1_ref, lg_ref,
                ucarry_sc, hcarry_sc, a_sc, b_sc, h_sc, *, alpha, batch):
    tm, width = a_sc.shape
    halo = (CONV_WIDTH - 1) * batch
    blk_w = width // LRU_BLOCKS

    @pl.when(pl.program_id(0) == 0)
    def _():
        ucarry_sc[...] = jnp.zeros(ucarry_sc.shape, F32)
        hcarry_sc[...] = jnp.zeros(hcarry_sc.shape, F32)

    x = x_ref[...]
    gu = _dot(x.astype(BF16), win_ref[...])
    gate = gu[:, :width]
    u = gu[:, width:]
    u_ext = jnp.concatenate([ucarry_sc[...], u], axis=0)
    ucarry_sc[...] = u[tm - halo:, :]
    cw = cw_ref[...]
    uc = cb_ref[...] + cw[0:1, :] * u_ext[0:tm, :]
    for j in range(1, CONV_WIDTH):
        uc = uc + cw[j:j + 1, :] * u_ext[j * batch:j * batch + tm, :]
    ucb = uc.astype(BF16)
    ra = jnp.concatenate([_dot(ucb[:, n * blk_w:(n + 1) * blk_w], wa_ref[n]) for n in range(LRU_BLOCKS)], axis=1)
    rx = jnp.concatenate([_dot(ucb[:, n * blk_w:(n + 1) * blk_w], wx_ref[n]) for n in range(LRU_BLOCKS)], axis=1)
    r = _sigmoid(ra + ba_ref[...])
    gi = _sigmoid(rx + bx_ref[...])
    z = -lam_ref[...]
    softplus = jnp.maximum(z, 0.0) + jnp.log1p(jnp.exp(-jnp.abs(z)))
    log_a = (-LRU_C) * r * softplus
    a_sc[...] = jnp.exp(log_a)
    th = jnp.tanh(log_a)
    b_sc[...] = jnp.sqrt(-2.0 * th / (1.0 - th)) * gi * uc

    def body(t, h):
        rows = pl.ds(pl.multiple_of(t * batch, batch), batch)
        h = a_sc[rows, :] * h + b_sc[rows, :]
        h_sc[rows, :] = h
        return h

    hcarry_sc[...] = lax.fori_loop(0, tm // batch, body, hcarry_sc[...], unroll=8)
    c = math.sqrt(2.0 / math.pi)
    gelu = 0.5 * gate * (1.0 + jnp.tanh(c * (gate + 0.044715 * (gate * gate * gate))))
    y = gelu * h_sc[...]
    mix = _dot(y.astype(BF16), wout_ref[...])
    _mix_epilogue(x, mix, g_ref, b_ref, wrh_ref, wrl_ref, br_ref, x1_ref, lg_ref, alpha)


def _lru_block(x, win, cw, cb, wa, ba, wx, bx, lam, wout, g, b, wr_hi, wr_lo, br, alpha, batch, tm):
    t, d = x.shape
    width = wout.shape[0]
    row = lambda c: pl.BlockSpec((tm, c), lambda i: (i, 0))
    consts = (win, cw, cb, wa, ba, wx, bx, lam, wout, g, b, wr_hi, wr_lo, br)
    return pl.pallas_call(
        functools.partial(_lru_kernel, alpha=alpha, batch=batch),
        grid=(t // tm,),
        in_specs=[row(d)] + [_full(c.shape) for c in consts],
        out_specs=[row(d), row(LANES)],
        out_shape=[jax.ShapeDtypeStruct((t, d), F32), jax.ShapeDtypeStruct((t, LANES), F32)],
        scratch_shapes=[pltpu.VMEM(((CONV_WIDTH - 1) * batch, width), F32), pltpu.VMEM((batch, width), F32),
                        pltpu.VMEM((tm, width), F32), pltpu.VMEM((tm, width), F32), pltpu.VMEM((tm, width), F32)],
        compiler_params=_params("arbitrary"),
        name="rglru_block",
    )(x, *consts)


SEG_ROWS = SUBLANES


def _route_kernel(lg_ref, gate_ref, spos_ref, spt_ref, seg_ref, cnt_ref, run_sc):
    tm = lg_ref.shape[0]

    @pl.when(pl.program_id(0) == 0)
    def _():
        run_sc[...] = jnp.zeros(run_sc.shape, F32)

    lane = lax.broadcasted_iota(jnp.int32, (tm, LANES), 1)
    lane_f = lane.astype(F32)
    work = jnp.where(lane < N_EXPERTS, lg_ref[...], -jnp.inf)
    tops, onehots = [], []
    for k in range(TOP_K):
        top = jnp.max(work, axis=1, keepdims=True)
        idx = jnp.min(jnp.where(work == top, lane_f, float(LANES)), axis=1, keepdims=True)
        hot = lane_f == idx
        work = jnp.where(hot, -jnp.inf, work)
        tops.append(top)
        onehots.append(hot)
    exps = [jnp.exp(top - tops[0]) for top in tops]
    denom = exps[0]
    for e in exps[1:]:
        denom = denom + e
    gate = jnp.zeros((tm, LANES), F32)
    for k in range(TOP_K):
        gate = jnp.where(lane == k, exps[k] / denom, gate)
    hits = onehots[0].astype(F32)
    for hot in onehots[1:]:
        hits = hits + hot.astype(F32)
    earlier = (lax.broadcasted_iota(jnp.int32, (tm, tm), 1) < lax.broadcasted_iota(jnp.int32, (tm, tm), 0))
    before = _dot(earlier.astype(BF16), hits.astype(BF16))
    count = jnp.sum(hits, axis=0, keepdims=True)
    chunks = jnp.floor((count + (SEG_ROWS - 1.0)) * (1.0 / SEG_ROWS))
    lower = (lax.broadcasted_iota(jnp.int32, (LANES, LANES), 0) < lax.broadcasted_iota(jnp.int32, (LANES, LANES), 1))
    chunks8 = jnp.broadcast_to(chunks, (SUBLANES, LANES))
    slab_off = _dot(chunks8.astype(BF16), lower.astype(BF16))[0:1, :] * SEG_ROWS
    run = run_sc[...]
    spos = jnp.zeros((tm, LANES), F32)
    for k in range(TOP_K):
        s_k = jnp.sum(jnp.where(onehots[k], before + slab_off, 0.0), axis=1, keepdims=True)
        spos = jnp.where(lane == k, s_k, spos)
    run_sc[...] = run + chunks * SEG_ROWS
    gate_ref[...] = gate
    spos_ref[...] = spos.astype(jnp.int32)
    spt_ref[0] = jnp.transpose(spos)[0:SUBLANES, :].astype(jnp.int32)
    row = lax.broadcasted_iota(jnp.int32, (SUBLANES, LANES), 0)
    seg = jnp.where(row == 0, chunks, jnp.where(row == 1, slab_off, jnp.where(row == 2, run, 0.0)))
    seg_ref[0] = seg.astype(jnp.int32)
    cnt_ref[...] = run_sc[...].astype(jnp.int32)


def _route(logits, tm):
    t = logits.shape[0]
    n_tiles = t // tm
    row = pl.BlockSpec((tm, LANES), lambda i: (i, 0))
    return pl.pallas_call(
        _route_kernel,
        grid=(n_tiles,),
        in_specs=[row],
        out_specs=[row, row, pl.BlockSpec((1, SUBLANES, tm), lambda i: (i, 0, 0)),
                   pl.BlockSpec((1, SUBLANES, LANES), lambda i: (i, 0, 0)),
                   pl.BlockSpec((1, LANES), lambda i: (0, 0))],
        out_shape=[jax.ShapeDtypeStruct((t, LANES), F32), jax.ShapeDtypeStruct((t, LANES), jnp.int32),
                   jax.ShapeDtypeStruct((n_tiles, SUBLANES, tm), jnp.int32),
                   jax.ShapeDtypeStruct((n_tiles, SUBLANES, LANES), jnp.int32),
                   jax.ShapeDtypeStruct((1, LANES), jnp.int32)],
        scratch_shapes=[pltpu.VMEM((1, LANES), F32)],
        compiler_params=_params("arbitrary"),
        name="moe_route",
    )(logits)


BIG_COPY_CHUNKS = 4


def _chunk_copy(src, src_row, dst, dst_row, sem, rows=SEG_ROWS):
    return pltpu.make_async_copy(src.at[pl.ds(pl.multiple_of(src_row, SEG_ROWS), rows)],
                                 dst.at[pl.ds(pl.multiple_of(dst_row, SEG_ROWS), rows)], sem)


def _for_each_copy(n_seg, seg_of, fn):
    big = BIG_COPY_CHUNKS * SEG_ROWS
    shift = BIG_COPY_CHUNKS.bit_length() - 1

    def per_expert(e, carry):
        lo, go, n = seg_of(e)
        n_big = lax.shift_right_logical(n, shift)

        def big_copy(c, carry2):
            fn(lo + c * big, go + c * big, big)
            return carry2

        def small_copy(c, carry2):
            fn(lo + n_big * big + c * SEG_ROWS, go + n_big * big + c * SEG_ROWS, SEG_ROWS)
            return carry2

        lax.fori_loop(0, n_big, big_copy, 0)
        lax.fori_loop(0, n - n_big * BIG_COPY_CHUNKS, small_copy, 0)
        return carry

    lax.fori_loop(0, n_seg, per_expert, 0)


def _for_each_segment_copy(tile, nch_ref, loc_ref, glob_ref, fn):
    def seg_of(e):
        s = tile * N_EXPERTS + e
        return loc_ref[s], glob_ref[s], nch_ref[s]

    _for_each_copy(N_EXPERTS, seg_of, fn)


SLAB_CHUNK = 256


def _slab_rows(tm):
    worst = tm * TOP_K + N_EXPERTS * (SEG_ROWS - 1)
    return -(-worst // SLAB_CHUNK) * SLAB_CHUNK


def _dispatch_kernel(nch_ref, loc_ref, glob_ref, gap_ref, gapn_ref, spt_ref, x_ref, xs_hbm, slab_sc, zero_sc, sem):
    i = pl.program_id(0)
    tm = x_ref.shape[0]
    slab_rows = slab_sc.shape[1]
    slot = lax.rem(i, 2)
    slab = slab_sc.at[slot]
    xb = x_ref[...].astype(BF16)
    sp = spt_ref[0]
    for c0 in range(0, slab_rows, SLAB_CHUNK):
        j = lax.broadcasted_iota(jnp.int32, (SLAB_CHUNK, tm), 0) + c0
        hit = j == sp[0:1, :]
        for k in range(1, TOP_K):
            hit = jnp.logical_or(hit, j == sp[k:k + 1, :])
        slab[c0:c0 + SLAB_CHUNK, :] = _dot(hit.astype(BF16), xb)

    def for_each_chunk(tile, fn):
        s = lax.rem(tile, 2)
        _for_each_segment_copy(tile, nch_ref, loc_ref, glob_ref,
                               lambda lo, go, rows: fn(_chunk_copy(slab_sc.at[s], lo, xs_hbm, go, sem.at[s], rows)))

    @pl.when(i > 0)
    def _():
        for_each_chunk(i - 1, lambda cp: cp.wait())

    for_each_chunk(i, lambda cp: cp.start())

    def for_each_gap_copy(fn):
        _for_each_copy(gap_ref.shape[0], lambda e: (0, gap_ref[e], gapn_ref[e]),
                       lambda lo, go, rows: fn(_chunk_copy(zero_sc, 0, xs_hbm, go, sem.at[2], rows)))

    @pl.when(i == pl.num_programs(0) - 1)
    def _():
        zero_sc[...] = jnp.zeros(zero_sc.shape, zero_sc.dtype)
        for_each_gap_copy(lambda cp: cp.start())
        for_each_chunk(i, lambda cp: cp.wait())
        for_each_gap_copy(lambda cp: cp.wait())


def _dispatch(n_chunks, slab_off, dest_off, gap_start, gap_chunks, spt, x1, n_pad):
    t, d = x1.shape
    n_tiles, _, tm = spt.shape
    slab_rows = _slab_rows(tm)
    grid_spec = pltpu.PrefetchScalarGridSpec(
        num_scalar_prefetch=5,
        grid=(n_tiles,),
        in_specs=[pl.BlockSpec((1, SUBLANES, tm), lambda i, *_: (i, 0, 0)),
                  pl.BlockSpec((tm, d), lambda i, *_: (i, 0))],
        out_specs=pl.BlockSpec(memory_space=pl.ANY),
        scratch_shapes=[pltpu.VMEM((2, slab_rows, d), F32), pltpu.VMEM((BIG_COPY_CHUNKS * SEG_ROWS, d), F32),
                        pltpu.SemaphoreType.DMA((3,))],
    )
    return pl.pallas_call(
        _dispatch_kernel,
        grid_spec=grid_spec,
        out_shape=jax.ShapeDtypeStruct((n_pad, d), F32),
        compiler_params=_params("arbitrary"),
        name="moe_dispatch",
    )(n_chunks, slab_off, dest_off, gap_start, gap_chunks, spt, x1)


def _expert_kernel(ie_ref, ix_ref, xs_ref, wup_ref, bup_ref, wdn_ref, bdn_ref, ys_ref, wup_sc, wdn_sc):
    w = pl.program_id(0)
    d_ff = wdn_ref.shape[1]
    compute = ix_ref[w] == w
    new_expert = jnp.logical_or(w == 0, ie_ref[w] != ie_ref[jnp.maximum(w - 1, 0)])

    @pl.when(jnp.logical_and(compute, new_expert))
    def _():
        wup_sc[...] = wup_ref[0].astype(BF16)
        wdn_sc[...] = wdn_ref[0].astype(BF16)

    @pl.when(jnp.logical_not(compute))
    def _():
        ys_ref[...] = jnp.zeros(ys_ref.shape, ys_ref.dtype)

    @pl.when(compute)
    def _():
        hb = _dot(xs_ref[...].astype(BF16), wup_sc[...]) + bup_ref[0]
        gl = jnp.minimum(hb[:, :d_ff], SWIGLU_LIMIT)
        up = jnp.clip(hb[:, d_ff:], -SWIGLU_LIMIT, SWIGLU_LIMIT)
        yb = (up + 1.0) * (gl * _sigmoid(SWIGLU_ALPHA * gl))
        ys_ref[...] = _dot(yb.astype(BF16), wdn_sc[...]) + bdn_ref[0]


def _experts(item_e, item_x, xs, w_up, b_up, w_down, b_down, layer, bm):
    n, d = xs.shape
    f2 = w_up.shape[3]
    xblk = lambda w, ie, ix: (ix[w], 0)
    exp3 = lambda w, ie, ix: (layer, ie[w], 0, 0)
    grid_spec = pltpu.PrefetchScalarGridSpec(
        num_scalar_prefetch=2,
        grid=(n // bm,),
        in_specs=[pl.BlockSpec((bm, d), xblk),
                  pl.BlockSpec((None, 1, d, f2), exp3), pl.BlockSpec((None, 1, 1, f2), exp3),
                  pl.BlockSpec((None, 1, f2 // 2, d), exp3), pl.BlockSpec((None, 1, 1, d), exp3)],
        out_specs=pl.BlockSpec((bm, d), lambda w, ie, ix: (w, 0)),
        scratch_shapes=[pltpu.VMEM((d, f2), BF16), pltpu.VMEM((f2 // 2, d), BF16)],
    )
    return pl.pallas_call(
        _expert_kernel,
        grid_spec=grid_spec,
        out_shape=jax.ShapeDtypeStruct((n, d), F32),
        compiler_params=_params("arbitrary"),
        name="moe_experts",
    )(item_e, item_x, xs, w_up, b_up, w_down, b_down)


def _post_moe_kernel(nch_ref, loc_ref, glob_ref, x1_ref, gate_ref, spos_ref, p_ref, g_ref, b_ref, wg_ref, wp_ref,
                     ys_hbm, out_ref, slab_sc, sem, *, alpha):
    i = pl.program_id(0)
    n_tiles = pl.num_programs(0)
    tm = x1_ref.shape[0]
    slab_rows = slab_sc.shape[1]

    def for_each_chunk(tile, fn):
        s = lax.rem(tile, 2)
        _for_each_segment_copy(tile, nch_ref, loc_ref, glob_ref,
                               lambda lo, go, rows: fn(_chunk_copy(ys_hbm, go, slab_sc.at[s], lo, sem.at[s], rows)))

    @pl.when(i == 0)
    def _():
        slab_sc[...] = jnp.zeros(slab_sc.shape, slab_sc.dtype)
        for_each_chunk(i, lambda cp: cp.start())

    @pl.when(i + 1 < n_tiles)
    def _():
        for_each_chunk(i + 1, lambda cp: cp.start())

    for_each_chunk(i, lambda cp: cp.wait())
    slab = slab_sc.at[lax.rem(i, 2)]
    gates = gate_ref[...]
    spos = spos_ref[...]
    ffn = jnp.zeros(x1_ref.shape, F32)
    for c0 in range(0, slab_rows, SLAB_CHUNK):
        col = lax.broadcasted_iota(jnp.int32, (tm, SLAB_CHUNK), 1) + c0
        weight = jnp.zeros((tm, SLAB_CHUNK), F32)
        for k in range(TOP_K):
            weight = jnp.where(col == spos[:, k:k + 1], gates[:, k:k + 1], weight)
        ffn = ffn + _dot(weight.astype(BF16), slab[c0:c0 + SLAB_CHUNK, :].astype(BF16))
    x2 = _layer_norm(alpha * x1_ref[...] + ffn, g_ref[...], b_ref[...])
    gate = _sigmoid(_dot(x2.astype(BF16), wg_ref[...]))
    batch, ts = p_ref.shape[0], p_ref.shape[1]
    p_bm = p_ref[...].reshape(tm, p_ref.shape[2]).astype(BF16)
    tok = lax.broadcasted_iota(jnp.int32, (tm, tm), 0)
    src = lax.broadcasted_iota(jnp.int32, (tm, tm), 1)
    shift = batch.bit_length() - 1
    pick = src == jnp.bitwise_and(tok, batch - 1) * ts + lax.shift_right_logical(tok, shift)
    p_tok = _dot(pick.astype(BF16), p_bm).astype(BF16)
    out_ref[...] = x2 + gate * _dot(p_tok, wp_ref[...])


def _post_moe(n_chunks, slab_off, dest_off, x1, gates, spos, ys, p, layer, g, b, wg, wp, alpha, batch, tm):
    t, d = x1.shape
    ts = tm // batch
    row = lambda c: pl.BlockSpec((tm, c), lambda i, *_: (i, 0))
    grid_spec = pltpu.PrefetchScalarGridSpec(
        num_scalar_prefetch=3,
        grid=(t // tm,),
        in_specs=[row(d), row(LANES), row(LANES),
                  pl.BlockSpec((None, batch, ts, p.shape[-1]), lambda i, *_: (layer, 0, i, 0)),
                  _full(g.shape), _full(b.shape), _full(wg.shape), _full(wp.shape),
                  pl.BlockSpec(memory_space=pl.ANY)],
        out_specs=row(d),
        scratch_shapes=[pltpu.VMEM((2, _slab_rows(tm), d), F32), pltpu.SemaphoreType.DMA((2,))],
    )
    return pl.pallas_call(
        functools.partial(_post_moe_kernel, alpha=alpha),
        grid_spec=grid_spec,
        out_shape=jax.ShapeDtypeStruct((t, d), F32),
        compiler_params=_params("arbitrary"),
        name="post_moe_ple",
    )(n_chunks, slab_off, dest_off, x1, gates, spos, p, g, b, wg, wp, ys)


def _moe_plan(counts, n_pad, bm):
    i32 = jnp.int32
    n_blocks = n_pad // bm
    region = (counts + bm - 1) // bm * bm
    region_end = jnp.cumsum(region)
    starts = region_end - region
    total = region_end[-1]
    w = jnp.arange(n_blocks, dtype=i32)
    is_compute = w * bm < total
    item_e = jnp.sum((w[:, None] * bm >= region_end[None, :]).astype(i32), axis=1)
    item_e = jnp.minimum(item_e, N_EXPERTS - 1).astype(i32)
    item_x = jnp.where(is_compute, w, 0).astype(i32)
    gap_start = jnp.concatenate([starts + counts, total[None]]).astype(i32)
    gap_chunks = (jnp.concatenate([region - counts, (n_pad - total)[None]]) // SEG_ROWS).astype(i32)
    return starts, gap_start, gap_chunks, item_e, item_x


def _moe_dispatch(x1, logits, w_up, b_up, w_down, b_down, layer, tm, bm):
    t, d = x1.shape
    n_tiles = t // tm
    worst = t * TOP_K + n_tiles * N_EXPERTS * (SEG_ROWS - 1) + N_EXPERTS * (bm - SEG_ROWS)
    n_pad = -(-worst // bm) * bm
    gates, spos, spt, seg, counts = _route(logits, tm)
    starts, gap_start, gap_chunks, item_e, item_x = _moe_plan(counts[0, :N_EXPERTS], n_pad, bm)
    n_chunks = seg[:, 0, :N_EXPERTS].reshape(-1)
    slab_off = seg[:, 1, :N_EXPERTS].reshape(-1)
    dest_off = (starts[None, :] + seg[:, 2, :N_EXPERTS]).reshape(-1).astype(jnp.int32)
    xs = _dispatch(n_chunks, slab_off, dest_off, gap_start, gap_chunks, spt, x1, n_pad)
    ys = _experts(item_e, item_x, xs, w_up, b_up, w_down, b_down, layer, bm)
    return (n_chunks, slab_off, dest_off), gates, spos, ys


def _pick_tile(n, target, quantum):
    tile = min(n, target)
    while n % tile or tile % quantum:
        tile -= quantum
    return tile


def kernel(x, p, positions, mla_w_in, mla_q_norm, mla_kv_norm, mla_w_uq, mla_w_ukv, mla_w_o, lru_w_in, lru_conv_w, lru_conv_b, lru_w_a, lru_b_a, lru_w_x, lru_b_x, lru_lambda, lru_w_out, ln1_g, ln1_b, ln2_g, ln2_b, moe_w_router, moe_b_router, moe_w_up, moe_b_up, moe_w_down, moe_b_down, ple_w_gate, ple_w_proj):
    batch, seq, d = x.shape
    depth = ln1_g.shape[0]
    t = batch * seq
    assert batch == SUBLANES, "the recurrence kernel maps the batch onto the sublanes of a vreg"
    alpha = (2.0 * depth) ** 0.25
    tm = _pick_tile(t, 512, SUBLANES * batch)
    attn_blk = _pick_tile(seq, 512, LANES)
    moe_bm = _pick_tile(t * TOP_K, 512, SUBLANES)

    xt = x
    half = QK_ROPE_DIM // 2
    inv_freq = jnp.exp(-math.log(ROPE_THETA) * jnp.arange(half, dtype=F32) / half)
    rope_freq = jnp.concatenate([inv_freq, inv_freq, jnp.zeros((LANES - QK_ROPE_DIM,), F32)]).reshape(1, LANES)
    pos_lanes = jnp.broadcast_to(jnp.transpose(positions).astype(F32)[:, :, None],
                                 (seq, batch, LANES)).reshape(seq, batch * LANES)

    row2 = lambda a: a.reshape(1, -1)
    scale = math.log2(math.e) / math.sqrt(QK_NOPE_DIM + QK_ROPE_DIM)

    for layer in range(depth):
        j = layer // 2
        wr = jnp.pad(moe_w_router[layer], ((0, 0), (0, LANES - N_EXPERTS)))
        wr_hi, wr_lo = _split_bf16(wr)
        br = row2(jnp.pad(moe_b_router[layer], (0, LANES - N_EXPERTS)))
        g1, b1 = row2(ln1_g[layer]), row2(ln1_b[layer])
        if layer % 2 == 0:
            q_lora, kv_lora = mla_q_norm.shape[1], mla_kv_norm.shape[1]
            w_in = mla_w_in[j]
            win_p = jnp.pad(w_in, ((0, 0), (0, LANES - QK_ROPE_DIM))).astype(BF16)
            wuq = mla_w_uq[j].reshape(q_lora, MLA_HEADS, QK_NOPE_DIM + QK_ROPE_DIM)
            wuq_p = jnp.pad(wuq, ((0, 0), (0, 0), (0, QK_PAD - QK_NOPE_DIM - QK_ROPE_DIM)))
            wuq_p = wuq_p.reshape(q_lora, MLA_HEADS * QK_PAD).astype(BF16)
            q, k, vt = _mla_proj(xt, pos_lanes, rope_freq, win_p,
                                 row2(mla_q_norm[j] * scale), row2(mla_kv_norm[j]),
                                 wuq_p, mla_w_ukv[j].astype(BF16), seq, batch, attn_blk)
            o = _attention(q, k, vt, seq, batch, attn_blk, 4)
            x1, logits = _post_attn(xt, o, mla_w_o[j].astype(BF16), g1, b1,
                                    wr_hi, wr_lo, br, alpha, seq, batch, attn_blk)
        else:
            x1, logits = _lru_block(
                xt.reshape(t, d), lru_w_in[j].astype(BF16), lru_conv_w[j], row2(lru_conv_b[j]),
                lru_w_a[j].astype(BF16), row2(lru_b_a[j]), lru_w_x[j].astype(BF16), row2(lru_b_x[j]),
                row2(lru_lambda[j]), lru_w_out[j].astype(BF16), g1, b1, wr_hi, wr_lo, br, alpha, batch, tm)
        x1 = x1.reshape(t, d)
        tables, gates, spos, ys = _moe_dispatch(x1, logits.reshape(t, LANES), moe_w_up,
                                                moe_b_up[:, :, None, :], moe_w_down, moe_b_down[:, :, None, :],
                                                layer, tm, moe_bm)
        xt = _post_moe(*tables, x1, gates, spos, ys, p, layer, row2(ln2_g[layer]), row2(ln2_b[layer]),
                       ple_w_gate[layer].astype(BF16), ple_w_proj[layer].astype(BF16), alpha, batch, tm)
        if layer + 1 < depth and (layer + 1) % 2 == 0:
            xt = xt.reshape(seq, batch * d)
    return jnp.transpose(xt.reshape(seq, batch, d), (1, 0, 2))
```

```python
import functools
import math

import jax
import jax.numpy as jnp
from jax import lax
from jax.experimental import pallas as pl
from jax.experimental.pallas import tpu as pltpu

F32 = jnp.float32
BF16 = jnp.bfloat16

MLA_HEADS = 8
QK_NOPE_DIM = 128
QK_ROPE_DIM = 64
V_HEAD_DIM = 128
ROPE_THETA = 10000.0
LRU_BLOCKS = 4
CONV_WIDTH = 4
LRU_C = 8.0
N_EXPERTS = 32
TOP_K = 4
SWIGLU_LIMIT = 7.0
SWIGLU_ALPHA = 1.702
LN_EPS = 1e-5
RMS_EPS = 1e-6

LANES = 128
SUBLANES = 8
QK_PAD = 2 * LANES
VMEM_LIMIT_BYTES = 56 * 1024 * 1024


def _params(*sem, flags=None):
    return pltpu.CompilerParams(dimension_semantics=sem, vmem_limit_bytes=VMEM_LIMIT_BYTES, flags=flags)


def _dot(a, b):
    return jnp.dot(a, b, preferred_element_type=F32)


def _sigmoid(z):
    return 1.0 / (1.0 + jnp.exp(-z))


def _layer_norm(y, g, b):
    mu = jnp.mean(y, axis=-1, keepdims=True)
    yc = y - mu
    var = jnp.mean(yc * yc, axis=-1, keepdims=True)
    return yc * lax.rsqrt(var + LN_EPS) * g + b


def _rms_norm(y, g):
    ms = jnp.mean(y * y, axis=-1, keepdims=True)
    return y * lax.rsqrt(ms + RMS_EPS) * g


def _split_bf16(a):
    hi = a.astype(BF16)
    lo = (a - hi.astype(F32)).astype(BF16)
    return hi, lo


def _router_logits(x1, wr_hi, wr_lo, br):
    hi, lo = _split_bf16(x1)
    return _dot(hi, wr_hi) + (_dot(hi, wr_lo) + _dot(lo, wr_hi)) + br


def _full(shape):
    return pl.BlockSpec(shape, lambda *_: (0,) * len(shape))


def _rope(blk, cos, sin_a, sin_b):
    return (blk * cos + pltpu.roll(blk, LANES - QK_ROPE_DIM // 2, 1) * sin_a
            + pltpu.roll(blk, QK_ROPE_DIM // 2, 1) * sin_b)


def _mla_proj_kernel(x_ref, pos_ref, freq_ref, win_ref, qn_ref, kvn_ref, wuq_ref, wukv_ref,
                     q_ref, k_ref, vt_ref, *, q_lora, kv_lora):
    lat = _dot(x_ref[...].astype(BF16), win_ref[...])
    c_q = _rms_norm(lat[:, :q_lora], qn_ref[...])
    c_kv = _rms_norm(lat[:, q_lora:q_lora + kv_lora], kvn_ref[...])
    ang = pos_ref[...] * freq_ref[...]
    lane = lax.broadcasted_iota(jnp.int32, ang.shape, 1)
    cos, sin = jnp.cos(ang), jnp.sin(ang)
    half = QK_ROPE_DIM // 2
    sin_a = jnp.where(lane < half, -sin, 0.0)
    sin_b = jnp.where(jnp.logical_and(lane >= half, lane < QK_ROPE_DIM), sin, 0.0)
    k_rope = _rope(lat[:, q_lora + kv_lora:], cos, sin_a, sin_b).astype(BF16)
    q = _dot(c_q.astype(BF16), wuq_ref[...])
    kv = _dot(c_kv.astype(BF16), wukv_ref[...])
    for h in range(MLA_HEADS):
        lo = h * QK_PAD
        mid = lo + LANES
        hi = lo + QK_PAD
        q_ref[:, lo:mid] = q[:, lo:mid].astype(BF16)
        q_ref[:, mid:hi] = _rope(q[:, mid:hi], cos, sin_a, sin_b).astype(BF16)
        k_ref[:, lo:mid] = kv[:, lo:mid].astype(BF16)
        k_ref[:, mid:hi] = k_rope
        vt_ref[h * V_HEAD_DIM:(h + 1) * V_HEAD_DIM, :] = jnp.transpose(kv[:, mid:hi]).astype(BF16)


def _col_spec(ts, c):
    return pl.BlockSpec((ts, c), lambda i, b: (i, b))


def _x_spec(x, ts, d):
    if x.ndim == 3:
        return pl.BlockSpec((None, ts, d), lambda i, b: (b, i, 0))
    return _col_spec(ts, d)


def _mla_proj(x, pos, freq, win_p, qn, kvn, wuq_p, wukv, seq, batch, ts):
    d = win_p.shape[0]
    q_lora, kv_lora = qn.shape[1], kvn.shape[1]
    hq = MLA_HEADS * QK_PAD
    hv = MLA_HEADS * V_HEAD_DIM
    col = functools.partial(_col_spec, ts)
    return pl.pallas_call(
        functools.partial(_mla_proj_kernel, q_lora=q_lora, kv_lora=kv_lora),
        grid=(seq // ts, batch),
        in_specs=[_x_spec(x, ts, d), col(LANES), _full(freq.shape), _full(win_p.shape), _full(qn.shape),
                  _full(kvn.shape), _full(wuq_p.shape), _full(wukv.shape)],
        out_specs=[col(hq), col(hq), pl.BlockSpec((hv, ts), lambda i, b: (b, i))],
        out_shape=[jax.ShapeDtypeStruct((seq, batch * hq), BF16), jax.ShapeDtypeStruct((seq, batch * hq), BF16),
                   jax.ShapeDtypeStruct((batch * hv, seq), BF16)],
        compiler_params=_params("parallel", "parallel"),
        name="mla_proj",
    )(x, pos, freq, win_p, qn, kvn, wuq_p, wukv)


ONES_ROWS = 16


def _attn_kernel(q_ref, k_ref, vt_ref, o_ref, m_sc, acc_sc, st_a, st_b, *, blk, heads):
    qi = pl.program_id(1)
    m_sc[...] = jnp.full(m_sc.shape, -jnp.inf, F32)
    acc_sc[...] = jnp.zeros(acc_sc.shape, F32)
    ones = jnp.ones((ONES_ROWS, blk), BF16)

    def scores(ki, st_ref):
        start = pl.multiple_of(ki * blk, blk)
        for h in range(heads):
            k = k_ref[pl.ds(start, blk), h * QK_PAD:(h + 1) * QK_PAD]
            st_ref[h] = lax.dot_general(k, q_ref[:, h * QK_PAD:(h + 1) * QK_PAD], (((1,), (1,)), ((), ())),
                                        preferred_element_type=F32)

    def consume(ki, st_ref, diagonal):
        start = pl.multiple_of(ki * blk, blk)
        for h in range(heads):
            vt = vt_ref[h * V_HEAD_DIM:(h + 1) * V_HEAD_DIM, pl.ds(start, blk)]
            vt = jnp.concatenate([vt, ones], axis=0)
            st = st_ref[h]
            if diagonal:
                keys = lax.broadcasted_iota(jnp.int32, st.shape, 0)
                queries = lax.broadcasted_iota(jnp.int32, st.shape, 1)
                st = jnp.where(keys <= queries, st, -jnp.inf)
            m_prev = m_sc[h]
            m_new = jnp.maximum(m_prev, jnp.max(st, axis=0, keepdims=True))
            alpha = jnp.exp2(m_prev - m_new)
            p = jnp.exp2(st - m_new).astype(BF16)
            acc_sc[h] = alpha * acc_sc[h] + _dot(vt, p)
            m_sc[h] = m_new

    pairs = qi // 2
    scores(0, st_a)

    def body(j, carry):
        scores(2 * j + 1, st_b)
        consume(2 * j, st_a, False)
        scores(2 * j + 2, st_a)
        consume(2 * j + 1, st_b, False)
        return carry

    lax.fori_loop(0, pairs, body, 0)

    @pl.when(qi == 2 * pairs)
    def _():
        consume(qi, st_a, True)

    @pl.when(qi != 2 * pairs)
    def _():
        scores(qi, st_b)
        consume(qi - 1, st_a, False)
        consume(qi, st_b, True)

    for h in range(heads):
        acc = acc_sc[h]
        out_t = acc[:V_HEAD_DIM, :] / acc[V_HEAD_DIM:V_HEAD_DIM + 1, :]
        o_ref[:, h * V_HEAD_DIM:(h + 1) * V_HEAD_DIM] = jnp.transpose(out_t).astype(o_ref.dtype)


def _attention(q, k, vt, seq, batch, blk, heads):
    groups = batch * MLA_HEADS // heads
    return pl.pallas_call(
        functools.partial(_attn_kernel, blk=blk, heads=heads),
        grid=(groups, seq // blk),
        in_specs=[pl.BlockSpec((blk, heads * QK_PAD), lambda g, i: (i, g)),
                  pl.BlockSpec((seq, heads * QK_PAD), lambda g, i: (0, g)),
                  pl.BlockSpec((heads * V_HEAD_DIM, seq), lambda g, i: (g, 0))],
        out_specs=pl.BlockSpec((blk, heads * V_HEAD_DIM), lambda g, i: (i, g)),
        out_shape=jax.ShapeDtypeStruct((seq, batch * MLA_HEADS * V_HEAD_DIM), BF16),
        scratch_shapes=[pltpu.VMEM((heads, 1, blk), F32),
                        pltpu.VMEM((heads, V_HEAD_DIM + ONES_ROWS, blk), F32),
                        pltpu.VMEM((heads, blk, blk), F32), pltpu.VMEM((heads, blk, blk), F32)],
        compiler_params=_params("parallel", "arbitrary"),
        name="mla_attention",
    )(q, k, vt)


def _mix_epilogue(x, mix, g_ref, b_ref, wrh_ref, wrl_ref, br_ref, x1_ref, lg_ref, alpha):
    x1 = _layer_norm(alpha * x + mix, g_ref[...], b_ref[...])
    x1_ref[...] = x1
    lg_ref[...] = _router_logits(x1, wrh_ref[...], wrl_ref[...], br_ref[...])


def _post_attn_kernel(x_ref, o_ref, wo_ref, g_ref, b_ref, wrh_ref, wrl_ref, br_ref, x1_hbm, lg_ref,
                      buf, sem, *, alpha):
    i, b = pl.program_id(0), pl.program_id(1)
    ts = buf.shape[1]
    step = i * pl.num_programs(1) + b
    n_steps = pl.num_programs(0) * pl.num_programs(1)
    slot = lax.rem(step, 2)

    def tile_copy(s):
        return pltpu.make_async_copy(buf.at[s], x1_hbm.at[pl.ds(i * ts, ts), b], sem.at[s])

    @pl.when(step >= 2)
    def _():
        tile_copy(slot).wait()

    mix = _dot(o_ref[...], wo_ref[...])
    _mix_epilogue(x_ref[...], mix, g_ref, b_ref, wrh_ref, wrl_ref, br_ref, buf.at[slot], lg_ref, alpha)
    tile_copy(slot).start()

    @pl.when(step == n_steps - 1)
    def _():
        @pl.when(n_steps >= 2)
        def _():
            tile_copy(1 - slot).wait()
        tile_copy(slot).wait()


def _post_attn(x, o, wo, g, b, wr_hi, wr_lo, br, alpha, seq, batch, ts):
    hv, d = wo.shape
    col = functools.partial(_col_spec, ts)
    return pl.pallas_call(
        functools.partial(_post_attn_kernel, alpha=alpha),
        grid=(seq // ts, batch),
        in_specs=[_x_spec(x, ts, d), col(hv), _full(wo.shape), _full(g.shape), _full(b.shape),
                  _full(wr_hi.shape), _full(wr_lo.shape), _full(br.shape)],
        out_specs=[pl.BlockSpec(memory_space=pl.ANY), col(LANES)],
        out_shape=[jax.ShapeDtypeStruct((seq, batch, d), F32), jax.ShapeDtypeStruct((seq, batch * LANES), F32)],
        scratch_shapes=[pltpu.VMEM((2, ts, d), F32), pltpu.SemaphoreType.DMA((2,))],
        compiler_params=_params("arbitrary", "arbitrary"),
        name="post_attention",
    )(x, o, wo, g, b, wr_hi, wr_lo, br)


def _lru_kernel(x_ref, win_ref, cw_ref, cb_ref, wa_ref, ba_ref, wx_ref, bx_ref, lam_ref, wout_ref,
                g_ref, b_ref, wrh_ref, wrl_ref, br_ref, x1_ref, lg_ref,
                ucarry_sc, hcarry_sc, a_sc, b_sc, h_sc, *, alpha, batch):
    tm, width = a_sc.shape
    halo = (CONV_WIDTH - 1) * batch
    blk_w = width // LRU_BLOCKS

    @pl.when(pl.program_id(0) == 0)
    def _():
        ucarry_sc[...] = jnp.zeros(ucarry_sc.shape, F32)
        hcarry_sc[...] = jnp.zeros(hcarry_sc.shape, F32)

    x = x_ref[...]
    xb = x.astype(BF16)
    u = _dot(xb, win_ref[:, width:])
    u_ext = jnp.concatenate([ucarry_sc[...], u], axis=0)
    ucarry_sc[...] = u[tm - halo:, :]
    cw = cw_ref[...]
    uc = cb_ref[...] + cw[0:1, :] * u_ext[0:tm, :]
    for j in range(1, CONV_WIDTH):
        uc = uc + cw[j:j + 1, :] * u_ext[j * batch:j * batch + tm, :]
    ucb = uc.astype(BF16)
    ra = jnp.concatenate([_dot(ucb[:, n * blk_w:(n + 1) * blk_w], wa_ref[n]) for n in range(LRU_BLOCKS)], axis=1)
    rx = jnp.concatenate([_dot(ucb[:, n * blk_w:(n + 1) * blk_w], wx_ref[n]) for n in range(LRU_BLOCKS)], axis=1)
    r = _sigmoid(ra + ba_ref[...])
    gi = _sigmoid(rx + bx_ref[...])
    z = -lam_ref[...]
    softplus = jnp.maximum(z, 0.0) + jnp.log1p(jnp.exp(-jnp.abs(z)))
    log_a = (-LRU_C) * r * softplus
    a_sc[...] = jnp.exp(log_a)
    th = jnp.tanh(log_a)
    b_sc[...] = jnp.sqrt(-2.0 * th / (1.0 - th)) * gi * uc

    def body(t, h):
        rows = pl.ds(pl.multiple_of(t * batch, batch), batch)
        h = a_sc[rows, :] * h + b_sc[rows, :]
        h_sc[rows, :] = h
        return h

    hcarry_sc[...] = lax.fori_loop(0, tm // batch, body, hcarry_sc[...], unroll=8)
    gate = _dot(xb, win_ref[:, :width])
    c = math.sqrt(2.0 / math.pi)
    gelu = 0.5 * gate * (1.0 + jnp.tanh(c * (gate + 0.044715 * (gate * gate * gate))))
    y = gelu * h_sc[...]
    mix = _dot(y.astype(BF16), wout_ref[...])
    _mix_epilogue(x, mix, g_ref, b_ref, wrh_ref, wrl_ref, br_ref, x1_ref, lg_ref, alpha)


def _lru_block(x, win, cw, cb, wa, ba, wx, bx, lam, wout, g, b, wr_hi, wr_lo, br, alpha, batch, tm):
    t, d = x.shape
    width = wout.shape[0]
    row = lambda c: pl.BlockSpec((tm, c), lambda i: (i, 0))
    consts = (win, cw, cb, wa, ba, wx, bx, lam, wout, g, b, wr_hi, wr_lo, br)
    return pl.pallas_call(
        functools.partial(_lru_kernel, alpha=alpha, batch=batch),
        grid=(t // tm,),
        in_specs=[row(d)] + [_full(c.shape) for c in consts],
        out_specs=[row(d), row(LANES)],
        out_shape=[jax.ShapeDtypeStruct((t, d), F32), jax.ShapeDtypeStruct((t, LANES), F32)],
        scratch_shapes=[pltpu.VMEM(((CONV_WIDTH - 1) * batch, width), F32), pltpu.VMEM((batch, width), F32),
                        pltpu.VMEM((tm, width), F32), pltpu.VMEM((tm, width), F32), pltpu.VMEM((tm, width), F32)],
        compiler_params=_params("arbitrary"),
        name="rglru_block",
    )(x, *consts)


SEG_ROWS = SUBLANES


def _route_kernel(lg_ref, gate_ref, spos_ref, spt_ref, seg_ref, cnt_ref, run_sc):
    tm = lg_ref.shape[0]

    @pl.when(pl.program_id(0) == 0)
    def _():
        run_sc[...] = jnp.zeros(run_sc.shape, F32)

    lane = lax.broadcasted_iota(jnp.int32, (tm, LANES), 1)
    lane_f = lane.astype(F32)
    work = jnp.where(lane < N_EXPERTS, lg_ref[...], -jnp.inf)
    tops, onehots = [], []
    for k in range(TOP_K):
        top = jnp.max(work, axis=1, keepdims=True)
        idx = jnp.min(jnp.where(work == top, lane_f, float(LANES)), axis=1, keepdims=True)
        hot = lane_f == idx
        work = jnp.where(hot, -jnp.inf, work)
        tops.append(top)
        onehots.append(hot)
    exps = [jnp.exp(top - tops[0]) for top in tops]
    denom = exps[0]
    for e in exps[1:]:
        denom = denom + e
    gate = jnp.zeros((tm, LANES), F32)
    for k in range(TOP_K):
        gate = jnp.where(lane == k, exps[k] / denom, gate)
    hits = onehots[0].astype(F32)
    for hot in onehots[1:]:
        hits = hits + hot.astype(F32)
    earlier = (lax.broadcasted_iota(jnp.int32, (tm, tm), 1) < lax.broadcasted_iota(jnp.int32, (tm, tm), 0))
    before = _dot(earlier.astype(BF16), hits.astype(BF16))
    count = jnp.sum(hits, axis=0, keepdims=True)
    chunks = jnp.floor((count + (SEG_ROWS - 1.0)) * (1.0 / SEG_ROWS))
    lower = (lax.broadcasted_iota(jnp.int32, (LANES, LANES), 0) < lax.broadcasted_iota(jnp.int32, (LANES, LANES), 1))
    chunks8 = jnp.broadcast_to(chunks, (SUBLANES, LANES))
    slab_off = _dot(chunks8.astype(BF16), lower.astype(BF16))[0:1, :] * SEG_ROWS
    run = run_sc[...]
    spos = jnp.zeros((tm, LANES), F32)
    for k in range(TOP_K):
        s_k = jnp.sum(jnp.where(onehots[k], before + slab_off, 0.0), axis=1, keepdims=True)
        spos = jnp.where(lane == k, s_k, spos)
    run_sc[...] = run + chunks * SEG_ROWS
    gate_ref[...] = gate
    spos_ref[...] = spos.astype(jnp.int32)
    spt_ref[0] = jnp.transpose(spos)[0:SUBLANES, :].astype(jnp.int32)
    row = lax.broadcasted_iota(jnp.int32, (SUBLANES, LANES), 0)
    seg = jnp.where(row == 0, chunks, jnp.where(row == 1, slab_off, jnp.where(row == 2, run, 0.0)))
    seg_ref[0] = seg.astype(jnp.int32)
    cnt_ref[...] = run_sc[...].astype(jnp.int32)


def _route(logits, tm):
    t = logits.shape[0]
    n_tiles = t // tm
    row = pl.BlockSpec((tm, LANES), lambda i: (i, 0))
    return pl.pallas_call(
        _route_kernel,
        grid=(n_tiles,),
        in_specs=[row],
        out_specs=[row, row, pl.BlockSpec((1, SUBLANES, tm), lambda i: (i, 0, 0)),
                   pl.BlockSpec((1, SUBLANES, LANES), lambda i: (i, 0, 0)),
                   pl.BlockSpec((1, LANES), lambda i: (0, 0))],
        out_shape=[jax.ShapeDtypeStruct((t, LANES), F32), jax.ShapeDtypeStruct((t, LANES), jnp.int32),
                   jax.ShapeDtypeStruct((n_tiles, SUBLANES, tm), jnp.int32),
                   jax.ShapeDtypeStruct((n_tiles, SUBLANES, LANES), jnp.int32),
                   jax.ShapeDtypeStruct((1, LANES), jnp.int32)],
        scratch_shapes=[pltpu.VMEM((1, LANES), F32)],
        compiler_params=_params("arbitrary"),
        name="moe_route",
    )(logits)


BIG_COPY_CHUNKS = 4


def _chunk_copy(src, src_row, dst, dst_row, sem, rows=SEG_ROWS):
    return pltpu.make_async_copy(src.at[pl.ds(pl.multiple_of(src_row, SEG_ROWS), rows)],
                                 dst.at[pl.ds(pl.multiple_of(dst_row, SEG_ROWS), rows)], sem)


def _for_each_copy(n_seg, seg_of, fn):
    big = BIG_COPY_CHUNKS * SEG_ROWS
    shift = BIG_COPY_CHUNKS.bit_length() - 1

    def per_expert(e, carry):
        lo, go, n = seg_of(e)
        n_big = lax.shift_right_logical(n, shift)

        def big_copy(c, carry2):
            fn(lo + c * big, go + c * big, big)
            return carry2

        def small_copy(c, carry2):
            fn(lo + n_big * big + c * SEG_ROWS, go + n_big * big + c * SEG_ROWS, SEG_ROWS)
            return carry2

        lax.fori_loop(0, n_big, big_copy, 0)
        lax.fori_loop(0, n - n_big * BIG_COPY_CHUNKS, small_copy, 0)
        return carry

    lax.fori_loop(0, n_seg, per_expert, 0)


def _for_each_segment_copy(tile, nch_ref, loc_ref, glob_ref, fn):
    def seg_of(e):
        s = tile * N_EXPERTS + e
        return loc_ref[s], glob_ref[s], nch_ref[s]

    _for_each_copy(N_EXPERTS, seg_of, fn)


SLAB_CHUNK = 256


def _slab_rows(tm):
    worst = tm * TOP_K + N_EXPERTS * (SEG_ROWS - 1)
    return -(-worst // SLAB_CHUNK) * SLAB_CHUNK


def _dispatch_kernel(nch_ref, loc_ref, glob_ref, gap_ref, gapn_ref, spt_ref, x_ref, xs_hbm, slab_sc, zero_sc, sem):
    i = pl.program_id(0)
    tm = x_ref.shape[0]
    slab_rows = slab_sc.shape[1]
    slot = lax.rem(i, 2)
    slab = slab_sc.at[slot]
    xb = x_ref[...].astype(BF16)
    sp = spt_ref[0]
    for c0 in range(0, slab_rows, SLAB_CHUNK):
        j = lax.broadcasted_iota(jnp.int32, (SLAB_CHUNK, tm), 0) + c0
        hit = j == sp[0:1, :]
        for k in range(1, TOP_K):
            hit = jnp.logical_or(hit, j == sp[k:k + 1, :])
        slab[c0:c0 + SLAB_CHUNK, :] = _dot(hit.astype(BF16), xb)

    def for_each_chunk(tile, fn):
        s = lax.rem(tile, 2)
        _for_each_segment_copy(tile, nch_ref, loc_ref, glob_ref,
                               lambda lo, go, rows: fn(_chunk_copy(slab_sc.at[s], lo, xs_hbm, go, sem.at[s], rows)))

    @pl.when(i > 0)
    def _():
        for_each_chunk(i - 1, lambda cp: cp.wait())

    for_each_chunk(i, lambda cp: cp.start())

    def for_each_gap_copy(fn):
        _for_each_copy(gap_ref.shape[0], lambda e: (0, gap_ref[e], gapn_ref[e]),
                       lambda lo, go, rows: fn(_chunk_copy(zero_sc, 0, xs_hbm, go, sem.at[2], rows)))

    @pl.when(i == pl.num_programs(0) - 1)
    def _():
        zero_sc[...] = jnp.zeros(zero_sc.shape, zero_sc.dtype)
        for_each_gap_copy(lambda cp: cp.start())
        for_each_chunk(i, lambda cp: cp.wait())
        for_each_gap_copy(lambda cp: cp.wait())


def _dispatch(n_chunks, slab_off, dest_off, gap_start, gap_chunks, spt, x1, n_pad):
    t, d = x1.shape
    n_tiles, _, tm = spt.shape
    slab_rows = _slab_rows(tm)
    grid_spec = pltpu.PrefetchScalarGridSpec(
        num_scalar_prefetch=5,
        grid=(n_tiles,),
        in_specs=[pl.BlockSpec((1, SUBLANES, tm), lambda i, *_: (i, 0, 0)),
                  pl.BlockSpec((tm, d), lambda i, *_: (i, 0))],
        out_specs=pl.BlockSpec(memory_space=pl.ANY),
        scratch_shapes=[pltpu.VMEM((2, slab_rows, d), F32), pltpu.VMEM((BIG_COPY_CHUNKS * SEG_ROWS, d), F32),
                        pltpu.SemaphoreType.DMA((3,))],
    )
    return pl.pallas_call(
        _dispatch_kernel,
        grid_spec=grid_spec,
        out_shape=jax.ShapeDtypeStruct((n_pad, d), F32),
        compiler_params=_params("arbitrary"),
        name="moe_dispatch",
    )(n_chunks, slab_off, dest_off, gap_start, gap_chunks, spt, x1)


def _expert_kernel(ie_ref, ix_ref, xs_ref, wup_ref, bup_ref, wdn_ref, bdn_ref, ys_ref, wup_sc, wdn_sc):
    w = pl.program_id(0)
    d_ff = wdn_ref.shape[1]
    compute = ix_ref[w] == w
    new_expert = jnp.logical_or(w == 0, ie_ref[w] != ie_ref[jnp.maximum(w - 1, 0)])

    @pl.when(jnp.logical_and(compute, new_expert))
    def _():
        wup_sc[...] = wup_ref[0].astype(BF16)
        wdn_sc[...] = wdn_ref[0].astype(BF16)

    @pl.when(jnp.logical_not(compute))
    def _():
        ys_ref[...] = jnp.zeros(ys_ref.shape, ys_ref.dtype)

    @pl.when(compute)
    def _():
        hb = _dot(xs_ref[...].astype(BF16), wup_sc[...]) + bup_ref[0]
        gl = jnp.minimum(hb[:, :d_ff], SWIGLU_LIMIT)
        up = jnp.clip(hb[:, d_ff:], -SWIGLU_LIMIT, SWIGLU_LIMIT)
        yb = (up + 1.0) * (gl * _sigmoid(SWIGLU_ALPHA * gl))
        ys_ref[...] = _dot(yb.astype(BF16), wdn_sc[...]) + bdn_ref[0]


def _experts(item_e, item_x, xs, w_up, b_up, w_down, b_down, layer, bm):
    n, d = xs.shape
    f2 = w_up.shape[3]
    xblk = lambda w, ie, ix: (ix[w], 0)
    exp3 = lambda w, ie, ix: (layer, ie[w], 0, 0)
    grid_spec = pltpu.PrefetchScalarGridSpec(
        num_scalar_prefetch=2,
        grid=(n // bm,),
        in_specs=[pl.BlockSpec((bm, d), xblk),
                  pl.BlockSpec((None, 1, d, f2), exp3), pl.BlockSpec((None, 1, 1, f2), exp3),
                  pl.BlockSpec((None, 1, f2 // 2, d), exp3), pl.BlockSpec((None, 1, 1, d), exp3)],
        out_specs=pl.BlockSpec((bm, d), lambda w, ie, ix: (w, 0)),
        scratch_shapes=[pltpu.VMEM((d, f2), BF16), pltpu.VMEM((f2 // 2, d), BF16)],
    )
    return pl.pallas_call(
        _expert_kernel,
        grid_spec=grid_spec,
        out_shape=jax.ShapeDtypeStruct((n, d), F32),
        compiler_params=_params("arbitrary"),
        name="moe_experts",
    )(item_e, item_x, xs, w_up, b_up, w_down, b_down)


def _post_moe_kernel(nch_ref, loc_ref, glob_ref, x1_ref, gate_ref, spos_ref, p_ref, g_ref, b_ref, wg_ref, wp_ref,
                     ys_hbm, out_ref, slab_sc, sem, *out_scratch, alpha, out_mode):
    i = pl.program_id(0)
    n_tiles = pl.num_programs(0)
    tm = x1_ref.shape[0]
    slab_rows = slab_sc.shape[1]

    def for_each_chunk(tile, fn):
        s = lax.rem(tile, 2)
        _for_each_segment_copy(tile, nch_ref, loc_ref, glob_ref,
                               lambda lo, go, rows: fn(_chunk_copy(ys_hbm, go, slab_sc.at[s], lo, sem.at[s], rows)))

    @pl.when(i == 0)
    def _():
        slab_sc[...] = jnp.zeros(slab_sc.shape, slab_sc.dtype)
        for_each_chunk(i, lambda cp: cp.start())

    @pl.when(i + 1 < n_tiles)
    def _():
        for_each_chunk(i + 1, lambda cp: cp.start())

    for_each_chunk(i, lambda cp: cp.wait())
    slab = slab_sc.at[lax.rem(i, 2)]
    gates = gate_ref[...]
    spos = spos_ref[...]
    ffn = jnp.zeros(x1_ref.shape, F32)
    for c0 in range(0, slab_rows, SLAB_CHUNK):
        col = lax.broadcasted_iota(jnp.int32, (tm, SLAB_CHUNK), 1) + c0
        weight = jnp.zeros((tm, SLAB_CHUNK), F32)
        for k in range(TOP_K):
            weight = jnp.where(col == spos[:, k:k + 1], gates[:, k:k + 1], weight)
        ffn = ffn + _dot(weight.astype(BF16), slab[c0:c0 + SLAB_CHUNK, :].astype(BF16))
    x2 = _layer_norm(alpha * x1_ref[...] + ffn, g_ref[...], b_ref[...])
    gate = _sigmoid(_dot(x2.astype(BF16), wg_ref[...]))
    batch, ts = p_ref.shape[0], p_ref.shape[1]
    p_bm = p_ref[...].reshape(tm, p_ref.shape[2]).astype(BF16)
    tok = lax.broadcasted_iota(jnp.int32, (tm, tm), 0)
    src = lax.broadcasted_iota(jnp.int32, (tm, tm), 1)
    shift = batch.bit_length() - 1
    pick = src == jnp.bitwise_and(tok, batch - 1) * ts + lax.shift_right_logical(tok, shift)
    p_tok = _dot(pick.astype(BF16), p_bm).astype(BF16)
    out = x2 + gate * _dot(p_tok, wp_ref[...])
    if out_mode == "rows":
        out_ref[...] = out
        return
    obuf, osem = out_scratch
    d = out.shape[1]

    def batch_copy(bb):
        if out_mode == "batch_major":
            dst = out_ref.at[bb, pl.ds(i * ts, ts)]
        else:
            dst = out_ref.at[pl.ds(i * ts, ts), pl.ds(bb * d, d)]
        return pltpu.make_async_copy(obuf.at[:, bb], dst, osem)

    @pl.when(i > 0)
    def _():
        for bb in range(batch):
            batch_copy(bb).wait()

    obuf[...] = out.reshape(ts, batch, d)
    for bb in range(batch):
        batch_copy(bb).start()

    @pl.when(i == n_tiles - 1)
    def _():
        for bb in range(batch):
            batch_copy(bb).wait()


def _post_moe(n_chunks, slab_off, dest_off, x1, gates, spos, ys, p, layer, g, b, wg, wp, alpha, batch, tm,
              out_mode):
    t, d = x1.shape
    ts = tm // batch
    seq = t // batch
    row = lambda c: pl.BlockSpec((tm, c), lambda i, *_: (i, 0))
    scratch = [pltpu.VMEM((2, _slab_rows(tm), d), F32), pltpu.SemaphoreType.DMA((2,))]
    if out_mode == "rows":
        out_spec, out_shape = row(d), (t, d)
    else:
        out_spec = pl.BlockSpec(memory_space=pl.ANY)
        out_shape = (batch, seq, d) if out_mode == "batch_major" else (seq, batch * d)
        scratch += [pltpu.VMEM((ts, batch, d), F32), pltpu.SemaphoreType.DMA(())]
    grid_spec = pltpu.PrefetchScalarGridSpec(
        num_scalar_prefetch=3,
        grid=(t // tm,),
        in_specs=[row(d), row(LANES), row(LANES),
                  pl.BlockSpec((None, batch, ts, p.shape[-1]), lambda i, *_: (layer, 0, i, 0)),
                  _full(g.shape), _full(b.shape), _full(wg.shape), _full(wp.shape),
                  pl.BlockSpec(memory_space=pl.ANY)],
        out_specs=out_spec,
        scratch_shapes=scratch,
    )
    return pl.pallas_call(
        functools.partial(_post_moe_kernel, alpha=alpha, out_mode=out_mode),
        grid_spec=grid_spec,
        out_shape=jax.ShapeDtypeStruct(out_shape, F32),
        compiler_params=_params("arbitrary"),
        name="post_moe_ple",
    )(n_chunks, slab_off, dest_off, x1, gates, spos, p, g, b, wg, wp, ys)


def _moe_plan(counts, n_pad, bm):
    i32 = jnp.int32
    n_blocks = n_pad // bm
    region = (counts + bm - 1) // bm * bm
    region_end = jnp.cumsum(region)
    starts = region_end - region
    total = region_end[-1]
    w = jnp.arange(n_blocks, dtype=i32)
    is_compute = w * bm < total
    item_e = jnp.sum((w[:, None] * bm >= region_end[None, :]).astype(i32), axis=1)
    item_e = jnp.minimum(item_e, N_EXPERTS - 1).astype(i32)
    item_x = jnp.where(is_compute, w, 0).astype(i32)
    gap_start = jnp.concatenate([starts + counts, total[None]]).astype(i32)
    gap_chunks = (jnp.concatenate([region - counts, (n_pad - total)[None]]) // SEG_ROWS).astype(i32)
    return starts, gap_start, gap_chunks, item_e, item_x


def _moe_dispatch(x1, logits, w_up, b_up, w_down, b_down, layer, tm, bm):
    t, d = x1.shape
    n_tiles = t // tm
    worst = t * TOP_K + n_tiles * N_EXPERTS * (SEG_ROWS - 1) + N_EXPERTS * (bm - SEG_ROWS)
    n_pad = -(-worst // bm) * bm
    gates, spos, spt, seg, counts = _route(logits, tm)
    starts, gap_start, gap_chunks, item_e, item_x = _moe_plan(counts[0, :N_EXPERTS], n_pad, bm)
    n_chunks = seg[:, 0, :N_EXPERTS].reshape(-1)
    slab_off = seg[:, 1, :N_EXPERTS].reshape(-1)
    dest_off = (starts[None, :] + seg[:, 2, :N_EXPERTS]).reshape(-1).astype(jnp.int32)
    xs = _dispatch(n_chunks, slab_off, dest_off, gap_start, gap_chunks, spt, x1, n_pad)
    ys = _experts(item_e, item_x, xs, w_up, b_up, w_down, b_down, layer, bm)
    return (n_chunks, slab_off, dest_off), gates, spos, ys


def _pick_tile(n, target, quantum):
    tile = min(n, target)
    while n % tile or tile % quantum:
        tile -= quantum
    return tile


def kernel(x, p, positions, mla_w_in, mla_q_norm, mla_kv_norm, mla_w_uq, mla_w_ukv, mla_w_o, lru_w_in, lru_conv_w, lru_conv_b, lru_w_a, lru_b_a, lru_w_x, lru_b_x, lru_lambda, lru_w_out, ln1_g, ln1_b, ln2_g, ln2_b, moe_w_router, moe_b_router, moe_w_up, moe_b_up, moe_w_down, moe_b_down, ple_w_gate, ple_w_proj):
    batch, seq, d = x.shape
    depth = ln1_g.shape[0]
    t = batch * seq
    assert batch == SUBLANES, "the recurrence kernel maps the batch onto the sublanes of a vreg"
    alpha = (2.0 * depth) ** 0.25
    tm = _pick_tile(t, 512, SUBLANES * batch)
    attn_blk = _pick_tile(seq, 512, LANES)
    moe_bm = _pick_tile(t * TOP_K, 512, SUBLANES)

    xt = x
    half = QK_ROPE_DIM // 2
    inv_freq = jnp.exp(-math.log(ROPE_THETA) * jnp.arange(half, dtype=F32) / half)
    rope_freq = jnp.concatenate([inv_freq, inv_freq, jnp.zeros((LANES - QK_ROPE_DIM,), F32)]).reshape(1, LANES)
    pos_lanes = jnp.broadcast_to(jnp.transpose(positions).astype(F32)[:, :, None],
                                 (seq, batch, LANES)).reshape(seq, batch * LANES)

    row2 = lambda a: a.reshape(1, -1)
    scale = math.log2(math.e) / math.sqrt(QK_NOPE_DIM + QK_ROPE_DIM)

    for layer in range(depth):
        j = layer // 2
        wr = jnp.pad(moe_w_router[layer], ((0, 0), (0, LANES - N_EXPERTS)))
        wr_hi, wr_lo = _split_bf16(wr)
        br = row2(jnp.pad(moe_b_router[layer], (0, LANES - N_EXPERTS)))
        g1, b1 = row2(ln1_g[layer]), row2(ln1_b[layer])
        if layer % 2 == 0:
            q_lora, kv_lora = mla_q_norm.shape[1], mla_kv_norm.shape[1]
            w_in = mla_w_in[j]
            win_p = jnp.pad(w_in, ((0, 0), (0, LANES - QK_ROPE_DIM))).astype(BF16)
            wuq = mla_w_uq[j].reshape(q_lora, MLA_HEADS, QK_NOPE_DIM + QK_ROPE_DIM)
            wuq_p = jnp.pad(wuq, ((0, 0), (0, 0), (0, QK_PAD - QK_NOPE_DIM - QK_ROPE_DIM)))
            wuq_p = wuq_p.reshape(q_lora, MLA_HEADS * QK_PAD).astype(BF16)
            q, k, vt = _mla_proj(xt, pos_lanes, rope_freq, win_p,
                                 row2(mla_q_norm[j] * scale), row2(mla_kv_norm[j]),
                                 wuq_p, mla_w_ukv[j].astype(BF16), seq, batch, attn_blk)
            o = _attention(q, k, vt, seq, batch, attn_blk, 4)
            x1, logits = _post_attn(xt, o, mla_w_o[j].astype(BF16), g1, b1,
                                    wr_hi, wr_lo, br, alpha, seq, batch, attn_blk)
        else:
            x1, logits = _lru_block(
                xt.reshape(t, d), lru_w_in[j].astype(BF16), lru_conv_w[j], row2(lru_conv_b[j]),
                lru_w_a[j].astype(BF16), row2(lru_b_a[j]), lru_w_x[j].astype(BF16), row2(lru_b_x[j]),
                row2(lru_lambda[j]), lru_w_out[j].astype(BF16), g1, b1, wr_hi, wr_lo, br, alpha, batch, tm)
        x1 = x1.reshape(t, d)
        tables, gates, spos, ys = _moe_dispatch(x1, logits.reshape(t, LANES), moe_w_up,
                                                moe_b_up[:, :, None, :], moe_w_down, moe_b_down[:, :, None, :],
                                                layer, tm, moe_bm)
        if layer + 1 == depth:
            out_mode = "batch_major"
        elif (layer + 1) % 2 == 0:
            out_mode = "columns"
        else:
            out_mode = "rows"
        xt = _post_moe(*tables, x1, gates, spos, ys, p, layer, row2(ln2_g[layer]), row2(ln2_b[layer]),
                       ple_w_gate[layer].astype(BF16), ple_w_proj[layer].astype(BF16), alpha, batch, tm,
                       out_mode)
    return xt
```
